```python
import math
import jax, jax.numpy as jnp
from jax import lax
import numpy as np

D_MODEL = 1024
BATCH = 16
SEQ = 256
DEPTH = 2
DEC_BATCH = 2
DEC_SEQ = 2048
PAST_LEN = 256

GRID_W = 64
Q_BLOCK = 128
ROPE_THETA = 10000.0
NORM_EPS = 1e-6

DIFF_HEADS = 4
DIFF_HEAD_DIM = 64
DIFF_V_DIM = 2 * DIFF_HEAD_DIM
DIFF_QK_WIDTH = DIFF_HEADS * 2 * DIFF_HEAD_DIM
DIFF_WIDTH = DIFF_HEADS * DIFF_V_DIM

GQA_HEADS = 8
GQA_KV_HEADS = 2
GQA_GROUP = GQA_HEADS // GQA_KV_HEADS
GQA_HEAD_DIM = 64
GQA_WIDTH = GQA_HEADS * GQA_HEAD_DIM
GQA_KV_WIDTH = GQA_KV_HEADS * GQA_HEAD_DIM

CONV_WIDTH = 512
CONV_KSIZE = 31

N_BRANCH = 3
MLP_HIDDEN = 4 * D_MODEL
N_MOD = 6

IN_SIZES = (DIFF_QK_WIDTH, DIFF_QK_WIDTH, DIFF_WIDTH,
            GQA_WIDTH, GQA_KV_WIDTH, GQA_KV_WIDTH,
            2 * CONV_WIDTH, N_BRANCH * D_MODEL)
IN_WIDTH = sum(IN_SIZES)

kernel_name = "hybrid_diff_gqa_conformer_dit_step"


def rms_norm(x, g):
    xf = x.astype(jnp.float32)
    y = xf * lax.rsqrt(jnp.mean(xf * xf, axis=-1, keepdims=True) + NORM_EPS)
    return (y * g.astype(jnp.float32)).astype(x.dtype)


def layer_norm(x, g, b):
    xf = x.astype(jnp.float32)
    mu = jnp.mean(xf, axis=-1, keepdims=True)
    xc = xf - mu
    y = xc * lax.rsqrt(jnp.mean(xc * xc, axis=-1, keepdims=True) + NORM_EPS)
    return (y * g.astype(jnp.float32) + b.astype(jnp.float32)).astype(x.dtype)


def axial_rope_tables(n_tokens, dim):
    n_rows = n_tokens // GRID_W
    row = jnp.repeat(jnp.arange(n_rows), GRID_W).astype(jnp.float32)
    col = jnp.tile(jnp.arange(GRID_W), n_rows).astype(jnp.float32)
    axis_dim = dim // 2
    freqs = ROPE_THETA ** (-jnp.arange(0, axis_dim, 2, dtype=jnp.float32) / axis_dim)
    ang_r = row[:, None] * freqs[None, :]
    ang_c = col[:, None] * freqs[None, :]
    ang = jnp.concatenate([ang_r, ang_r, ang_c, ang_c], axis=-1)
    return jnp.cos(ang), jnp.sin(ang)


def apply_axial_rope(x, rope):
    cos, sin = rope
    x1, x2, x3, x4 = jnp.split(x, 4, axis=-1)
    rot = jnp.concatenate([-x2, x1, -x4, x3], axis=-1)
    shape = (1, x.shape[1]) + (1,) * (x.ndim - 3) + (x.shape[-1],)
    return x * cos.reshape(shape).astype(x.dtype) + rot * sin.reshape(shape).astype(x.dtype)


def sweep_query_blocks(fn, q):
    b, s = q.shape[0], q.shape[1]
    nb = s // Q_BLOCK
    qb = jnp.moveaxis(q.reshape((b, nb, Q_BLOCK) + q.shape[2:]), 1, 0)
    out = lax.map(fn, qb)
    out = jnp.moveaxis(out, 0, 1)
    return out.reshape((b, s) + out.shape[3:])


def differential_attention(q, k, v, lam):
    scale = DIFF_HEAD_DIM ** -0.5
    kf = k.astype(jnp.float32)
    vf = v.astype(jnp.float32)

    def block(qb):
        s = jnp.einsum('bqhcd,bkhcd->bhcqk', qb.astype(jnp.float32), kf) * scale
        p = jax.nn.softmax(s, axis=-1)
        a = p[:, :, 0] - lam * p[:, :, 1]
        return jnp.einsum('bhqk,bkhe->bqhe', a, vf).astype(q.dtype)

    return sweep_query_blocks(block, q)


def grouped_query_attention(q, k, v):
    scale = GQA_HEAD_DIM ** -0.5
    kf = k.astype(jnp.float32)
    vf = v.astype(jnp.float32)

    def block(qb):
        s = jnp.einsum('bqngd,bknd->bngqk', qb.astype(jnp.float32), kf) * scale
        p = jax.nn.softmax(s, axis=-1)
        return jnp.einsum('bngqk,bknd->bqngd', p, vf).astype(q.dtype)

    return sweep_query_blocks(block, q)


def split_points():
    pts, acc = [], 0
    for s in IN_SIZES[:-1]:
        acc += s
        pts.append(acc)
    return pts


def parallel_mixer(h, l, P, ctx, rope_d, rope_g):
    b, s, _ = h.shape
    proj = h @ P['w_in'][l]
    dq, dk, dv, gq, gk, gv, cv, gl = jnp.split(proj, split_points(), axis=-1)

    dq = dq.reshape(b, s, DIFF_HEADS, 2, DIFF_HEAD_DIM)
    dk = dk.reshape(b, s, DIFF_HEADS, 2, DIFF_HEAD_DIM)
    dv = dv.reshape(b, s, DIFF_HEADS, DIFF_V_DIM)
    gq = rms_norm(gq.reshape(b, s, GQA_KV_HEADS, GQA_GROUP, GQA_HEAD_DIM), P['gqa_q_norm'][l])
    gk = rms_norm(gk.reshape(b, s, GQA_KV_HEADS, GQA_HEAD_DIM), P['gqa_k_norm'][l])
    gv = gv.reshape(b, s, GQA_KV_HEADS, GQA_HEAD_DIM)
    own_ctx = (dk, dv, gk, gv)

    if ctx is None:
        dk_all, dv_all, gk_all, gv_all = dk, dv, gk, gv
    else:
        dq = apply_axial_rope(dq, rope_d)
        gq = apply_axial_rope(gq, rope_g)
        dk_all = jnp.concatenate([apply_axial_rope(dk, rope_d), ctx[0].astype(dk.dtype)], axis=1)
        dv_all = jnp.concatenate([dv, ctx[1].astype(dv.dtype)], axis=1)
        gk_all = jnp.concatenate([apply_axial_rope(gk, rope_g), ctx[2].astype(gk.dtype)], axis=1)
        gv_all = jnp.concatenate([gv, ctx[3].astype(gv.dtype)], axis=1)

    lam_init = 0.8 - 0.6 * math.exp(-0.3 * l)
    f32 = jnp.float32
    lam = (jnp.exp(jnp.sum(P['diff_lq1'][l].astype(f32) * P['diff_lk1'][l].astype(f32)))
           - jnp.exp(jnp.sum(P['diff_lq2'][l].astype(f32) * P['diff_lk2'][l].astype(f32)))
           + lam_init)
    da = differential_attention(dq, dk_all, dv_all, lam)
    da = rms_norm(da, P['diff_subln'][l]) * (1.0 - lam_init)
    branch_a = da.reshape(b, s, DIFF_WIDTH) @ P['w_diff_o'][l]

    ga = grouped_query_attention(gq, gk_all, gv_all)
    branch_b = ga.reshape(b, s, GQA_WIDTH) @ P['w_gqa_o'][l]

    u = cv[..., :CONV_WIDTH] * jax.nn.sigmoid(cv[..., CONV_WIDTH:])
    kern = P['conv_dw'][l][:, None, :].astype(u.dtype)
    u = lax.conv_general_dilated(u, kern, window_strides=(1,),
                                 padding=[(CONV_KSIZE // 2, CONV_KSIZE // 2)],
                                 dimension_numbers=('NWC', 'WIO', 'NWC'),
                                 feature_group_count=CONV_WIDTH)
    u = u + P['conv_dw_b'][l]
    u = jax.nn.silu(layer_norm(u, P['conv_ln_g'][l], P['conv_ln_b'][l]))
    branch_c = u @ P['w_conv_o'][l]

    g = jax.nn.sigmoid(gl.reshape(b, s, N_BRANCH, D_MODEL))
    merged = g[:, :, 0] * branch_a + g[:, :, 1] * branch_b + g[:, :, 2] * branch_c
    return merged @ P['w_o'][l], own_ctx


def trunk_layer(x, mod, l, P, ctx, rope_d, rope_g):
    shift1, scale1, gate1, shift2, scale2, gate2 = jnp.split(mod, N_MOD, axis=-1)
    h = rms_norm(x, P['norm1'][l]) * (1.0 + scale1) + shift1
    m, own_ctx = parallel_mixer(h, l, P, ctx, rope_d, rope_g)
    x = x + gate1 * m
    h = rms_norm(x, P['norm2'][l]) * (1.0 + scale2) + shift2
    f = jnp.square(jax.nn.relu(h @ P['w_mlp1'][l])) @ P['w_mlp2'][l]
    x = x + gate2 * f
    return x, own_ctx


def setup_inputs(seed: int = 0) -> dict:
    key = jax.random.key(seed)
    ks = jax.random.split(key, 32)
    f32 = jnp.float32

    def nrm(k, shape, scale):
        return jax.random.normal(k, shape, f32) * scale

    return {
        'x_prompt': nrm(ks[0], (BATCH, SEQ, D_MODEL), 1.0),
        'x_sample': nrm(ks[1], (DEC_BATCH, DEC_SEQ, D_MODEL), 1.0),
        'cache_diff_k': nrm(ks[2], (DEC_BATCH, DEPTH, PAST_LEN, DIFF_HEADS, 2, DIFF_HEAD_DIM), 1.0),
        'cache_diff_v': nrm(ks[3], (DEC_BATCH, DEPTH, PAST_LEN, DIFF_HEADS, DIFF_V_DIM), 1.0),
        'cache_gqa_k': nrm(ks[4], (DEC_BATCH, DEPTH, PAST_LEN, GQA_KV_HEADS, GQA_HEAD_DIM), 1.0),
        'cache_gqa_v': nrm(ks[5], (DEC_BATCH, DEPTH, PAST_LEN, GQA_KV_HEADS, GQA_HEAD_DIM), 1.0),
        'c': nrm(ks[6], (DEC_BATCH, D_MODEL), 1.0),
        'c_ctx': nrm(ks[7], (D_MODEL,), 1.0),
        'w_ada': nrm(ks[8], (DEPTH, D_MODEL, N_MOD * D_MODEL), 0.5 * D_MODEL ** -0.5),
        'b_ada': nrm(ks[9], (DEPTH, N_MOD * D_MODEL), 0.02),
        'norm1': 1.0 + nrm(ks[10], (DEPTH, D_MODEL), 0.02),
        'norm2': 1.0 + nrm(ks[11], (DEPTH, D_MODEL), 0.02),
        'w_in': nrm(ks[12], (DEPTH, D_MODEL, IN_WIDTH), D_MODEL ** -0.5),
        'diff_lq1': nrm(ks[13], (DEPTH, DIFF_HEAD_DIM), 0.1),
        'diff_lk1': nrm(ks[14], (DEPTH, DIFF_HEAD_DIM), 0.1),
        'diff_lq2': nrm(ks[15], (DEPTH, DIFF_HEAD_DIM), 0.1),
        'diff_lk2': nrm(ks[16], (DEPTH, DIFF_HEAD_DIM), 0.1),
        'diff_subln': 1.0 + nrm(ks[17], (DEPTH, DIFF_V_DIM), 0.02),
        'w_diff_o': nrm(ks[18], (DEPTH, DIFF_WIDTH, D_MODEL), DIFF_WIDTH ** -0.5),
        'gqa_q_norm': 1.0 + nrm(ks[19], (DEPTH, GQA_HEAD_DIM), 0.02),
        'gqa_k_norm': 1.0 + nrm(ks[20], (DEPTH, GQA_HEAD_DIM), 0.02),
        'w_gqa_o': nrm(ks[21], (DEPTH, GQA_WIDTH, D_MODEL), GQA_WIDTH ** -0.5),
        'conv_dw': nrm(ks[22], (DEPTH, CONV_KSIZE, CONV_WIDTH), CONV_KSIZE ** -0.5),
        'conv_dw_b': nrm(ks[23], (DEPTH, CONV_WIDTH), 0.02),
        'conv_ln_g': 1.0 + nrm(ks[24], (DEPTH, CONV_WIDTH), 0.02),
        'conv_ln_b': nrm(ks[25], (DEPTH, CONV_WIDTH), 0.02),
        'w_conv_o': nrm(ks[26], (DEPTH, CONV_WIDTH, D_MODEL), CONV_WIDTH ** -0.5),
        'w_o': nrm(ks[27], (DEPTH, D_MODEL, D_MODEL), D_MODEL ** -0.5),
        'w_mlp1': nrm(ks[28], (DEPTH, D_MODEL, MLP_HIDDEN), D_MODEL ** -0.5),
        'w_mlp2': nrm(ks[29], (DEPTH, MLP_HIDDEN, D_MODEL), MLP_HIDDEN ** -0.5),
        'final_norm': 1.0 + nrm(ks[30], (D_MODEL,), 0.02),
    }


def reference(x_prompt, x_sample, cache_diff_k, cache_diff_v, cache_gqa_k, cache_gqa_v, c, c_ctx,
              w_ada, b_ada, norm1, norm2, w_in, diff_lq1, diff_lk1, diff_lq2, diff_lk2, diff_subln,
              w_diff_o, gqa_q_norm, gqa_k_norm, w_gqa_o, conv_dw, conv_dw_b, conv_ln_g, conv_ln_b,
              w_conv_o, w_o, w_mlp1, w_mlp2, final_norm):
    P = dict(norm1=norm1, norm2=norm2, w_in=w_in, diff_lq1=diff_lq1, diff_lk1=diff_lk1,
             diff_lq2=diff_lq2, diff_lk2=diff_lk2, diff_subln=diff_subln, w_diff_o=w_diff_o,
             gqa_q_norm=gqa_q_norm, gqa_k_norm=gqa_k_norm, w_gqa_o=w_gqa_o, conv_dw=conv_dw,
             conv_dw_b=conv_dw_b, conv_ln_g=conv_ln_g, conv_ln_b=conv_ln_b, w_conv_o=w_conv_o,
             w_o=w_o, w_mlp1=w_mlp1, w_mlp2=w_mlp2)

    x = x_prompt
    ctx_layers = []
    for l in range(DEPTH):
        mod = (jax.nn.silu(c_ctx) @ w_ada[l] + b_ada[l])[None, None, :]
        x, own_ctx = trunk_layer(x, mod, l, P, None, None, None)
        ctx_layers.append(own_ctx)
    y_prompt = rms_norm(x, final_norm)
    new_diff_k = jnp.stack([t[0] for t in ctx_layers], axis=1)
    new_diff_v = jnp.stack([t[1] for t in ctx_layers], axis=1)
    new_gqa_k = jnp.stack([t[2] for t in ctx_layers], axis=1)
    new_gqa_v = jnp.stack([t[3] for t in ctx_layers], axis=1)

    n_lat = x_sample.shape[1]
    rope_d = axial_rope_tables(n_lat, DIFF_HEAD_DIM)
    rope_g = axial_rope_tables(n_lat, GQA_HEAD_DIM)
    x = x_sample
    for l in range(DEPTH):
        mod = (jax.nn.silu(c) @ w_ada[l] + b_ada[l])[:, None, :]
        ctx = (cache_diff_k[:, l], cache_diff_v[:, l], cache_gqa_k[:, l], cache_gqa_v[:, l])
        x, _ = trunk_layer(x, mod, l, P, ctx, rope_d, rope_g)
    y_sample = rms_norm(x, final_norm)

    return (y_prompt, y_sample, new_diff_k, new_diff_v, new_gqa_k, new_gqa_v)
```

```python
import functools
import math

import jax
import jax.numpy as jnp
from jax import lax
from jax.experimental import pallas as pl
from jax.experimental.pallas import tpu as pltpu

D_MODEL = 1024
DEPTH = 2
GRID_W = 64
ROPE_THETA = 10000.0
NORM_EPS = 1e-6

DIFF_HEADS = 4
HEAD_DIM = 64
DIFF_WIDTH = 512
GQA_KV_HEADS = 2
GQA_WIDTH = 512
GQA_KV_WIDTH = 128
CONV_WIDTH = 512
CONV_KSIZE = 31
CONV_HALO = 16
N_BRANCH = 3
MLP_HIDDEN = 4 * D_MODEL
N_MOD = 6

MIX_WIDTH = 3 * 512 + 512 + 2 * 128 + 2 * CONV_WIDTH
GATE_WIDTH = N_BRANCH * D_MODEL

LANES = 128
MOD_ROWS = 8
VMEM_LIMIT = 56 * 1024 * 1024

F32 = jnp.float32
BF16 = jnp.bfloat16


def _dot(a, b):
    return jnp.dot(a, b, preferred_element_type=F32)


def _dot_nt(a, b):
    return lax.dot_general(a, b, (((1,), (1,)), ((), ())), preferred_element_type=F32)


def _rms(x, gain):
    return x * lax.rsqrt(jnp.mean(x * x, axis=-1, keepdims=True) + NORM_EPS) * gain


def _modulated_norm(x, gain, shift, scale):
    return (_rms(x, gain) * (1.0 + scale) + shift).astype(BF16)


def _lane_iota(shape):
    return lax.broadcasted_iota(jnp.int32, shape, len(shape) - 1)


def _resident(shape):
    nd = len(shape)
    return pl.BlockSpec(shape, lambda *_: (0,) * nd, pipeline_mode=pl.Buffered(1))


def _params(n_axes):
    return pltpu.CompilerParams(dimension_semantics=("arbitrary",) * n_axes,
                                vmem_limit_bytes=VMEM_LIMIT)


def _mod_kernel(c_ref, w_ref, b_ref, o_ref):
    c = c_ref[...]
    s = (c * jax.nn.sigmoid(c)).astype(BF16)
    o_ref[0] = _dot(s, w_ref[0].astype(BF16)) + b_ref[0]


def _modulation(cvecs, w_ada, b_ada):
    width = N_MOD * D_MODEL
    tn = 1536
    return pl.pallas_call(
        _mod_kernel,
        grid=(DEPTH, width // tn),
        in_specs=[pl.BlockSpec((MOD_ROWS, D_MODEL), lambda l, j: (0, 0)),
                  pl.BlockSpec((1, D_MODEL, tn), lambda l, j: (l, 0, j)),
                  pl.BlockSpec((1, 1, tn), lambda l, j: (l, 0, j))],
        out_specs=pl.BlockSpec((1, MOD_ROWS, tn), lambda l, j: (l, 0, j)),
        out_shape=jax.ShapeDtypeStruct((DEPTH, MOD_ROWS, width), F32),
        compiler_params=_params(2),
        name="adaln_mod",
    )(cvecs, w_ada, b_ada.reshape(DEPTH, 1, width))


def _rope(x, cos, sin_signed, first_half):
    rot = jnp.where(first_half, pltpu.roll(x, LANES - 16, 1), pltpu.roll(x, 16, 1))
    return x * cos + rot * sin_signed


def _head_mean_sq(x, blockdiag):
    sq = x * x
    hi = sq.astype(BF16)
    lo = (sq - hi.astype(F32)).astype(BF16)
    return _dot(hi, blockdiag) + _dot(lo, blockdiag)


def _pre_kernel(*refs, latent):
    if latent:
        (x_ref, mod_ref, n1_ref, w_ref, qn_ref, kn_ref, cos_ref, sin_ref,
         qd_ref, kd_ref, vd_ref, qg_ref, kk_ref, vv_ref, u_ref) = refs
    else:
        (x_ref, mod_ref, n1_ref, w_ref, qn_ref, kn_ref,
         qd_ref, kd_ref, vd_ref, qg_ref, kk_ref, vv_ref, u_ref,
         ndk_ref, ndv_ref, ngk_ref, ngv_ref) = refs

    mod = mod_ref[0, 0]
    h = _modulated_norm(x_ref[...], n1_ref[...], mod[:, 0:D_MODEL], mod[:, D_MODEL:2 * D_MODEL])
    rows = h.shape[0]

    lane = _lane_iota((rows, LANES))
    low_half = lane < HEAD_DIM
    if latent:
        cos, sin_signed = cos_ref[...], sin_ref[...]
        first_half = ((lane % HEAD_DIM) // 16) % 2 == 0
        rope = lambda t: _rope(t, cos, sin_signed, first_half)
    else:
        rope = lambda t: t

    r_i = lax.broadcasted_iota(jnp.int32, (LANES, LANES), 0) // HEAD_DIM
    c_i = lax.broadcasted_iota(jnp.int32, (LANES, LANES), 1) // HEAD_DIM
    blockdiag = jnp.where(r_i == c_i, 1.0 / HEAD_DIM, 0.0).astype(BF16)
    qk_scale = HEAD_DIM ** -0.5

    def slab(col):
        return _dot(h, w_ref[:, col:col + LANES])

    for s in range(DIFF_WIDTH // LANES):
        lo = s * LANES
        q = slab(lo)
        k = slab(512 + lo)
        v = slab(1024 + lo)
        if not latent:
            ndk_ref[:, lo:lo + LANES] = k
            ndv_ref[:, lo:lo + LANES] = v
        qd_ref[:, lo:lo + LANES] = (rope(q) * qk_scale).astype(BF16)
        kd_ref[:, lo:lo + LANES] = rope(k).astype(BF16)
        vd_ref[:, lo:lo + LANES] = v.astype(BF16)

    qn, kn = qn_ref[...], kn_ref[...]
    for s in range(GQA_WIDTH // LANES):
        lo = s * LANES
        q = slab(1536 + lo)
        q = q * lax.rsqrt(_head_mean_sq(q, blockdiag) + NORM_EPS) * qn
        qg_ref[:, lo:lo + LANES] = (rope(q) * qk_scale).astype(BF16)

    k = slab(2048)
    k = k * lax.rsqrt(_head_mean_sq(k, blockdiag) + NORM_EPS) * kn
    v = slab(2048 + GQA_KV_WIDTH)
    if not latent:
        ngk_ref[...] = k
        ngv_ref[...] = v
    k = rope(k)
    k_sw, v_sw = pltpu.roll(k, HEAD_DIM, 1), pltpu.roll(v, HEAD_DIM, 1)
    kk_ref[:, 0:LANES] = jnp.where(low_half, k, k_sw).astype(BF16)
    kk_ref[:, LANES:2 * LANES] = jnp.where(low_half, k_sw, k).astype(BF16)
    vv_ref[:, 0:LANES] = jnp.where(low_half, v, v_sw).astype(BF16)
    vv_ref[:, LANES:2 * LANES] = jnp.where(low_half, v_sw, v).astype(BF16)

    base = 2048 + 2 * GQA_KV_WIDTH
    for s in range(CONV_WIDTH // LANES):
        lo = s * LANES
        u_ref[:, lo:lo + LANES] = slab(base + lo) * jax.nn.sigmoid(slab(base + CONV_WIDTH + lo))


def _pre(x2d, mod, layer, norm1, w_mix, qn, kn, rope_tabs, *, latent, seq, tile):
    n_tok = x2d.shape[0]
    tiles_per_seq = seq // tile if latent else 1
    row = (lambda i: 1 + i // tiles_per_seq) if latent else (lambda i: 0)
    tok = lambda w: pl.BlockSpec((tile, w), lambda i: (i, 0))
    in_specs = [tok(D_MODEL),
                pl.BlockSpec((1, 1, 1, N_MOD * D_MODEL), lambda i: (layer, row(i), 0, 0)),
                _resident((1, D_MODEL)), _resident((D_MODEL, MIX_WIDTH)),
                _resident((1, LANES)), _resident((1, LANES))]
    args = [x2d, mod, norm1, w_mix, qn, kn]
    if latent:
        in_specs += [pl.BlockSpec((tile, LANES), lambda i: (i % tiles_per_seq, 0))] * 2
        args += list(rope_tabs)
    widths = [(512, BF16), (512, BF16), (512, BF16), (512, BF16), (256, BF16), (256, BF16), (512, F32)]
    if not latent:
        widths += [(512, F32), (512, F32), (128, F32), (128, F32)]
    return pl.pallas_call(
        functools.partial(_pre_kernel, latent=latent),
        grid=(n_tok // tile,),
        in_specs=in_specs,
        out_specs=[tok(w) for w, _ in widths],
        out_shape=[jax.ShapeDtypeStruct((n_tok, w), dt) for w, dt in widths],
        compiler_params=_params(1),
        name="pre_latent" if latent else "pre_ctx",
    )(*args)


def _softmax_terms(q, keys):
    scores = [_dot_nt(q, k) for k in keys]
    m = functools.reduce(jnp.maximum, [jnp.max(s, axis=-1, keepdims=True) for s in scores])
    exps = [jnp.exp(s - m) for s in scores]
    denom = functools.reduce(jnp.add, [jnp.sum(e, axis=-1, keepdims=True) for e in exps])
    return exps, denom


def _attend(q, keys, values):
    exps, denom = _softmax_terms(q, keys)
    out = functools.reduce(jnp.add, [_dot(e.astype(BF16), v) for e, v in zip(exps, values)])
    return out / denom


def _attn_kernel(*refs, latent, lam_init):
    if latent:
        (qd_ref, kd_ref, vd_ref, qg_ref, kk_ref, vv_ref, cdk_ref, cdv_ref, cgk_ref, cgv_ref,
         lp_ref, sub_ref, da_ref, ga_ref) = refs
    else:
        qd_ref, kd_ref, vd_ref, qg_ref, kk_ref, vv_ref, lp_ref, sub_ref, da_ref, ga_ref = refs

    rows = qd_ref.shape[0]
    low_half = _lane_iota((rows, LANES)) < HEAD_DIM
    zero = jnp.zeros((), BF16)

    lp = lp_ref[...]
    lam = (jnp.exp(jnp.sum(lp[0:1] * lp[1:2], axis=-1, keepdims=True))
           - jnp.exp(jnp.sum(lp[2:3] * lp[3:4], axis=-1, keepdims=True)) + lam_init)
    sub_gain = sub_ref[...] * (1.0 - lam_init)

    for h in range(DIFF_HEADS):
        sl = slice(h * LANES, (h + 1) * LANES)
        q = qd_ref[:, sl]
        keys, values = [kd_ref[:, sl]], [vd_ref[:, sl]]
        if latent:
            keys.append(cdk_ref[0, 0, :, sl].astype(BF16))
            values.append(cdv_ref[0, 0, :, sl].astype(BF16))
        o1 = _attend(jnp.where(low_half, q, zero), keys, values)
        o2 = _attend(jnp.where(low_half, zero, q), keys, values)
        da_ref[:, sl] = _rms(o1 - lam * o2, sub_gain).astype(BF16)

    if latent:
        ck, cv = cgk_ref[0, 0], cgv_ref[0, 0]
        ck_sw, cv_sw = pltpu.roll(ck, HEAD_DIM, 1), pltpu.roll(cv, HEAD_DIM, 1)
        low_c = _lane_iota(ck.shape) < HEAD_DIM
        cache_k = [jnp.where(low_c, ck, ck_sw).astype(BF16), jnp.where(low_c, ck_sw, ck).astype(BF16)]
        cache_v = [jnp.where(low_c, cv, cv_sw).astype(BF16), jnp.where(low_c, cv_sw, cv).astype(BF16)]
    for n in range(GQA_KV_HEADS):
        kv_sl = slice(n * LANES, (n + 1) * LANES)
        keys, values = [kk_ref[:, kv_sl]], [vv_ref[:, kv_sl]]
        if latent:
            keys.append(cache_k[n])
            values.append(cache_v[n])
        for j in range(2):
            sl = slice((2 * n + j) * LANES, (2 * n + j + 1) * LANES)
            q = qg_ref[:, sl]
            o_even = _attend(jnp.where(low_half, q, zero), keys, values)
            o_odd = _attend(jnp.where(low_half, zero, q), keys, values)
            ga_ref[:, sl] = jnp.where(low_half, o_even, o_odd).astype(BF16)


def _attention(pre_outs, caches, layer, lam_params, subln, *, latent, n_seq, seq, q_tile):
    qd, kd, vd, qg, kk, vv = pre_outs
    tiles = seq // q_tile
    q_spec = lambda w: pl.BlockSpec((q_tile, w), lambda b, i: (b * tiles + i, 0))
    kv_spec = lambda w: pl.BlockSpec((seq, w), lambda b, i: (b, 0))
    in_specs = [q_spec(512), kv_spec(512), kv_spec(512), q_spec(512), kv_spec(256), kv_spec(256)]
    args = [qd, kd, vd, qg, kk, vv]
    if latent:
        for c in caches:
            in_specs.append(pl.BlockSpec((1, 1) + c.shape[2:], lambda b, i: (b, layer, 0, 0)))
            args.append(c)
    in_specs += [_resident((4, HEAD_DIM)), _resident((1, LANES))]
    args += [lam_params, subln]
    lam_init = 0.8 - 0.6 * math.exp(-0.3 * layer)
    return pl.pallas_call(
        functools.partial(_attn_kernel, latent=latent, lam_init=lam_init),
        grid=(n_seq, tiles),
        in_specs=in_specs,
        out_specs=[q_spec(512), q_spec(512)],
        out_shape=[jax.ShapeDtypeStruct((n_seq * seq, 512), BF16)] * 2,
        compiler_params=_params(2),
        name="attn_latent" if latent else "attn_ctx",
    )(*args)


def _conv_kernel(prev_ref, cur_ref, next_ref, w_ref, b_ref, g_ref, beta_ref, o_ref, win_ref, *, chunk):
    c, n_chunks = pl.program_id(1), pl.num_programs(1)
    head = jnp.where(c > 0, prev_ref[chunk - CONV_HALO:, :], 0.0)
    tail = jnp.where(c < n_chunks - 1, next_ref[:CONV_HALO, :], 0.0)
    win_ref[0:CONV_HALO, :] = head
    win_ref[CONV_HALO:CONV_HALO + chunk, :] = cur_ref[...]
    win_ref[CONV_HALO + chunk:, :] = tail

    sub = 32
    first = CONV_HALO - CONV_KSIZE // 2
    bias, gain, beta = b_ref[...], g_ref[...], beta_ref[...]
    for r in range(0, chunk, sub):
        acc = jnp.zeros((sub, CONV_WIDTH), F32)
        for j in range(CONV_KSIZE):
            acc = acc + win_ref[r + first + j:r + first + j + sub, :] * w_ref[j:j + 1, :]
        acc = acc + bias
        mu = jnp.mean(acc, axis=-1, keepdims=True)
        xc = acc - mu
        y = xc * lax.rsqrt(jnp.mean(xc * xc, axis=-1, keepdims=True) + NORM_EPS) * gain + beta
        o_ref[r:r + sub, :] = (y * jax.nn.sigmoid(y)).astype(BF16)


def _conv(u, conv_w, conv_b, ln_g, ln_b, *, n_seq, seq, chunk):
    n = seq // chunk
    blk = lambda f: pl.BlockSpec((chunk, CONV_WIDTH), lambda b, c: (b * n + f(c), 0))
    return pl.pallas_call(
        functools.partial(_conv_kernel, chunk=chunk),
        grid=(n_seq, n),
        in_specs=[blk(lambda c: jnp.maximum(c - 1, 0)), blk(lambda c: c),
                  blk(lambda c: jnp.minimum(c + 1, n - 1)),
                  _resident((CONV_KSIZE, CONV_WIDTH)), _resident((1, CONV_WIDTH)),
                  _resident((1, CONV_WIDTH)), _resident((1, CONV_WIDTH))],
        out_specs=blk(lambda c: c),
        out_shape=jax.ShapeDtypeStruct((n_seq * seq, CONV_WIDTH), BF16),
        scratch_shapes=[pltpu.VMEM((chunk + 2 * CONV_HALO, CONV_WIDTH), F32)],
        compiler_params=_params(2),
        name="conv",
    )(u, u, u, conv_w, conv_b, ln_g, ln_b)


def _post_kernel(x_ref, da_ref, ga_ref, ca_ref, mod_ref, n1_ref, n2_ref, fn_ref,
                 wg_ref, wda_ref, wga_ref, wco_ref, wo_ref, w1_ref, w2_ref, o_ref, *, last):
    x = x_ref[...]
    mod = mod_ref[0, 0]
    m = lambda k: mod[:, k * D_MODEL:(k + 1) * D_MODEL]
    h = _modulated_norm(x, n1_ref[...], m(0), m(1))

    branches = ((da_ref, wda_ref), (ga_ref, wga_ref), (ca_ref, wco_ref))
    merged = None
    for j, (act_ref, w_ref) in enumerate(branches):
        gate = jax.nn.sigmoid(_dot(h, wg_ref[:, j * D_MODEL:(j + 1) * D_MODEL]))
        term = gate * _dot(act_ref[...], w_ref[...])
        merged = term if merged is None else merged + term
    x = x + m(2) * _dot(merged.astype(BF16), wo_ref[...])

    h2 = _modulated_norm(x, n2_ref[...], m(3), m(4))
    hid = 1024
    f = None
    for c in range(0, MLP_HIDDEN, hid):
        a = jnp.maximum(_dot(h2, w1_ref[:, c:c + hid]), 0.0)
        term = _dot((a * a).astype(BF16), w2_ref[c:c + hid, :])
        f = term if f is None else f + term
    x = x + m(5) * f
    o_ref[...] = _rms(x, fn_ref[...]) if last else x


def _post(x2d, da, ga, ca, mod, layer, norm1, norm2, final_norm, weights, *, latent, seq, tile, last):
    n_tok = x2d.shape[0]
    tiles_per_seq = seq // tile if latent else 1
    row = (lambda i: 1 + i // tiles_per_seq) if latent else (lambda i: 0)
    tok = lambda w: pl.BlockSpec((tile, w), lambda i: (i, 0))
    in_specs = [tok(D_MODEL), tok(512), tok(512), tok(512),
                pl.BlockSpec((1, 1, 1, N_MOD * D_MODEL), lambda i: (layer, row(i), 0, 0)),
                _resident((1, D_MODEL)), _resident((1, D_MODEL)), _resident((1, D_MODEL))]
    in_specs += [_resident(w.shape) for w in weights]
    return pl.pallas_call(
        functools.partial(_post_kernel, last=last),
        grid=(n_tok // tile,),
        in_specs=in_specs,
        out_specs=tok(D_MODEL),
        out_shape=jax.ShapeDtypeStruct((n_tok, D_MODEL), F32),
        compiler_params=_params(1),
        name="post_latent" if latent else "post_ctx",
    )(x2d, da, ga, ca, mod, norm1, norm2, final_norm, *weights)


def _rope_tables(n_tokens):
    n_rows = n_tokens // GRID_W
    row = jnp.repeat(jnp.arange(n_rows), GRID_W).astype(F32)
    col = jnp.tile(jnp.arange(GRID_W), n_rows).astype(F32)
    axis_dim = HEAD_DIM // 2
    freqs = ROPE_THETA ** (-jnp.arange(0, axis_dim, 2, dtype=F32) / axis_dim)
    ang_r = row[:, None] * freqs[None, :]
    ang_c = col[:, None] * freqs[None, :]
    ang = jnp.concatenate([ang_r, ang_r, ang_c, ang_c], axis=-1)
    sign = jnp.tile(jnp.repeat(jnp.array([-1.0, 1.0], F32), HEAD_DIM // 4), 2)
    return jnp.tile(jnp.cos(ang), (1, 2)), jnp.tile(jnp.sin(ang) * sign, (1, 2))


def kernel(x_prompt, x_sample, cache_diff_k, cache_diff_v, cache_gqa_k, cache_gqa_v, c, c_ctx, w_ada, b_ada, norm1, norm2, w_in, diff_lq1, diff_lk1, diff_lq2, diff_lk2, diff_subln, w_diff_o, gqa_q_norm, gqa_k_norm, w_gqa_o, conv_dw, conv_dw_b, conv_ln_g, conv_ln_b, w_conv_o, w_o, w_mlp1, w_mlp2, final_norm):
    n_ctx, s_ctx, _ = x_prompt.shape
    n_lat, s_lat, _ = x_sample.shape
    past = cache_diff_k.shape[2]
    assert n_lat + 1 <= MOD_ROWS

    cvecs = jnp.concatenate([c_ctx[None], c, jnp.zeros((MOD_ROWS - 1 - n_lat, D_MODEL), F32)], axis=0)
    mod = _modulation(cvecs, w_ada, b_ada).reshape(DEPTH, MOD_ROWS, 1, N_MOD * D_MODEL)

    caches = (cache_diff_k.reshape(n_lat, DEPTH, past, 512), cache_diff_v.reshape(n_lat, DEPTH, past, 512),
              cache_gqa_k.reshape(n_lat, DEPTH, past, GQA_KV_WIDTH),
              cache_gqa_v.reshape(n_lat, DEPTH, past, GQA_KV_WIDTH))
    rope_tabs = _rope_tables(s_lat)
    row_vec = lambda p: p.reshape(1, -1)
    fn = row_vec(final_norm)

    groups = (dict(latent=False, n_seq=n_ctx, seq=s_ctx, tile=512, q_tile=s_ctx, chunk=s_ctx),
              dict(latent=True, n_seq=n_lat, seq=s_lat, tile=512, q_tile=256, chunk=256))
    xs = [x_prompt.reshape(n_ctx * s_ctx, D_MODEL), x_sample.reshape(n_lat * s_lat, D_MODEL)]
    new_cache = []

    for l in range(DEPTH):
        w_mix = w_in[l, :, :MIX_WIDTH].astype(BF16)
        post_w = [w_in[l, :, MIX_WIDTH:].astype(BF16), w_diff_o[l].astype(BF16), w_gqa_o[l].astype(BF16),
                  w_conv_o[l].astype(BF16), w_o[l].astype(BF16), w_mlp1[l].astype(BF16),
                  w_mlp2[l].astype(BF16)]
        qn = row_vec(jnp.tile(gqa_q_norm[l], 2))
        kn = row_vec(jnp.tile(gqa_k_norm[l], 2))
        lam_params = jnp.stack([diff_lq1[l], diff_lk1[l], diff_lq2[l], diff_lk2[l]])
        n1, n2 = row_vec(norm1[l]), row_vec(norm2[l])
        for gi, g in enumerate(groups):
            latent, n_seq, seq = g["latent"], g["n_seq"], g["seq"]
            outs = _pre(xs[gi], mod, l, n1, w_mix, qn, kn, rope_tabs, latent=latent, seq=seq, tile=g["tile"])
            if not latent:
                new_cache.append(outs[7:])
            da, ga = _attention(outs[:6], caches, l, lam_params, row_vec(diff_subln[l]), latent=latent,
                                n_seq=n_seq, seq=seq, q_tile=g["q_tile"])
            ca = _conv(outs[6], conv_dw[l], row_vec(conv_dw_b[l]), row_vec(conv_ln_g[l]),
                       row_vec(conv_ln_b[l]), n_seq=n_seq, seq=seq, chunk=g["chunk"])
            xs[gi] = _post(xs[gi], da, ga, ca, mod, l, n1, n2, fn, post_w, latent=latent, seq=seq,
                           tile=g["tile"], last=(l == DEPTH - 1))

    stack = lambda k, shape: jnp.stack([new_cache[l][k].reshape((n_ctx, s_ctx) + shape) for l in range(DEPTH)], axis=1)
    return (xs[0].reshape(n_ctx, s_ctx, D_MODEL), xs[1].reshape(n_lat, s_lat, D_MODEL),
            stack(0, (DIFF_HEADS, 2, HEAD_DIM)), stack(1, (DIFF_HEADS, 2 * HEAD_DIM)),
            stack(2, (GQA_KV_HEADS, HEAD_DIM)), stack(3, (GQA_KV_HEADS, HEAD_DIM)))
```

```python
import functools
import math

import jax
import jax.numpy as jnp
from jax import lax
from jax.experimental import pallas as pl
from jax.experimental.pallas import tpu as pltpu

D_MODEL = 1024
DEPTH = 2
GRID_W = 64
ROPE_THETA = 10000.0
NORM_EPS = 1e-6

DIFF_HEADS = 4
HEAD_DIM = 64
DIFF_WIDTH = 512
GQA_KV_HEADS = 2
GQA_WIDTH = 512
GQA_KV_WIDTH = 128
CONV_WIDTH = 512
CONV_KSIZE = 31
CONV_HALO = 16
N_BRANCH = 3
MLP_HIDDEN = 4 * D_MODEL
N_MOD = 6

MIX_WIDTH = 3 * 512 + 512 + 2 * 128 + 2 * CONV_WIDTH
GATE_WIDTH = N_BRANCH * D_MODEL

LANES = 128
SUBLANES = 8
CONV_ROW_STRIDE = 4
MXU_WIDTH = 256
MOD_ROWS = 8
VMEM_LIMIT = 56 * 1024 * 1024

F32 = jnp.float32
BF16 = jnp.bfloat16


def _dot(a, b):
    return jnp.dot(a, b, preferred_element_type=F32)


def _dot_nt(a, b):
    return lax.dot_general(a, b, (((1,), (1,)), ((), ())), preferred_element_type=F32)


def _rms(x, gain):
    return x * lax.rsqrt(jnp.mean(x * x, axis=-1, keepdims=True) + NORM_EPS) * gain


def _modulated_norm(x, gain, shift, scale):
    return (_rms(x, gain) * (1.0 + scale) + shift).astype(BF16)


def _lane_iota(shape):
    return lax.broadcasted_iota(jnp.int32, shape, len(shape) - 1)


def _resident(shape):
    nd = len(shape)
    return pl.BlockSpec(shape, lambda *_: (0,) * nd, pipeline_mode=pl.Buffered(1))


def _params(n_axes):
    return pltpu.CompilerParams(dimension_semantics=("arbitrary",) * n_axes,
                                vmem_limit_bytes=VMEM_LIMIT)


def _mod_kernel(c_ref, w_ref, b_ref, o_ref):
    c = c_ref[...]
    s = (c * jax.nn.sigmoid(c)).astype(BF16)
    o_ref[0] = _dot(s, w_ref[0].astype(BF16)) + b_ref[0]


def _modulation(cvecs, w_ada, b_ada):
    width = N_MOD * D_MODEL
    tn = 1536
    return pl.pallas_call(
        _mod_kernel,
        grid=(DEPTH, width // tn),
        in_specs=[pl.BlockSpec((MOD_ROWS, D_MODEL), lambda l, j: (0, 0)),
                  pl.BlockSpec((1, D_MODEL, tn), lambda l, j: (l, 0, j)),
                  pl.BlockSpec((1, 1, tn), lambda l, j: (l, 0, j))],
        out_specs=pl.BlockSpec((1, MOD_ROWS, tn), lambda l, j: (l, 0, j)),
        out_shape=jax.ShapeDtypeStruct((DEPTH, MOD_ROWS, width), F32),
        compiler_params=_params(2),
        name="adaln_mod",
    )(cvecs, w_ada, b_ada.reshape(DEPTH, 1, width))


def _rope(x, cos, sin_signed, first_half):
    rot = jnp.where(first_half, pltpu.roll(x, LANES - 16, 1), pltpu.roll(x, 16, 1))
    return x * cos + rot * sin_signed


def _head_mean_sq(x):
    width = min(x.shape[1], MXU_WIDTH)
    r = lax.broadcasted_iota(jnp.int32, (width, width), 0) // HEAD_DIM
    c = lax.broadcasted_iota(jnp.int32, (width, width), 1) // HEAD_DIM
    blockdiag = jnp.where(r == c, 1.0 / HEAD_DIM, 0.0).astype(BF16)
    parts = []
    for lo in range(0, x.shape[1], width):
        sq = x[:, lo:lo + width] * x[:, lo:lo + width]
        hi = sq.astype(BF16)
        rest = (sq - hi.astype(F32)).astype(BF16)
        parts.append(_dot(hi, blockdiag) + _dot(rest, blockdiag))
    return parts[0] if len(parts) == 1 else jnp.concatenate(parts, axis=1)


def _pre_kernel(*refs, latent):
    if latent:
        (x_ref, mod_ref, n1_ref, w_ref, qn_ref, kn_ref, cos_ref, sin_ref,
         qd_ref, kd_ref, vd_ref, qg_ref, kk_ref, vv_ref, u_ref) = refs
    else:
        (x_ref, mod_ref, n1_ref, w_ref, qn_ref, kn_ref,
         qd_ref, kd_ref, vd_ref, qg_ref, kk_ref, vv_ref, u_ref,
         ndk_ref, ndv_ref, ngk_ref, ngv_ref) = refs

    mod = mod_ref[0, 0]
    h = _modulated_norm(x_ref[...], n1_ref[...], mod[:, 0:D_MODEL], mod[:, D_MODEL:2 * D_MODEL])
    rows = h.shape[0]

    lane = _lane_iota((rows, LANES))
    low_half = lane < HEAD_DIM
    if latent:
        cos, sin_signed = cos_ref[...], sin_ref[...]
        first_half = ((lane % HEAD_DIM) // 16) % 2 == 0
        rope = lambda t: _rope(t, cos, sin_signed, first_half)
    else:
        rope = lambda t: t

    qk_scale = HEAD_DIM ** -0.5

    def proj(lo, width):
        return _dot(h, w_ref[:, lo:lo + width])

    def slabs(t):
        return [t[:, s:s + LANES] for s in range(0, t.shape[1], LANES)]

    def store_slabs(ref, parts):
        for s, p in enumerate(parts):
            ref[:, s * LANES:(s + 1) * LANES] = p.astype(ref.dtype)

    store_slabs(qd_ref, [rope(t) * qk_scale for t in slabs(proj(0, DIFF_WIDTH))])
    dk = proj(DIFF_WIDTH, DIFF_WIDTH)
    dv = proj(2 * DIFF_WIDTH, DIFF_WIDTH)
    if not latent:
        ndk_ref[...] = dk
        ndv_ref[...] = dv
    store_slabs(kd_ref, [rope(t) for t in slabs(dk)])
    vd_ref[...] = dv.astype(BF16)

    gq = proj(3 * DIFF_WIDTH, GQA_WIDTH)
    gq = gq * lax.rsqrt(_head_mean_sq(gq) + NORM_EPS)
    qn = qn_ref[...]
    store_slabs(qg_ref, [rope(t * qn) * qk_scale for t in slabs(gq)])

    gkv = proj(3 * DIFF_WIDTH + GQA_WIDTH, 2 * GQA_KV_WIDTH)
    k, v = gkv[:, :GQA_KV_WIDTH], gkv[:, GQA_KV_WIDTH:]
    k = k * lax.rsqrt(_head_mean_sq(k) + NORM_EPS) * kn_ref[...]
    if not latent:
        ngk_ref[...] = k
        ngv_ref[...] = v
    k = rope(k)
    k_sw, v_sw = pltpu.roll(k, HEAD_DIM, 1), pltpu.roll(v, HEAD_DIM, 1)
    store_slabs(kk_ref, [jnp.where(low_half, k, k_sw), jnp.where(low_half, k_sw, k)])
    store_slabs(vv_ref, [jnp.where(low_half, v, v_sw), jnp.where(low_half, v_sw, v)])

    cv = proj(3 * DIFF_WIDTH + GQA_WIDTH + 2 * GQA_KV_WIDTH, 2 * CONV_WIDTH)
    u_ref[...] = cv[:, :CONV_WIDTH] * jax.nn.sigmoid(cv[:, CONV_WIDTH:])


def _pre(x2d, mod, layer, norm1, w_mix, qn, kn, rope_tabs, *, latent, seq, tile):
    n_tok = x2d.shape[0]
    tiles_per_seq = seq // tile if latent else 1
    row = (lambda i: 1 + i // tiles_per_seq) if latent else (lambda i: 0)
    tok = lambda w: pl.BlockSpec((tile, w), lambda i: (i, 0))
    in_specs = [tok(D_MODEL),
                pl.BlockSpec((1, 1, 1, N_MOD * D_MODEL), lambda i: (layer, row(i), 0, 0)),
                _resident((1, D_MODEL)), _resident((D_MODEL, MIX_WIDTH)),
                _resident((1, LANES)), _resident((1, LANES))]
    args = [x2d, mod, norm1, w_mix, qn, kn]
    if latent:
        in_specs += [pl.BlockSpec((tile, LANES), lambda i: (i % tiles_per_seq, 0))] * 2
        args += list(rope_tabs)
    widths = [(512, BF16), (512, BF16), (512, BF16), (512, BF16), (256, BF16), (256, BF16), (512, F32)]
    if not latent:
        widths += [(512, F32), (512, F32), (128, F32), (128, F32)]
    return pl.pallas_call(
        functools.partial(_pre_kernel, latent=latent),
        grid=(n_tok // tile,),
        in_specs=in_specs,
        out_specs=[tok(w) for w, _ in widths],
        out_shape=[jax.ShapeDtypeStruct((n_tok, w), dt) for w, dt in widths],
        compiler_params=_params(1),
        name="pre_latent" if latent else "pre_ctx",
    )(*args)


def _softmax_terms(q, keys):
    scores = [_dot_nt(q, k) for k in keys]
    m = functools.reduce(jnp.maximum, [jnp.max(s, axis=-1, keepdims=True) for s in scores])
    exps = [jnp.exp(s - m) for s in scores]
    denom = functools.reduce(jnp.add, [jnp.sum(e, axis=-1, keepdims=True) for e in exps])
    return exps, denom


def _attend(q, keys, values):
    exps, denom = _softmax_terms(q, keys)
    out = functools.reduce(jnp.add, [_dot(e.astype(BF16), v) for e, v in zip(exps, values)])
    return out / denom


def _attn_kernel(*refs, latent, lam_init):
    if latent:
        (qd_ref, kd_ref, vd_ref, qg_ref, kk_ref, vv_ref, cdk_ref, cdv_ref, cgk_ref, cgv_ref,
         lp_ref, sub_ref, da_ref, ga_ref) = refs
    else:
        qd_ref, kd_ref, vd_ref, qg_ref, kk_ref, vv_ref, lp_ref, sub_ref, da_ref, ga_ref = refs

    rows = qd_ref.shape[0]
    low_half = _lane_iota((rows, LANES)) < HEAD_DIM
    zero = jnp.zeros((), BF16)

    lp = lp_ref[...]
    lam = (jnp.exp(jnp.sum(lp[0:1] * lp[1:2], axis=-1, keepdims=True))
           - jnp.exp(jnp.sum(lp[2:3] * lp[3:4], axis=-1, keepdims=True)) + lam_init)
    sub_gain = sub_ref[...] * (1.0 - lam_init)

    for h in range(DIFF_HEADS):
        sl = slice(h * LANES, (h + 1) * LANES)
        q = qd_ref[:, sl]
        keys, values = [kd_ref[:, sl]], [vd_ref[:, sl]]
        if latent:
            keys.append(cdk_ref[0, 0, :, sl].astype(BF16))
            values.append(cdv_ref[0, 0, :, sl].astype(BF16))
        o1 = _attend(jnp.where(low_half, q, zero), keys, values)
        o2 = _attend(jnp.where(low_half, zero, q), keys, values)
        da_ref[:, sl] = _rms(o1 - lam * o2, sub_gain).astype(BF16)

    if latent:
        ck, cv = cgk_ref[0, 0], cgv_ref[0, 0]
        ck_sw, cv_sw = pltpu.roll(ck, HEAD_DIM, 1), pltpu.roll(cv, HEAD_DIM, 1)
        low_c = _lane_iota(ck.shape) < HEAD_DIM
        cache_k = [jnp.where(low_c, ck, ck_sw).astype(BF16), jnp.where(low_c, ck_sw, ck).astype(BF16)]
        cache_v = [jnp.where(low_c, cv, cv_sw).astype(BF16), jnp.where(low_c, cv_sw, cv).astype(BF16)]
    for n in range(GQA_KV_HEADS):
        kv_sl = slice(n * LANES, (n + 1) * LANES)
        keys, values = [kk_ref[:, kv_sl]], [vv_ref[:, kv_sl]]
        if latent:
            keys.append(cache_k[n])
            values.append(cache_v[n])
        for j in range(2):
            sl = slice((2 * n + j) * LANES, (2 * n + j + 1) * LANES)
            q = qg_ref[:, sl]
            o_even = _attend(jnp.where(low_half, q, zero), keys, values)
            o_odd = _attend(jnp.where(low_half, zero, q), keys, values)
            ga_ref[:, sl] = jnp.where(low_half, o_even, o_odd).astype(BF16)


def _attention(pre_outs, caches, layer, lam_params, subln, *, latent, n_seq, seq, q_tile):
    qd, kd, vd, qg, kk, vv = pre_outs
    tiles = seq // q_tile
    q_spec = lambda w: pl.BlockSpec((q_tile, w), lambda b, i: (b * tiles + i, 0))
    kv_spec = lambda w: pl.BlockSpec((seq, w), lambda b, i: (b, 0))
    in_specs = [q_spec(512), kv_spec(512), kv_spec(512), q_spec(512), kv_spec(256), kv_spec(256)]
    args = [qd, kd, vd, qg, kk, vv]
    if latent:
        for c in caches:
            in_specs.append(pl.BlockSpec((1, 1) + c.shape[2:], lambda b, i: (b, layer, 0, 0)))
            args.append(c)
    in_specs += [_resident((4, HEAD_DIM)), _resident((1, LANES))]
    args += [lam_params, subln]
    lam_init = 0.8 - 0.6 * math.exp(-0.3 * layer)
    return pl.pallas_call(
        functools.partial(_attn_kernel, latent=latent, lam_init=lam_init),
        grid=(n_seq, tiles),
        in_specs=in_specs,
        out_specs=[q_spec(512), q_spec(512)],
        out_shape=[jax.ShapeDtypeStruct((n_seq * seq, 512), BF16)] * 2,
        compiler_params=_params(2),
        name="attn_latent" if latent else "attn_ctx",
    )(*args)


def _conv_kernel(prev_ref, cur_ref, next_ref, w_ref, b_ref, g_ref, beta_ref, o_ref, win_ref, y_ref, *, chunk):
    c, n_chunks = pl.program_id(1), pl.num_programs(1)
    head = jnp.where(c > 0, prev_ref[chunk - CONV_HALO:, :], 0.0)
    tail = jnp.where(c < n_chunks - 1, next_ref[:CONV_HALO, :], 0.0)
    lane_slabs = [slice(s * LANES, (s + 1) * LANES) for s in range(CONV_WIDTH // LANES)]
    for s, ls in enumerate(lane_slabs):
        win_ref[s, 0:CONV_HALO, :] = head[:, ls]
        win_ref[s, CONV_HALO:CONV_HALO + chunk, :] = cur_ref[:, ls]
        win_ref[s, CONV_HALO + chunk:, :] = tail[:, ls]

    first = CONV_HALO - CONV_KSIZE // 2
    bias, gain, beta = b_ref[...], g_ref[...], beta_ref[...]
    for r in range(0, chunk, SUBLANES * CONV_ROW_STRIDE):
        for t in range(CONV_ROW_STRIDE):
            accs = []
            for s, ls in enumerate(lane_slabs):
                acc = jnp.zeros((SUBLANES, LANES), F32)
                for j in range(CONV_KSIZE):
                    taps = win_ref[s, pl.ds(r + t + first + j, SUBLANES, stride=CONV_ROW_STRIDE), :]
                    acc = acc + taps * w_ref[j:j + 1, ls]
                accs.append(acc)
            acc = jnp.concatenate(accs, axis=1) + bias
            mu = jnp.mean(acc, axis=-1, keepdims=True)
            xc = acc - mu
            y = xc * lax.rsqrt(jnp.mean(xc * xc, axis=-1, keepdims=True) + NORM_EPS) * gain + beta
            y = y * jax.nn.sigmoid(y)
            for s, ls in enumerate(lane_slabs):
                y_ref[s, pl.ds(r + t, SUBLANES, stride=CONV_ROW_STRIDE), :] = y[:, ls]
    for s, ls in enumerate(lane_slabs):
        o_ref[:, ls] = y_ref[s].astype(BF16)


def _conv(u, conv_w, conv_b, ln_g, ln_b, *, n_seq, seq, chunk):
    n = seq // chunk
    blk = lambda f: pl.BlockSpec((chunk, CONV_WIDTH), lambda b, c: (b * n + f(c), 0))
    return pl.pallas_call(
        functools.partial(_conv_kernel, chunk=chunk),
        grid=(n_seq, n),
        in_specs=[blk(lambda c: jnp.maximum(c - 1, 0)), blk(lambda c: c),
                  blk(lambda c: jnp.minimum(c + 1, n - 1)),
                  _resident((CONV_KSIZE, CONV_WIDTH)), _resident((1, CONV_WIDTH)),
                  _resident((1, CONV_WIDTH)), _resident((1, CONV_WIDTH))],
        out_specs=blk(lambda c: c),
        out_shape=jax.ShapeDtypeStruct((n_seq * seq, CONV_WIDTH), BF16),
        scratch_shapes=[pltpu.VMEM((CONV_WIDTH // LANES, chunk + 2 * CONV_HALO, LANES), F32),
                        pltpu.VMEM((CONV_WIDTH // LANES, chunk, LANES), F32)],
        compiler_params=_params(2),
        name="conv",
    )(u, u, u, conv_w, conv_b, ln_g, ln_b)


def _post_kernel(x_ref, da_ref, ga_ref, ca_ref, mod_ref, n1_ref, n2_ref, fn_ref,
                 wg_ref, wda_ref, wga_ref, wco_ref, wo_ref, w1_ref, w2_ref, o_ref, *, last):
    x = x_ref[...]
    mod = mod_ref[0, 0]
    m = lambda k: mod[:, k * D_MODEL:(k + 1) * D_MODEL]
    h = _modulated_norm(x, n1_ref[...], m(0), m(1))

    branches = ((da_ref, wda_ref), (ga_ref, wga_ref), (ca_ref, wco_ref))
    merged = None
    for j, (act_ref, w_ref) in enumerate(branches):
        gate = jax.nn.sigmoid(_dot(h, wg_ref[:, j * D_MODEL:(j + 1) * D_MODEL]))
        term = gate * _dot(act_ref[...], w_ref[...])
        merged = term if merged is None else merged + term
    x = x + m(2) * _dot(merged.astype(BF16), wo_ref[...])

    h2 = _modulated_norm(x, n2_ref[...], m(3), m(4))
    hid = 1024
    f = None
    for c in range(0, MLP_HIDDEN, hid):
        a = jnp.maximum(_dot(h2, w1_ref[:, c:c + hid]), 0.0)
        term = _dot((a * a).astype(BF16), w2_ref[c:c + hid, :])
        f = term if f is None else f + term
    x = x + m(5) * f
    o_ref[...] = _rms(x, fn_ref[...]) if last else x


def _post(x2d, da, ga, ca, mod, layer, norm1, norm2, final_norm, weights, *, latent, seq, tile, last):
    n_tok = x2d.shape[0]
    tiles_per_seq = seq // tile if latent else 1
    row = (lambda i: 1 + i // tiles_per_seq) if latent else (lambda i: 0)
    tok = lambda w: pl.BlockSpec((tile, w), lambda i: (i, 0))
    in_specs = [tok(D_MODEL), tok(512), tok(512), tok(512),
                pl.BlockSpec((1, 1, 1, N_MOD * D_MODEL), lambda i: (layer, row(i), 0, 0)),
                _resident((1, D_MODEL)), _resident((1, D_MODEL)), _resident((1, D_MODEL))]
    in_specs += [_resident(w.shape) for w in weights]
    return pl.pallas_call(
        functools.partial(_post_kernel, last=last),
        grid=(n_tok // tile,),
        in_specs=in_specs,
        out_specs=tok(D_MODEL),
        out_shape=jax.ShapeDtypeStruct((n_tok, D_MODEL), F32),
        compiler_params=_params(1),
        name="post_latent" if latent else "post_ctx",
    )(x2d, da, ga, ca, mod, norm1, norm2, final_norm, *weights)


def _rope_tables(n_tokens):
    n_rows = n_tokens // GRID_W
    row = jnp.repeat(jnp.arange(n_rows), GRID_W).astype(F32)
    col = jnp.tile(jnp.arange(GRID_W), n_rows).astype(F32)
    axis_dim = HEAD_DIM // 2
    freqs = ROPE_THETA ** (-jnp.arange(0, axis_dim, 2, dtype=F32) / axis_dim)
    ang_r = row[:, None] * freqs[None, :]
    ang_c = col[:, None] * freqs[None, :]
    ang = jnp.concatenate([ang_r, ang_r, ang_c, ang_c], axis=-1)
    sign = jnp.tile(jnp.repeat(jnp.array([-1.0, 1.0], F32), HEAD_DIM // 4), 2)
    return jnp.tile(jnp.cos(ang), (1, 2)), jnp.tile(jnp.sin(ang) * sign, (1, 2))


def kernel(x_prompt, x_sample, cache_diff_k, cache_diff_v, cache_gqa_k, cache_gqa_v, c, c_ctx, w_ada, b_ada, norm1, norm2, w_in, diff_lq1, diff_lk1, diff_lq2, diff_lk2, diff_subln, w_diff_o, gqa_q_norm, gqa_k_norm, w_gqa_o, conv_dw, conv_dw_b, conv_ln_g, conv_ln_b, w_conv_o, w_o, w_mlp1, w_mlp2, final_norm):
    n_ctx, s_ctx, _ = x_prompt.shape
    n_lat, s_lat, _ = x_sample.shape
    past = cache_diff_k.shape[2]
    assert n_lat + 1 <= MOD_ROWS

    cvecs = jnp.concatenate([c_ctx[None], c, jnp.zeros((MOD_ROWS - 1 - n_lat, D_MODEL), F32)], axis=0)
    mod = _modulation(cvecs, w_ada, b_ada).reshape(DEPTH, MOD_ROWS, 1, N_MOD * D_MODEL)

    caches = (cache_diff_k.reshape(n_lat, DEPTH, past, 512), cache_diff_v.reshape(n_lat, DEPTH, past, 512),
              cache_gqa_k.reshape(n_lat, DEPTH, past, GQA_KV_WIDTH),
              cache_gqa_v.reshape(n_lat, DEPTH, past, GQA_KV_WIDTH))
    rope_tabs = _rope_tables(s_lat)
    row_vec = lambda p: p.reshape(1, -1)
    fn = row_vec(final_norm)

    groups = (dict(latent=False, n_seq=n_ctx, seq=s_ctx, tile=512, q_tile=s_ctx, chunk=s_ctx),
              dict(latent=True, n_seq=n_lat, seq=s_lat, tile=512, q_tile=256, chunk=256))
    xs = [x_prompt.reshape(n_ctx * s_ctx, D_MODEL), x_sample.reshape(n_lat * s_lat, D_MODEL)]
    new_cache = []

    for l in range(DEPTH):
        w_mix = w_in[l, :, :MIX_WIDTH].astype(BF16)
        post_w = [w_in[l, :, MIX_WIDTH:].astype(BF16), w_diff_o[l].astype(BF16), w_gqa_o[l].astype(BF16),
                  w_conv_o[l].astype(BF16), w_o[l].astype(BF16), w_mlp1[l].astype(BF16),
                  w_mlp2[l].astype(BF16)]
        qn = row_vec(jnp.tile(gqa_q_norm[l], 2))
        kn = row_vec(jnp.tile(gqa_k_norm[l], 2))
        lam_params = jnp.stack([diff_lq1[l], diff_lk1[l], diff_lq2[l], diff_lk2[l]])
        n1, n2 = row_vec(norm1[l]), row_vec(norm2[l])
        for gi, g in enumerate(groups):
            latent, n_seq, seq = g["latent"], g["n_seq"], g["seq"]
            outs = _pre(xs[gi], mod, l, n1, w_mix, qn, kn, rope_tabs, latent=latent, seq=seq, tile=g["tile"])
            if not latent:
                new_cache.append(outs[7:])
            da, ga = _attention(outs[:6], caches, l, lam_params, row_vec(diff_subln[l]), latent=latent,
                                n_seq=n_seq, seq=seq, q_tile=g["q_tile"])
            ca = _conv(outs[6], conv_dw[l], row_vec(conv_dw_b[l]), row_vec(conv_ln_g[l]),
                       row_vec(conv_ln_b[l]), n_seq=n_seq, seq=seq, chunk=g["chunk"])
            xs[gi] = _post(xs[gi], da, ga, ca, mod, l, n1, n2, fn, post_w, latent=latent, seq=seq,
                           tile=g["tile"], last=(l == DEPTH - 1))

    stack = lambda k, shape: jnp.stack([new_cache[l][k].reshape((n_ctx, s_ctx) + shape) for l in range(DEPTH)], axis=1)
    return (xs[0].reshape(n_ctx, s_ctx, D_MODEL), xs[1].reshape(n_lat, s_lat, D_MODEL),
            stack(0, (DIFF_HEADS, 2, HEAD_DIM)), stack(1, (DIFF_HEADS, 2 * HEAD_DIM)),
            stack(2, (GQA_KV_HEADS, HEAD_DIM)), stack(3, (GQA_KV_HEADS, HEAD_DIM)))
```

```python
import functools
import math

import jax
import jax.numpy as jnp
from jax import lax
from jax.experimental import pallas as pl
from jax.experimental.pallas import tpu as pltpu

D_MODEL = 1024
DEPTH = 2
GRID_W = 64
ROPE_THETA = 10000.0
NORM_EPS = 1e-6

DIFF_HEADS = 4
HEAD_DIM = 64
DIFF_WIDTH = 512
GQA_KV_HEADS = 2
GQA_WIDTH = 512
GQA_KV_WIDTH = 128
CONV_WIDTH = 512
CONV_KSIZE = 31
CONV_HALO = 16
N_BRANCH = 3
MLP_HIDDEN = 4 * D_MODEL
N_MOD = 6

MIX_WIDTH = 3 * 512 + 512 + 2 * 128 + 2 * CONV_WIDTH
GATE_WIDTH = N_BRANCH * D_MODEL

LANES = 128
SUBLANES = 8
CONV_ROW_STRIDE = 4
MXU_WIDTH = 256
MOD_ROWS = 8
VMEM_LIMIT = 56 * 1024 * 1024

F32 = jnp.float32
BF16 = jnp.bfloat16


def _dot(a, b):
    return jnp.dot(a, b, preferred_element_type=F32)


def _dot_nt(a, b):
    return lax.dot_general(a, b, (((1,), (1,)), ((), ())), preferred_element_type=F32)


def _rms(x, gain):
    return x * lax.rsqrt(jnp.mean(x * x, axis=-1, keepdims=True) + NORM_EPS) * gain


def _modulated_norm(x, gain, shift, scale):
    return (_rms(x, gain) * (1.0 + scale) + shift).astype(BF16)


def _lane_iota(shape):
    return lax.broadcasted_iota(jnp.int32, shape, len(shape) - 1)


def _resident(shape):
    nd = len(shape)
    return pl.BlockSpec(shape, lambda *_: (0,) * nd, pipeline_mode=pl.Buffered(1))


def _params(n_axes):
    return pltpu.CompilerParams(dimension_semantics=("arbitrary",) * n_axes,
                                vmem_limit_bytes=VMEM_LIMIT)


def _mod_kernel(c_ref, w_ref, b_ref, o_ref):
    c = c_ref[...]
    s = (c * jax.nn.sigmoid(c)).astype(BF16)
    o_ref[0] = _dot(s, w_ref[0].astype(BF16)) + b_ref[0]


def _modulation(cvecs, w_ada, b_ada):
    width = N_MOD * D_MODEL
    tn = 1536
    return pl.pallas_call(
        _mod_kernel,
        grid=(DEPTH, width // tn),
        in_specs=[pl.BlockSpec((MOD_ROWS, D_MODEL), lambda l, j: (0, 0)),
                  pl.BlockSpec((1, D_MODEL, tn), lambda l, j: (l, 0, j)),
                  pl.BlockSpec((1, 1, tn), lambda l, j: (l, 0, j))],
        out_specs=pl.BlockSpec((1, MOD_ROWS, tn), lambda l, j: (l, 0, j)),
        out_shape=jax.ShapeDtypeStruct((DEPTH, MOD_ROWS, width), F32),
        compiler_params=_params(2),
        name="adaln_mod",
    )(cvecs, w_ada, b_ada.reshape(DEPTH, 1, width))


def _rope(x, cos, sin_signed, first_half):
    rot = jnp.where(first_half, pltpu.roll(x, LANES - 16, 1), pltpu.roll(x, 16, 1))
    return x * cos + rot * sin_signed


def _head_mean_sq(x):
    width = min(x.shape[1], MXU_WIDTH)
    r = lax.broadcasted_iota(jnp.int32, (width, width), 0) // HEAD_DIM
    c = lax.broadcasted_iota(jnp.int32, (width, width), 1) // HEAD_DIM
    blockdiag = jnp.where(r == c, 1.0 / HEAD_DIM, 0.0).astype(BF16)
    parts = []
    for lo in range(0, x.shape[1], width):
        sq = x[:, lo:lo + width] * x[:, lo:lo + width]
        hi = sq.astype(BF16)
        rest = (sq - hi.astype(F32)).astype(BF16)
        parts.append(_dot(hi, blockdiag) + _dot(rest, blockdiag))
    return parts[0] if len(parts) == 1 else jnp.concatenate(parts, axis=1)


def _pre_kernel(*refs, latent):
    if latent:
        (x_ref, mod_ref, n1_ref, w_ref, qn_ref, kn_ref, cos_ref, sin_ref,
         qd_ref, kd_ref, vd_ref, qg_ref, kk_ref, vv_ref, u_ref) = refs
    else:
        (x_ref, mod_ref, n1_ref, w_ref, qn_ref, kn_ref,
         qd_ref, kd_ref, vd_ref, qg_ref, kk_ref, vv_ref, u_ref,
         ndk_ref, ndv_ref, ngk_ref, ngv_ref) = refs

    mod = mod_ref[0, 0]
    h = _modulated_norm(x_ref[...], n1_ref[...], mod[:, 0:D_MODEL], mod[:, D_MODEL:2 * D_MODEL])
    rows = h.shape[0]

    lane = _lane_iota((rows, LANES))
    low_half = lane < HEAD_DIM
    if latent:
        cos, sin_signed = cos_ref[...], sin_ref[...]
        first_half = ((lane % HEAD_DIM) // 16) % 2 == 0
        rope = lambda t: _rope(t, cos, sin_signed, first_half)
    else:
        rope = lambda t: t

    qk_scale = HEAD_DIM ** -0.5 * math.log2(math.e)

    def proj(lo, width):
        return _dot(h, w_ref[:, lo:lo + width])

    def slabs(t):
        return [t[:, s:s + LANES] for s in range(0, t.shape[1], LANES)]

    def store_slabs(ref, parts):
        for s, p in enumerate(parts):
            ref[:, s * LANES:(s + 1) * LANES] = p.astype(ref.dtype)

    store_slabs(qd_ref, [rope(t) * qk_scale for t in slabs(proj(0, DIFF_WIDTH))])
    dk = proj(DIFF_WIDTH, DIFF_WIDTH)
    dv = proj(2 * DIFF_WIDTH, DIFF_WIDTH)
    if not latent:
        ndk_ref[...] = dk
        ndv_ref[...] = dv
    store_slabs(kd_ref, [rope(t) for t in slabs(dk)])
    vd_ref[...] = dv.astype(BF16)

    gq = proj(3 * DIFF_WIDTH, GQA_WIDTH)
    gq = gq * lax.rsqrt(_head_mean_sq(gq) + NORM_EPS)
    qn = qn_ref[...]
    store_slabs(qg_ref, [rope(t * qn) * qk_scale for t in slabs(gq)])

    gkv = proj(3 * DIFF_WIDTH + GQA_WIDTH, 2 * GQA_KV_WIDTH)
    k, v = gkv[:, :GQA_KV_WIDTH], gkv[:, GQA_KV_WIDTH:]
    k = k * lax.rsqrt(_head_mean_sq(k) + NORM_EPS) * kn_ref[...]
    if not latent:
        ngk_ref[...] = k
        ngv_ref[...] = v
    k = rope(k)
    k_sw, v_sw = pltpu.roll(k, HEAD_DIM, 1), pltpu.roll(v, HEAD_DIM, 1)
    store_slabs(kk_ref, [jnp.where(low_half, k, k_sw), jnp.where(low_half, k_sw, k)])
    store_slabs(vv_ref, [jnp.where(low_half, v, v_sw), jnp.where(low_half, v_sw, v)])

    cv = proj(3 * DIFF_WIDTH + GQA_WIDTH + 2 * GQA_KV_WIDTH, 2 * CONV_WIDTH)
    u_ref[...] = cv[:, :CONV_WIDTH] * jax.nn.sigmoid(cv[:, CONV_WIDTH:])


def _pre(x2d, mod, layer, norm1, w_mix, qn, kn, rope_tabs, *, latent, seq, tile):
    n_tok = x2d.shape[0]
    tiles_per_seq = seq // tile if latent else 1
    row = (lambda i: 1 + i // tiles_per_seq) if latent else (lambda i: 0)
    tok = lambda w: pl.BlockSpec((tile, w), lambda i: (i, 0))
    in_specs = [tok(D_MODEL),
                pl.BlockSpec((1, 1, 1, N_MOD * D_MODEL), lambda i: (layer, row(i), 0, 0)),
                _resident((1, D_MODEL)), _resident((D_MODEL, MIX_WIDTH)),
                _resident((1, LANES)), _resident((1, LANES))]
    args = [x2d, mod, norm1, w_mix, qn, kn]
    if latent:
        in_specs += [pl.BlockSpec((tile, LANES), lambda i: (i % tiles_per_seq, 0))] * 2
        args += list(rope_tabs)
    widths = [(512, BF16), (512, BF16), (512, BF16), (512, BF16), (256, BF16), (256, BF16), (512, F32)]
    if not latent:
        widths += [(512, F32), (512, F32), (128, F32), (128, F32)]
    return pl.pallas_call(
        functools.partial(_pre_kernel, latent=latent),
        grid=(n_tok // tile,),
        in_specs=in_specs,
        out_specs=[tok(w) for w, _ in widths],
        out_shape=[jax.ShapeDtypeStruct((n_tok, w), dt) for w, dt in widths],
        compiler_params=_params(1),
        name="pre_latent" if latent else "pre_ctx",
    )(*args)


def _attend(q, keys, values):
    scores = [_dot_nt(q, k) for k in keys]
    m = functools.reduce(jnp.maximum, [jnp.max(s, axis=-1, keepdims=True) for s in scores])
    return functools.reduce(jnp.add, [_dot(jnp.exp2(s - m).astype(BF16), v) for s, v in zip(scores, values)])


def _attn_kernel(*refs, latent, lam_init):
    if latent:
        (qd_ref, kd_ref, vd_ref, qg_ref, kk_ref, vv_ref, cdk_ref, cdv_ref, cgk_ref, cgv_ref,
         lp_ref, sub_ref, da_ref, ga_ref) = refs
    else:
        qd_ref, kd_ref, vd_ref, qg_ref, kk_ref, vv_ref, lp_ref, sub_ref, da_ref, ga_ref = refs

    rows = qd_ref.shape[0]
    low_half = _lane_iota((rows, LANES)) < HEAD_DIM
    zero = jnp.zeros((), BF16)

    lp = lp_ref[...]
    lam = (jnp.exp(jnp.sum(lp[0:1] * lp[1:2], axis=-1, keepdims=True))
           - jnp.exp(jnp.sum(lp[2:3] * lp[3:4], axis=-1, keepdims=True)) + lam_init)
    sub_gain = sub_ref[...] * (1.0 - lam_init)

    def with_ones(v):
        return jnp.concatenate([v, jnp.ones_like(v)], axis=1)

    for h in range(DIFF_HEADS):
        sl = slice(h * LANES, (h + 1) * LANES)
        q = qd_ref[:, sl]
        keys, values = [kd_ref[:, sl]], [with_ones(vd_ref[:, sl])]
        if latent:
            keys.append(cdk_ref[0, 0, :, sl].astype(BF16))
            values.append(with_ones(cdv_ref[0, 0, :, sl].astype(BF16)))
        r1 = _attend(jnp.where(low_half, q, zero), keys, values)
        r2 = _attend(jnp.where(low_half, zero, q), keys, values)
        o = r1[:, :LANES] / r1[:, LANES:] - lam * (r2[:, :LANES] / r2[:, LANES:])
        da_ref[:, sl] = _rms(o, sub_gain).astype(BF16)

    def value_slabs(v_dup):
        low = _lane_iota(v_dup.shape) < HEAD_DIM
        one = jnp.ones((), v_dup.dtype)
        return jnp.where(low, v_dup, one), jnp.where(low, one, v_dup)

    if latent:
        ck, cv = cgk_ref[0, 0], cgv_ref[0, 0]
        ck_sw, cv_sw = pltpu.roll(ck, HEAD_DIM, 1), pltpu.roll(cv, HEAD_DIM, 1)
        low_c = _lane_iota(ck.shape) < HEAD_DIM
        cache_k = [jnp.where(low_c, ck, ck_sw).astype(BF16), jnp.where(low_c, ck_sw, ck).astype(BF16)]
        cache_v = [value_slabs(jnp.where(low_c, cv, cv_sw).astype(BF16)),
                   value_slabs(jnp.where(low_c, cv_sw, cv).astype(BF16))]
    for n in range(GQA_KV_HEADS):
        kv_sl = slice(n * LANES, (n + 1) * LANES)
        keys = [kk_ref[:, kv_sl]]
        values = [value_slabs(vv_ref[:, kv_sl])]
        if latent:
            keys.append(cache_k[n])
            values.append(cache_v[n])
        for j in range(2):
            sl = slice((2 * n + j) * LANES, (2 * n + j + 1) * LANES)
            q = qg_ref[:, sl]
            r_even = _attend(jnp.where(low_half, q, zero), keys, [v[0] for v in values])
            r_odd = _attend(jnp.where(low_half, zero, q), keys, [v[1] for v in values])
            r = jnp.where(low_half, r_even, r_odd)
            denom = jnp.where(low_half, pltpu.roll(r_even, HEAD_DIM, 1), pltpu.roll(r_odd, HEAD_DIM, 1))
            ga_ref[:, sl] = (r / denom).astype(BF16)


def _attention(pre_outs, caches, layer, lam_params, subln, *, latent, n_seq, seq, q_tile):
    qd, kd, vd, qg, kk, vv = pre_outs
    tiles = seq // q_tile
    q_spec = lambda w: pl.BlockSpec((q_tile, w), lambda b, i: (b * tiles + i, 0))
    kv_spec = lambda w: pl.BlockSpec((seq, w), lambda b, i: (b, 0))
    in_specs = [q_spec(512), kv_spec(512), kv_spec(512), q_spec(512), kv_spec(256), kv_spec(256)]
    args = [qd, kd, vd, qg, kk, vv]
    if latent:
        for c in caches:
            in_specs.append(pl.BlockSpec((1, 1) + c.shape[2:], lambda b, i: (b, layer, 0, 0)))
            args.append(c)
    in_specs += [_resident((4, HEAD_DIM)), _resident((1, LANES))]
    args += [lam_params, subln]
    lam_init = 0.8 - 0.6 * math.exp(-0.3 * layer)
    return pl.pallas_call(
        functools.partial(_attn_kernel, latent=latent, lam_init=lam_init),
        grid=(n_seq, tiles),
        in_specs=in_specs,
        out_specs=[q_spec(512), q_spec(512)],
        out_shape=[jax.ShapeDtypeStruct((n_seq * seq, 512), BF16)] * 2,
        compiler_params=_params(2),
        name="attn_latent" if latent else "attn_ctx",
    )(*args)


def _conv_branch(u_ref, head, tail, w_ref, bias, gain, beta, win_ref, y_ref, seg):
    n_seg = u_ref.shape[0] // seg
    pitch = seg + 2 * CONV_HALO
    lane_slabs = [slice(s * LANES, (s + 1) * LANES) for s in range(CONV_WIDTH // LANES)]
    zeros = jnp.zeros((CONV_HALO, LANES), F32)
    for k in range(n_seg):
        for s, ls in enumerate(lane_slabs):
            win_ref[s, k * pitch:k * pitch + CONV_HALO, :] = head[:, ls] if k == 0 else zeros
            win_ref[s, k * pitch + CONV_HALO:(k + 1) * pitch - CONV_HALO, :] = u_ref[k * seg:(k + 1) * seg, ls]
            win_ref[s, (k + 1) * pitch - CONV_HALO:(k + 1) * pitch, :] = tail[:, ls] if k == n_seg - 1 else zeros

    first = CONV_HALO - CONV_KSIZE // 2
    for k in range(n_seg):
        for r in range(0, seg, SUBLANES * CONV_ROW_STRIDE):
            for t in range(CONV_ROW_STRIDE):
                accs = []
                for s, ls in enumerate(lane_slabs):
                    acc = jnp.zeros((SUBLANES, LANES), F32)
                    for j in range(CONV_KSIZE):
                        start = k * pitch + r + t + first + j
                        taps = win_ref[s, pl.ds(start, SUBLANES, stride=CONV_ROW_STRIDE), :]
                        acc = acc + taps * w_ref[j:j + 1, ls]
                    accs.append(acc)
                acc = jnp.concatenate(accs, axis=1) + bias
                mu = jnp.mean(acc, axis=-1, keepdims=True)
                xc = acc - mu
                y = xc * lax.rsqrt(jnp.mean(xc * xc, axis=-1, keepdims=True) + NORM_EPS) * gain + beta
                y = y * jax.nn.sigmoid(y)
                for s, ls in enumerate(lane_slabs):
                    y_ref[s, pl.ds(k * seg + r + t, SUBLANES, stride=CONV_ROW_STRIDE), :] = y[:, ls]
    return jnp.concatenate([y_ref[s] for s in range(len(lane_slabs))], axis=1)


def _post_kernel(*refs, last, seg, tiles_per_seq):
    halo = tiles_per_seq > 1
    if halo:
        x_ref, da_ref, ga_ref, u_ref, uprev_ref, unext_ref = refs[:6]
        refs = refs[6:]
    else:
        x_ref, da_ref, ga_ref, u_ref = refs[:4]
        refs = refs[4:]
    (mod_ref, n1_ref, n2_ref, fn_ref, cw_ref, cb_ref, cg_ref, cbeta_ref,
     wg_ref, wda_ref, wga_ref, wco_ref, wo_ref, w1_ref, w2_ref, o_ref, win_ref, y_ref) = refs

    if halo:
        pos = pl.program_id(0) % tiles_per_seq
        head = jnp.where(pos > 0, uprev_ref[...], 0.0)
        tail = jnp.where(pos < tiles_per_seq - 1, unext_ref[...], 0.0)
    else:
        head = tail = jnp.zeros((CONV_HALO, CONV_WIDTH), F32)
    ca = _conv_branch(u_ref, head, tail, cw_ref, cb_ref[...], cg_ref[...], cbeta_ref[...], win_ref, y_ref, seg)

    x = x_ref[...]
    mod = mod_ref[0, 0]
    m = lambda k: mod[:, k * D_MODEL:(k + 1) * D_MODEL]
    h = _modulated_norm(x, n1_ref[...], m(0), m(1))

    branches = ((da_ref[...], wda_ref), (ga_ref[...], wga_ref), (ca.astype(BF16), wco_ref))
    merged = None
    for j, (act, w_ref) in enumerate(branches):
        gate = jax.nn.sigmoid(_dot(h, wg_ref[:, j * D_MODEL:(j + 1) * D_MODEL]))
        term = gate * _dot(act, w_ref[...])
        merged = term if merged is None else merged + term
    x = x + m(2) * _dot(merged.astype(BF16), wo_ref[...])

    h2 = _modulated_norm(x, n2_ref[...], m(3), m(4))
    hid = 1024
    f = None
    for c in range(0, MLP_HIDDEN, hid):
        a = jnp.maximum(_dot(h2, w1_ref[:, c:c + hid]), 0.0)
        term = _dot((a * a).astype(BF16), w2_ref[c:c + hid, :])
        f = term if f is None else f + term
    x = x + m(5) * f
    o_ref[...] = _rms(x, fn_ref[...]) if last else x


def _post(x2d, da, ga, u, mod, layer, norm1, norm2, final_norm, conv_params, weights, *, latent, seq, tile, last):
    n_tok = x2d.shape[0]
    assert seq % tile == 0 or tile % seq == 0
    tiles_per_seq = max(seq // tile, 1)
    seg = min(seq, tile)
    row = (lambda i: 1 + i // tiles_per_seq) if latent else (lambda i: 0)
    tok = lambda w: pl.BlockSpec((tile, w), lambda i: (i, 0))
    in_specs = [tok(D_MODEL), tok(512), tok(512), tok(CONV_WIDTH)]
    args = [x2d, da, ga, u]
    if tiles_per_seq > 1:
        per_tile, n_halo = tile // CONV_HALO, n_tok // CONV_HALO
        in_specs += [pl.BlockSpec((CONV_HALO, CONV_WIDTH), lambda i: (jnp.maximum(i * per_tile - 1, 0), 0)),
                     pl.BlockSpec((CONV_HALO, CONV_WIDTH),
                                  lambda i: (jnp.minimum((i + 1) * per_tile, n_halo - 1), 0))]
        args += [u, u]
    in_specs += [pl.BlockSpec((1, 1, 1, N_MOD * D_MODEL), lambda i: (layer, row(i), 0, 0)),
                 _resident((1, D_MODEL)), _resident((1, D_MODEL)), _resident((1, D_MODEL))]
    in_specs += [_resident(p.shape) for p in conv_params] + [_resident(w.shape) for w in weights]
    n_slab = CONV_WIDTH // LANES
    return pl.pallas_call(
        functools.partial(_post_kernel, last=last, seg=seg, tiles_per_seq=tiles_per_seq),
        grid=(n_tok // tile,),
        in_specs=in_specs,
        out_specs=tok(D_MODEL),
        out_shape=jax.ShapeDtypeStruct((n_tok, D_MODEL), F32),
        scratch_shapes=[pltpu.VMEM((n_slab, (tile // seg) * (seg + 2 * CONV_HALO), LANES), F32),
                        pltpu.VMEM((n_slab, tile, LANES), F32)],
        compiler_params=_params(1),
        name="post_latent" if latent else "post_ctx",
    )(*args, mod, norm1, norm2, final_norm, *conv_params, *weights)


def _rope_tables(n_tokens):
    n_rows = n_tokens // GRID_W
    row = jnp.repeat(jnp.arange(n_rows), GRID_W).astype(F32)
    col = jnp.tile(jnp.arange(GRID_W), n_rows).astype(F32)
    axis_dim = HEAD_DIM // 2
    freqs = ROPE_THETA ** (-jnp.arange(0, axis_dim, 2, dtype=F32) / axis_dim)
    ang_r = row[:, None] * freqs[None, :]
    ang_c = col[:, None] * freqs[None, :]
    ang = jnp.concatenate([ang_r, ang_r, ang_c, ang_c], axis=-1)
    sign = jnp.tile(jnp.repeat(jnp.array([-1.0, 1.0], F32), HEAD_DIM // 4), 2)
    return jnp.tile(jnp.cos(ang), (1, 2)), jnp.tile(jnp.sin(ang) * sign, (1, 2))


def kernel(x_prompt, x_sample, cache_diff_k, cache_diff_v, cache_gqa_k, cache_gqa_v, c, c_ctx, w_ada, b_ada, norm1, norm2, w_in, diff_lq1, diff_lk1, diff_lq2, diff_lk2, diff_subln, w_diff_o, gqa_q_norm, gqa_k_norm, w_gqa_o, conv_dw, conv_dw_b, conv_ln_g, conv_ln_b, w_conv_o, w_o, w_mlp1, w_mlp2, final_norm):
    n_ctx, s_ctx, _ = x_prompt.shape
    n_lat, s_lat, _ = x_sample.shape
    past = cache_diff_k.shape[2]
    assert n_lat + 1 <= MOD_ROWS

    cvecs = jnp.concatenate([c_ctx[None], c, jnp.zeros((MOD_ROWS - 1 - n_lat, D_MODEL), F32)], axis=0)
    mod = _modulation(cvecs, w_ada, b_ada).reshape(DEPTH, MOD_ROWS, 1, N_MOD * D_MODEL)

    caches = (cache_diff_k.reshape(n_lat, DEPTH, past, 512), cache_diff_v.reshape(n_lat, DEPTH, past, 512),
              cache_gqa_k.reshape(n_lat, DEPTH, past, GQA_KV_WIDTH),
              cache_gqa_v.reshape(n_lat, DEPTH, past, GQA_KV_WIDTH))
    rope_tabs = _rope_tables(s_lat)
    row_vec = lambda p: p.reshape(1, -1)
    fn = row_vec(final_norm)

    groups = (dict(latent=False, n_seq=n_ctx, seq=s_ctx, tile=512, q_tile=s_ctx),
              dict(latent=True, n_seq=n_lat, seq=s_lat, tile=512, q_tile=256))
    xs = [x_prompt.reshape(n_ctx * s_ctx, D_MODEL), x_sample.reshape(n_lat * s_lat, D_MODEL)]
    new_cache = []

    for l in range(DEPTH):
        w_mix = w_in[l, :, :MIX_WIDTH].astype(BF16)
        post_w = [w_in[l, :, MIX_WIDTH:].astype(BF16), w_diff_o[l].astype(BF16), w_gqa_o[l].astype(BF16),
                  w_conv_o[l].astype(BF16), w_o[l].astype(BF16), w_mlp1[l].astype(BF16),
                  w_mlp2[l].astype(BF16)]
        qn = row_vec(jnp.tile(gqa_q_norm[l], 2))
        kn = row_vec(jnp.tile(gqa_k_norm[l], 2))
        lam_params = jnp.stack([diff_lq1[l], diff_lk1[l], diff_lq2[l], diff_lk2[l]])
        n1, n2 = row_vec(norm1[l]), row_vec(norm2[l])
        conv_params = [conv_dw[l], row_vec(conv_dw_b[l]), row_vec(conv_ln_g[l]), row_vec(conv_ln_b[l])]
        for gi, g in enumerate(groups):
            latent, n_seq, seq = g["latent"], g["n_seq"], g["seq"]
            outs = _pre(xs[gi], mod, l, n1, w_mix, qn, kn, rope_tabs, latent=latent, seq=seq, tile=g["tile"])
            if not latent:
                new_cache.append(outs[7:])
            da, ga = _attention(outs[:6], caches, l, lam_params, row_vec(diff_subln[l]), latent=latent,
                                n_seq=n_seq, seq=seq, q_tile=g["q_tile"])
            xs[gi] = _post(xs[gi], da, ga, outs[6], mod, l, n1, n2, fn, conv_params, post_w, latent=latent,
                           seq=seq, tile=g["tile"], last=(l == DEPTH - 1))

    stack = lambda k, shape: jnp.stack([new_cache[l][k].reshape((n_ctx, s_ctx) + shape) for l in range(DEPTH)], axis=1)
    return (xs[0].reshape(n_ctx, s_ctx, D_MODEL), xs[1].reshape(n_lat, s_lat, D_MODEL),
            stack(0, (DIFF_HEADS, 2, HEAD_DIM)), stack(1, (DIFF_HEADS, 2 * HEAD_DIM)),
            stack(2, (GQA_KV_HEADS, HEAD_DIM)), stack(3, (GQA_KV_HEADS, HEAD_DIM)))
```

```python
import functools
import math

import jax
import jax.numpy as jnp
from jax import lax
from jax.experimental import pallas as pl
from jax.experimental.pallas import tpu as pltpu

D_MODEL = 1024
DEPTH = 2
GRID_W = 64
ROPE_THETA = 10000.0
NORM_EPS = 1e-6

DIFF_HEADS = 4
HEAD_DIM = 64
DIFF_WIDTH = 512
GQA_KV_HEADS = 2
GQA_WIDTH = 512
GQA_KV_WIDTH = 128
CONV_WIDTH = 512
CONV_KSIZE = 31
CONV_HALO = 16
N_BRANCH = 3
MLP_HIDDEN = 4 * D_MODEL
N_MOD = 6

MIX_WIDTH = 3 * 512 + 512 + 2 * 128 + 2 * CONV_WIDTH
GATE_WIDTH = N_BRANCH * D_MODEL

LANES = 128
SUBLANES = 8
CONV_ROW_STRIDE = 4
MXU_WIDTH = 256
MOD_ROWS = 8
VMEM_LIMIT = 56 * 1024 * 1024

F32 = jnp.float32
BF16 = jnp.bfloat16


def _dot(a, b):
    return jnp.dot(a, b, preferred_element_type=F32)


def _dot_nt(a, b):
    return lax.dot_general(a, b, (((1,), (1,)), ((), ())), preferred_element_type=F32)


def _rms(x, gain):
    return x * lax.rsqrt(jnp.mean(x * x, axis=-1, keepdims=True) + NORM_EPS) * gain


def _modulated_norm(x, gain, shift, scale):
    return (_rms(x, gain) * (1.0 + scale) + shift).astype(BF16)


def _lane_iota(shape):
    return lax.broadcasted_iota(jnp.int32, shape, len(shape) - 1)


def _resident(shape):
    nd = len(shape)
    return pl.BlockSpec(shape, lambda *_: (0,) * nd, pipeline_mode=pl.Buffered(1))


def _params(n_axes):
    return pltpu.CompilerParams(dimension_semantics=("arbitrary",) * n_axes,
                                vmem_limit_bytes=VMEM_LIMIT)


def _mod_kernel(c_ref, w_ref, b_ref, o_ref):
    c = c_ref[...]
    s = (c * jax.nn.sigmoid(c)).astype(BF16)
    o_ref[0] = _dot(s, w_ref[0].astype(BF16)) + b_ref[0]


def _modulation(cvecs, w_ada, b_ada):
    width = N_MOD * D_MODEL
    tn = 1536
    return pl.pallas_call(
        _mod_kernel,
        grid=(DEPTH, width // tn),
        in_specs=[pl.BlockSpec((MOD_ROWS, D_MODEL), lambda l, j: (0, 0)),
                  pl.BlockSpec((1, D_MODEL, tn), lambda l, j: (l, 0, j)),
                  pl.BlockSpec((1, 1, tn), lambda l, j: (l, 0, j))],
        out_specs=pl.BlockSpec((1, MOD_ROWS, tn), lambda l, j: (l, 0, j)),
        out_shape=jax.ShapeDtypeStruct((DEPTH, MOD_ROWS, width), F32),
        compiler_params=_params(2),
        name="adaln_mod",
    )(cvecs, w_ada, b_ada.reshape(DEPTH, 1, width))


def _rope(x, cos, sin_signed, first_half):
    rot = jnp.where(first_half, pltpu.roll(x, LANES - 16, 1), pltpu.roll(x, 16, 1))
    return x * cos + rot * sin_signed


def _head_mean_sq(x):
    width = min(x.shape[1], MXU_WIDTH)
    r = lax.broadcasted_iota(jnp.int32, (width, width), 0) // HEAD_DIM
    c = lax.broadcasted_iota(jnp.int32, (width, width), 1) // HEAD_DIM
    blockdiag = jnp.where(r == c, 1.0 / HEAD_DIM, 0.0).astype(BF16)
    parts = []
    for lo in range(0, x.shape[1], width):
        sq = x[:, lo:lo + width] * x[:, lo:lo + width]
        hi = sq.astype(BF16)
        rest = (sq - hi.astype(F32)).astype(BF16)
        parts.append(_dot(hi, blockdiag) + _dot(rest, blockdiag))
    return parts[0] if len(parts) == 1 else jnp.concatenate(parts, axis=1)


def _store_cache(ref, val):
    seq = ref.shape[2]
    for b in range(ref.shape[0]):
        ref[b, 0] = val[b * seq:(b + 1) * seq]


def _pre_kernel(*refs, latent, n_aliased):
    if latent:
        (x_ref, mod_ref, n1_ref, w_ref, qn_ref, kn_ref, cos_ref, sin_ref,
         qd_ref, kd_ref, vd_ref, qg_ref, kk_ref, vv_ref, u_ref) = refs
    else:
        x_ref, mod_ref, n1_ref, w_ref, qn_ref, kn_ref = refs[:6]
        (qd_ref, kd_ref, vd_ref, qg_ref, kk_ref, vv_ref, u_ref,
         ndk_ref, ndv_ref, ngk_ref, ngv_ref) = refs[6 + n_aliased:]

    mod = mod_ref[0, 0]
    h = _modulated_norm(x_ref[...], n1_ref[...], mod[:, 0:D_MODEL], mod[:, D_MODEL:2 * D_MODEL])
    rows = h.shape[0]

    lane = _lane_iota((rows, LANES))
    low_half = lane < HEAD_DIM
    if latent:
        cos, sin_signed = cos_ref[...], sin_ref[...]
        first_half = ((lane % HEAD_DIM) // 16) % 2 == 0
        rope = lambda t: _rope(t, cos, sin_signed, first_half)
    else:
        rope = lambda t: t

    qk_scale = HEAD_DIM ** -0.5 * math.log2(math.e)

    def proj(lo, width):
        return _dot(h, w_ref[:, lo:lo + width])

    def slabs(t):
        return [t[:, s:s + LANES] for s in range(0, t.shape[1], LANES)]

    def store_slabs(ref, parts):
        for s, p in enumerate(parts):
            ref[:, s * LANES:(s + 1) * LANES] = p.astype(ref.dtype)

    store_slabs(qd_ref, [rope(t) * qk_scale for t in slabs(proj(0, DIFF_WIDTH))])
    dk = proj(DIFF_WIDTH, DIFF_WIDTH)
    dv = proj(2 * DIFF_WIDTH, DIFF_WIDTH)
    if not latent:
        _store_cache(ndk_ref, dk)
        _store_cache(ndv_ref, dv)
    store_slabs(kd_ref, [rope(t) for t in slabs(dk)])
    vd_ref[...] = dv.astype(BF16)

    gq = proj(3 * DIFF_WIDTH, GQA_WIDTH)
    gq = gq * lax.rsqrt(_head_mean_sq(gq) + NORM_EPS)
    qn = qn_ref[...]
    store_slabs(qg_ref, [rope(t * qn) * qk_scale for t in slabs(gq)])

    gkv = proj(3 * DIFF_WIDTH + GQA_WIDTH, 2 * GQA_KV_WIDTH)
    k, v = gkv[:, :GQA_KV_WIDTH], gkv[:, GQA_KV_WIDTH:]
    k = k * lax.rsqrt(_head_mean_sq(k) + NORM_EPS) * kn_ref[...]
    if not latent:
        _store_cache(ngk_ref, k)
        _store_cache(ngv_ref, v)
    k = rope(k)
    k_sw, v_sw = pltpu.roll(k, HEAD_DIM, 1), pltpu.roll(v, HEAD_DIM, 1)
    store_slabs(kk_ref, [jnp.where(low_half, k, k_sw), jnp.where(low_half, k_sw, k)])
    store_slabs(vv_ref, [jnp.where(low_half, v, v_sw), jnp.where(low_half, v_sw, v)])

    cv = proj(3 * DIFF_WIDTH + GQA_WIDTH + 2 * GQA_KV_WIDTH, 2 * CONV_WIDTH)
    u_ref[...] = cv[:, :CONV_WIDTH] * jax.nn.sigmoid(cv[:, CONV_WIDTH:])


def _pre(x2d, mod, layer, norm1, w_mix, qn, kn, rope_tabs, new_cache, *, latent, seq, tile):
    n_tok = x2d.shape[0]
    tiles_per_seq = seq // tile if latent else 1
    row = (lambda i: 1 + i // tiles_per_seq) if latent else (lambda i: 0)
    tok = lambda w: pl.BlockSpec((tile, w), lambda i: (i, 0))
    in_specs = [tok(D_MODEL),
                pl.BlockSpec((1, 1, 1, N_MOD * D_MODEL), lambda i: (layer, row(i), 0, 0)),
                _resident((1, D_MODEL)), _resident((D_MODEL, MIX_WIDTH)),
                _resident((1, LANES)), _resident((1, LANES))]
    args = [x2d, mod, norm1, w_mix, qn, kn]
    widths = [(512, BF16), (512, BF16), (512, BF16), (512, BF16), (256, BF16), (256, BF16), (512, F32)]
    out_specs = [tok(w) for w, _ in widths]
    out_shape = [jax.ShapeDtypeStruct((n_tok, w), dt) for w, dt in widths]
    aliases = {}
    if latent:
        in_specs += [pl.BlockSpec((tile, LANES), lambda i: (i % tiles_per_seq, 0))] * 2
        args += list(rope_tabs)
    else:
        per_tile = tile // seq
        for k, w in enumerate((512, 512, GQA_KV_WIDTH, GQA_KV_WIDTH)):
            if new_cache is not None:
                aliases[len(args)] = len(out_shape)
                in_specs.append(pl.BlockSpec(memory_space=pl.ANY))
                args.append(new_cache[k])
            out_specs.append(pl.BlockSpec((per_tile, 1, seq, w), lambda i: (i, layer, 0, 0)))
            out_shape.append(jax.ShapeDtypeStruct((n_tok // seq, DEPTH, seq, w), F32))
    return pl.pallas_call(
        functools.partial(_pre_kernel, latent=latent, n_aliased=len(aliases)),
        grid=(n_tok // tile,),
        in_specs=in_specs,
        out_specs=out_specs,
        out_shape=out_shape,
        input_output_aliases=aliases,
        compiler_params=_params(1),
        name="pre_latent" if latent else "pre_ctx",
    )(*args)


def _attend(q, keys, values):
    scores = [_dot_nt(q, k) for k in keys]
    m = functools.reduce(jnp.maximum, [jnp.max(s, axis=-1, keepdims=True) for s in scores])
    return functools.reduce(jnp.add, [_dot(jnp.exp2(s - m).astype(BF16), v) for s, v in zip(scores, values)])


def _attn_kernel(*refs, latent, lam_init, cast_splits):
    n_core = 12 if latent else 8
    if latent:
        (qd_ref, kd_ref, vd_ref, qg_ref, kk_ref, vv_ref, cdk_ref, cdv_ref, cgk_ref, cgv_ref,
         lp_ref, sub_ref) = refs[:n_core]
    else:
        qd_ref, kd_ref, vd_ref, qg_ref, kk_ref, vv_ref, lp_ref, sub_ref = refs[:n_core]
    cast_in = refs[n_core:n_core + len(cast_splits)]
    da_ref, ga_ref = refs[n_core + len(cast_splits):n_core + len(cast_splits) + 2]
    cast_out = iter(refs[n_core + len(cast_splits) + 2:])
    for w_ref, splits in zip(cast_in, cast_splits):
        for lo, hi in splits:
            next(cast_out)[...] = w_ref[0, :, lo:hi].astype(BF16)

    rows = qd_ref.shape[0]
    low_half = _lane_iota((rows, LANES)) < HEAD_DIM
    zero = jnp.zeros((), BF16)

    lp = lp_ref[...]
    lam = (jnp.exp(jnp.sum(lp[0:1] * lp[1:2], axis=-1, keepdims=True))
           - jnp.exp(jnp.sum(lp[2:3] * lp[3:4], axis=-1, keepdims=True)) + lam_init)
    sub_gain = sub_ref[...] * (1.0 - lam_init)

    def with_ones(v):
        return jnp.concatenate([v, jnp.ones_like(v)], axis=1)

    for h in range(DIFF_HEADS):
        sl = slice(h * LANES, (h + 1) * LANES)
        q = qd_ref[:, sl]
        keys, values = [kd_ref[:, sl]], [with_ones(vd_ref[:, sl])]
        if latent:
            keys.append(cdk_ref[0, 0, :, sl].astype(BF16))
            values.append(with_ones(cdv_ref[0, 0, :, sl].astype(BF16)))
        r1 = _attend(jnp.where(low_half, q, zero), keys, values)
        r2 = _attend(jnp.where(low_half, zero, q), keys, values)
        o = r1[:, :LANES] / r1[:, LANES:] - lam * (r2[:, :LANES] / r2[:, LANES:])
        da_ref[:, sl] = _rms(o, sub_gain).astype(BF16)

    def value_slabs(v_dup):
        low = _lane_iota(v_dup.shape) < HEAD_DIM
        one = jnp.ones((), v_dup.dtype)
        return jnp.where(low, v_dup, one), jnp.where(low, one, v_dup)

    if latent:
        ck, cv = cgk_ref[0, 0], cgv_ref[0, 0]
        ck_sw, cv_sw = pltpu.roll(ck, HEAD_DIM, 1), pltpu.roll(cv, HEAD_DIM, 1)
        low_c = _lane_iota(ck.shape) < HEAD_DIM
        cache_k = [jnp.where(low_c, ck, ck_sw).astype(BF16), jnp.where(low_c, ck_sw, ck).astype(BF16)]
        cache_v = [value_slabs(jnp.where(low_c, cv, cv_sw).astype(BF16)),
                   value_slabs(jnp.where(low_c, cv_sw, cv).astype(BF16))]
    for n in range(GQA_KV_HEADS):
        kv_sl = slice(n * LANES, (n + 1) * LANES)
        keys = [kk_ref[:, kv_sl]]
        values = [value_slabs(vv_ref[:, kv_sl])]
        if latent:
            keys.append(cache_k[n])
            values.append(cache_v[n])
        for j in range(2):
            sl = slice((2 * n + j) * LANES, (2 * n + j + 1) * LANES)
            q = qg_ref[:, sl]
            r_even = _attend(jnp.where(low_half, q, zero), keys, [v[0] for v in values])
            r_odd = _attend(jnp.where(low_half, zero, q), keys, [v[1] for v in values])
            r = jnp.where(low_half, r_even, r_odd)
            denom = jnp.where(low_half, pltpu.roll(r_even, HEAD_DIM, 1), pltpu.roll(r_odd, HEAD_DIM, 1))
            ga_ref[:, sl] = (r / denom).astype(BF16)


def _attention(pre_outs, caches, layer, lam_params, subln, casts, *, latent, n_seq, seq, q_tile):
    qd, kd, vd, qg, kk, vv = pre_outs
    tiles = seq // q_tile
    n_steps = n_seq * tiles
    q_spec = lambda w: pl.BlockSpec((q_tile, w), lambda b, i: (b * tiles + i, 0))
    kv_spec = lambda w: pl.BlockSpec((seq, w), lambda b, i: (b, 0))
    in_specs = [q_spec(512), kv_spec(512), kv_spec(512), q_spec(512), kv_spec(256), kv_spec(256)]
    args = [qd, kd, vd, qg, kk, vv]
    if latent:
        for c in caches:
            in_specs.append(pl.BlockSpec((1, 1) + c.shape[2:], lambda b, i: (b, layer, 0, 0)))
            args.append(c)
    in_specs += [_resident((4, HEAD_DIM)), _resident((1, LANES))]
    args += [lam_params, subln]
    out_specs = [q_spec(512), q_spec(512)]
    out_shape = [jax.ShapeDtypeStruct((n_seq * seq, 512), BF16)] * 2
    for w, w_layer, splits in casts:
        _, n_rows, n_cols = w.shape
        blk = n_rows // n_steps
        assert n_rows % n_steps == 0 and blk % 16 == 0
        in_specs.append(pl.BlockSpec((1, blk, n_cols), lambda b, i, w_layer=w_layer: (w_layer, b * tiles + i, 0)))
        args.append(w)
        for lo, hi in splits:
            out_specs.append(pl.BlockSpec((blk, hi - lo), lambda b, i: (b * tiles + i, 0)))
            out_shape.append(jax.ShapeDtypeStruct((n_rows, hi - lo), BF16))
    lam_init = 0.8 - 0.6 * math.exp(-0.3 * layer)
    return pl.pallas_call(
        functools.partial(_attn_kernel, latent=latent, lam_init=lam_init,
                          cast_splits=tuple(tuple(s) for _, _, s in casts)),
        grid=(n_seq, tiles),
        in_specs=in_specs,
        out_specs=out_specs,
        out_shape=out_shape,
        compiler_params=_params(2),
        name="attn_latent" if latent else "attn_ctx",
    )(*args)


def _conv_branch(u_ref, head, tail, w_ref, bias, gain, beta, win_ref, y_ref, seg):
    n_seg = u_ref.shape[0] // seg
    pitch = seg + 2 * CONV_HALO
    lane_slabs = [slice(s * LANES, (s + 1) * LANES) for s in range(CONV_WIDTH // LANES)]
    zeros = jnp.zeros((CONV_HALO, LANES), F32)
    for k in range(n_seg):
        for s, ls in enumerate(lane_slabs):
            win_ref[s, k * pitch:k * pitch + CONV_HALO, :] = head[:, ls] if k == 0 else zeros
            win_ref[s, k * pitch + CONV_HALO:(k + 1) * pitch - CONV_HALO, :] = u_ref[k * seg:(k + 1) * seg, ls]
            win_ref[s, (k + 1) * pitch - CONV_HALO:(k + 1) * pitch, :] = tail[:, ls] if k == n_seg - 1 else zeros

    first = CONV_HALO - CONV_KSIZE // 2
    group = 16
    rows_per_group = group * SUBLANES
    for k in range(n_seg):
        for s, ls in enumerate(lane_slabs):
            for r0 in range(0, seg, rows_per_group):
                offs = [r + t for r in range(r0, r0 + rows_per_group, SUBLANES * CONV_ROW_STRIDE)
                        for t in range(CONV_ROW_STRIDE)]
                accs = [jnp.zeros((SUBLANES, LANES), F32)] * group
                for j in range(CONV_KSIZE):
                    w_tap = jnp.broadcast_to(w_ref[j:j + 1, ls], (SUBLANES, LANES))
                    for a, off in enumerate(offs):
                        start = k * pitch + off + first + j
                        accs[a] = accs[a] + win_ref[s, pl.ds(start, SUBLANES, stride=CONV_ROW_STRIDE), :] * w_tap
                for a, off in enumerate(offs):
                    y_ref[s, pl.ds(k * seg + off, SUBLANES, stride=CONV_ROW_STRIDE), :] = accs[a]
    acc = jnp.concatenate([y_ref[s] for s in range(len(lane_slabs))], axis=1) + bias
    mu = jnp.mean(acc, axis=-1, keepdims=True)
    xc = acc - mu
    y = xc * lax.rsqrt(jnp.mean(xc * xc, axis=-1, keepdims=True) + NORM_EPS) * gain + beta
    return y * jax.nn.sigmoid(y)


def _post_kernel(*refs, last, seg, tiles_per_seq):
    halo = tiles_per_seq > 1
    if halo:
        x_ref, da_ref, ga_ref, u_ref, uprev_ref, unext_ref = refs[:6]
        refs = refs[6:]
    else:
        x_ref, da_ref, ga_ref, u_ref = refs[:4]
        refs = refs[4:]
    (mod_ref, n1_ref, n2_ref, fn_ref, cw_ref, cb_ref, cg_ref, cbeta_ref,
     wg_ref, wda_ref, wga_ref, wco_ref, wo_ref, w1_ref, w2_ref, o_ref, win_ref, y_ref) = refs

    if halo:
        pos = pl.program_id(0) % tiles_per_seq
        head = jnp.where(pos > 0, uprev_ref[...], 0.0)
        tail = jnp.where(pos < tiles_per_seq - 1, unext_ref[...], 0.0)
    else:
        head = tail = jnp.zeros((CONV_HALO, CONV_WIDTH), F32)
    ca = _conv_branch(u_ref, head, tail, cw_ref, cb_ref[...], cg_ref[...], cbeta_ref[...], win_ref, y_ref, seg)

    x = x_ref[...]
    mod = mod_ref[0, 0]
    m = lambda k: mod[:, k * D_MODEL:(k + 1) * D_MODEL]
    h = _modulated_norm(x, n1_ref[...], m(0), m(1))

    branches = ((da_ref[...], wda_ref), (ga_ref[...], wga_ref), (ca.astype(BF16), wco_ref))
    merged = None
    for j, (act, w_ref) in enumerate(branches):
        gate = jax.nn.sigmoid(_dot(h, wg_ref[:, j * D_MODEL:(j + 1) * D_MODEL]))
        term = gate * _dot(act, w_ref[...])
        merged = term if merged is None else merged + term
    x = x + m(2) * _dot(merged.astype(BF16), wo_ref[...])

    h2 = _modulated_norm(x, n2_ref[...], m(3), m(4))
    hid = 1024
    f = None
    for c in range(0, MLP_HIDDEN, hid):
        a = jnp.maximum(_dot(h2, w1_ref[:, c:c + hid]), 0.0)
        term = _dot((a * a).astype(BF16), w2_ref[c:c + hid, :])
        f = term if f is None else f + term
    x = x + m(5) * f
    o_ref[...] = _rms(x, fn_ref[...]) if last else x


def _post(x2d, da, ga, u, mod, layer, norm1, norm2, final_norm, conv_params, weights, *, latent, seq, tile, last):
    n_tok = x2d.shape[0]
    assert seq % tile == 0 or tile % seq == 0
    tiles_per_seq = max(seq // tile, 1)
    seg = min(seq, tile)
    row = (lambda i: 1 + i // tiles_per_seq) if latent else (lambda i: 0)
    tok = lambda w: pl.BlockSpec((tile, w), lambda i: (i, 0))
    in_specs = [tok(D_MODEL), tok(512), tok(512), tok(CONV_WIDTH)]
    args = [x2d, da, ga, u]
    if tiles_per_seq > 1:
        per_tile, n_halo = tile // CONV_HALO, n_tok // CONV_HALO
        in_specs += [pl.BlockSpec((CONV_HALO, CONV_WIDTH), lambda i: (jnp.maximum(i * per_tile - 1, 0), 0)),
                     pl.BlockSpec((CONV_HALO, CONV_WIDTH),
                                  lambda i: (jnp.minimum((i + 1) * per_tile, n_halo - 1), 0))]
        args += [u, u]
    in_specs += [pl.BlockSpec((1, 1, 1, N_MOD * D_MODEL), lambda i: (layer, row(i), 0, 0)),
                 _resident((1, D_MODEL)), _resident((1, D_MODEL)), _resident((1, D_MODEL))]
    in_specs += [_resident(p.shape) for p in conv_params] + [_resident(w.shape) for w in weights]
    n_slab = CONV_WIDTH // LANES
    return pl.pallas_call(
        functools.partial(_post_kernel, last=last, seg=seg, tiles_per_seq=tiles_per_seq),
        grid=(n_tok // tile,),
        in_specs=in_specs,
        out_specs=tok(D_MODEL),
        out_shape=jax.ShapeDtypeStruct((n_tok, D_MODEL), F32),
        scratch_shapes=[pltpu.VMEM((n_slab, (tile // seg) * (seg + 2 * CONV_HALO), LANES), F32),
                        pltpu.VMEM((n_slab, tile, LANES), F32)],
        compiler_params=_params(1),
        name="post_latent" if latent else "post_ctx",
    )(*args, mod, norm1, norm2, final_norm, *conv_params, *weights)


def _rope_tables(n_tokens):
    n_rows = n_tokens // GRID_W
    row = jnp.repeat(jnp.arange(n_rows), GRID_W).astype(F32)
    col = jnp.tile(jnp.arange(GRID_W), n_rows).astype(F32)
    axis_dim = HEAD_DIM // 2
    freqs = ROPE_THETA ** (-jnp.arange(0, axis_dim, 2, dtype=F32) / axis_dim)
    ang_r = row[:, None] * freqs[None, :]
    ang_c = col[:, None] * freqs[None, :]
    ang = jnp.concatenate([ang_r, ang_r, ang_c, ang_c], axis=-1)
    sign = jnp.tile(jnp.repeat(jnp.array([-1.0, 1.0], F32), HEAD_DIM // 4), 2)
    return jnp.tile(jnp.cos(ang), (1, 2)), jnp.tile(jnp.sin(ang) * sign, (1, 2))


def kernel(x_prompt, x_sample, cache_diff_k, cache_diff_v, cache_gqa_k, cache_gqa_v, c, c_ctx, w_ada, b_ada, norm1, norm2, w_in, diff_lq1, diff_lk1, diff_lq2, diff_lk2, diff_subln, w_diff_o, gqa_q_norm, gqa_k_norm, w_gqa_o, conv_dw, conv_dw_b, conv_ln_g, conv_ln_b, w_conv_o, w_o, w_mlp1, w_mlp2, final_norm):
    n_ctx, s_ctx, _ = x_prompt.shape
    n_lat, s_lat, _ = x_sample.shape
    past = cache_diff_k.shape[2]
    assert n_lat + 1 <= MOD_ROWS

    cvecs = jnp.concatenate([c_ctx[None], c, jnp.zeros((MOD_ROWS - 1 - n_lat, D_MODEL), F32)], axis=0)
    mod = _modulation(cvecs, w_ada, b_ada).reshape(DEPTH, MOD_ROWS, 1, N_MOD * D_MODEL)

    caches = (cache_diff_k.reshape(n_lat, DEPTH, past, 512), cache_diff_v.reshape(n_lat, DEPTH, past, 512),
              cache_gqa_k.reshape(n_lat, DEPTH, past, GQA_KV_WIDTH),
              cache_gqa_v.reshape(n_lat, DEPTH, past, GQA_KV_WIDTH))
    rope_tabs = _rope_tables(s_lat)
    row_vec = lambda p: p.reshape(1, -1)
    fn = row_vec(final_norm)

    groups = (dict(latent=False, n_seq=n_ctx, seq=s_ctx, tile=512, q_tile=s_ctx),
              dict(latent=True, n_seq=n_lat, seq=s_lat, tile=512, q_tile=256))
    xs = [x_prompt.reshape(n_ctx * s_ctx, D_MODEL), x_sample.reshape(n_lat * s_lat, D_MODEL)]
    new_cache = None

    split_in = ((0, MIX_WIDTH), (MIX_WIDTH, MIX_WIDTH + GATE_WIDTH))
    w_mix, w_gate = (w_in[0, :, lo:hi].astype(BF16) for lo, hi in split_in)
    whole = lambda w: ((0, w.shape[2]),)

    for l in range(DEPTH):
        qn = row_vec(jnp.tile(gqa_q_norm[l], 2))
        kn = row_vec(jnp.tile(gqa_k_norm[l], 2))
        lam_params = jnp.stack([diff_lq1[l], diff_lk1[l], diff_lq2[l], diff_lk2[l]])
        n1, n2 = row_vec(norm1[l]), row_vec(norm2[l])
        conv_params = [conv_dw[l], row_vec(conv_dw_b[l]), row_vec(conv_ln_g[l]), row_vec(conv_ln_b[l])]
        casts = [(w, l, whole(w)) for w in (w_diff_o, w_gqa_o, w_conv_o, w_o, w_mlp1, w_mlp2)]
        if l + 1 < DEPTH:
            casts.append((w_in, l + 1, split_in))

        pre_outs = []
        for gi, g in enumerate(groups):
            outs = _pre(xs[gi], mod, l, n1, w_mix, qn, kn, rope_tabs, new_cache, latent=g["latent"],
                        seq=g["seq"], tile=g["tile"])
            if not g["latent"]:
                new_cache = outs[7:]
            pre_outs.append(outs)
        attn_outs = [_attention(pre_outs[gi][:6], caches, l, lam_params, row_vec(diff_subln[l]),
                                casts if g["latent"] else [], latent=g["latent"], n_seq=g["n_seq"],
                                seq=g["seq"], q_tile=g["q_tile"]) for gi, g in enumerate(groups)]
        converted = attn_outs[1][2:]
        post_w = [w_gate] + list(converted[:6])
        for gi, g in enumerate(groups):
            da, ga = attn_outs[gi][:2]
            xs[gi] = _post(xs[gi], da, ga, pre_outs[gi][6], mod, l, n1, n2, fn, conv_params, post_w,
                           latent=g["latent"], seq=g["seq"], tile=g["tile"], last=(l == DEPTH - 1))
        if l + 1 < DEPTH:
            w_mix, w_gate = converted[6:]

    ndk, ndv, ngk, ngv = new_cache
    lead = (n_ctx, DEPTH, s_ctx)
    return (xs[0].reshape(n_ctx, s_ctx, D_MODEL), xs[1].reshape(n_lat, s_lat, D_MODEL),
            ndk.reshape(lead + (DIFF_HEADS, 2, HEAD_DIM)), ndv.reshape(lead + (DIFF_HEADS, 2 * HEAD_DIM)),
            ngk.reshape(lead + (GQA_KV_HEADS, HEAD_DIM)), ngv.reshape(lead + (GQA_KV_HEADS, HEAD_DIM)))
```

```python
import functools
import math

import jax
import jax.numpy as jnp
from jax import lax
from jax.experimental import pallas as pl
from jax.experimental.pallas import tpu as pltpu

D_MODEL = 1024
DEPTH = 2
GRID_W = 64
ROPE_THETA = 10000.0
NORM_EPS = 1e-6

DIFF_HEADS = 4
HEAD_DIM = 64
DIFF_WIDTH = 512
GQA_KV_HEADS = 2
GQA_WIDTH = 512
GQA_KV_WIDTH = 128
CONV_WIDTH = 512
CONV_KSIZE = 31
CONV_HALO = 16
N_BRANCH = 3
MLP_HIDDEN = 4 * D_MODEL
N_MOD = 6

MIX_WIDTH = 3 * 512 + 512 + 2 * 128 + 2 * CONV_WIDTH
GATE_WIDTH = N_BRANCH * D_MODEL

LANES = 128
SUBLANES = 8
CONV_ROW_STRIDE = 4
MXU_WIDTH = 256
MOD_ROWS = 8
VMEM_LIMIT = 56 * 1024 * 1024

F32 = jnp.float32
BF16 = jnp.bfloat16


def _dot(a, b):
    return jnp.dot(a, b, preferred_element_type=F32)


def _dot_nt(a, b):
    return lax.dot_general(a, b, (((1,), (1,)), ((), ())), preferred_element_type=F32)


def _rms(x, gain):
    return x * lax.rsqrt(jnp.mean(x * x, axis=-1, keepdims=True) + NORM_EPS) * gain


def _modulated_norm(x, gain, shift, scale):
    return (_rms(x, gain) * (1.0 + scale) + shift).astype(BF16)


def _lane_iota(shape):
    return lax.broadcasted_iota(jnp.int32, shape, len(shape) - 1)


def _resident(shape):
    nd = len(shape)
    return pl.BlockSpec(shape, lambda *_: (0,) * nd, pipeline_mode=pl.Buffered(1))


def _params(n_axes):
    return pltpu.CompilerParams(dimension_semantics=("arbitrary",) * n_axes,
                                vmem_limit_bytes=VMEM_LIMIT)


def _mod_kernel(c_ref, w_ref, b_ref, o_ref):
    c = c_ref[...]
    s = (c * jax.nn.sigmoid(c)).astype(BF16)
    o_ref[0] = _dot(s, w_ref[0].astype(BF16)) + b_ref[0]


def _modulation(cvecs, w_ada, b_ada):
    width = N_MOD * D_MODEL
    tn = 1536
    return pl.pallas_call(
        _mod_kernel,
        grid=(DEPTH, width // tn),
        in_specs=[pl.BlockSpec((MOD_ROWS, D_MODEL), lambda l, j: (0, 0)),
                  pl.BlockSpec((1, D_MODEL, tn), lambda l, j: (l, 0, j)),
                  pl.BlockSpec((1, 1, tn), lambda l, j: (l, 0, j))],
        out_specs=pl.BlockSpec((1, MOD_ROWS, tn), lambda l, j: (l, 0, j)),
        out_shape=jax.ShapeDtypeStruct((DEPTH, MOD_ROWS, width), F32),
        compiler_params=_params(2),
        name="adaln_mod",
    )(cvecs, w_ada, b_ada.reshape(DEPTH, 1, width))


def _rope(x, cos, sin_signed, first_half):
    rot = jnp.where(first_half, pltpu.roll(x, LANES - 16, 1), pltpu.roll(x, 16, 1))
    return x * cos + rot * sin_signed


def _head_mean_sq(x):
    width = min(x.shape[1], MXU_WIDTH)
    r = lax.broadcasted_iota(jnp.int32, (width, width), 0) // HEAD_DIM
    c = lax.broadcasted_iota(jnp.int32, (width, width), 1) // HEAD_DIM
    blockdiag = jnp.where(r == c, 1.0 / HEAD_DIM, 0.0).astype(BF16)
    parts = []
    for lo in range(0, x.shape[1], width):
        sq = x[:, lo:lo + width] * x[:, lo:lo + width]
        parts.append(_dot(sq.astype(BF16), blockdiag))
    return parts[0] if len(parts) == 1 else jnp.concatenate(parts, axis=1)


def _store_cache(ref, val):
    seq = ref.shape[2]
    for b in range(ref.shape[0]):
        ref[b, 0] = val[b * seq:(b + 1) * seq]


def _pre_kernel(*refs, latent, n_aliased, f32_weights):
    if f32_weights:
        *refs, w_bf16_ref = refs
    if latent:
        (x_ref, mod_ref, n1_ref, w_ref, qn_ref, kn_ref, cos_ref, sin_ref,
         qd_ref, kd_ref, vd_ref, qg_ref, kk_ref, vv_ref, u_ref) = refs
    else:
        x_ref, mod_ref, n1_ref, w_ref, qn_ref, kn_ref = refs[:6]
        (qd_ref, kd_ref, vd_ref, qg_ref, kk_ref, vv_ref, u_ref,
         ndk_ref, ndv_ref, ngk_ref, ngv_ref) = refs[6 + n_aliased:]

    if f32_weights:
        @pl.when(pl.program_id(0) == 0)
        def _():
            w_bf16_ref[...] = w_ref[0].astype(BF16)
        w_ref = w_bf16_ref

    mod = mod_ref[0, 0]
    h = _modulated_norm(x_ref[...], n1_ref[...], mod[:, 0:D_MODEL], mod[:, D_MODEL:2 * D_MODEL])
    rows = h.shape[0]

    lane = _lane_iota((rows, LANES))
    low_half = lane < HEAD_DIM
    if latent:
        cos, sin_signed = cos_ref[...], sin_ref[...]
        first_half = ((lane % HEAD_DIM) // 16) % 2 == 0
        rope = lambda t: _rope(t, cos, sin_signed, first_half)
    else:
        rope = lambda t: t

    qk_scale = HEAD_DIM ** -0.5 * math.log2(math.e)

    def proj(lo, width):
        return _dot(h, w_ref[:, lo:lo + width])

    def slabs(t):
        return [t[:, s:s + LANES] for s in range(0, t.shape[1], LANES)]

    def store_slabs(ref, parts):
        for s, p in enumerate(parts):
            ref[:, s * LANES:(s + 1) * LANES] = p.astype(ref.dtype)

    store_slabs(qd_ref, [rope(t) * qk_scale for t in slabs(proj(0, DIFF_WIDTH))])
    dk = proj(DIFF_WIDTH, DIFF_WIDTH)
    dv = proj(2 * DIFF_WIDTH, DIFF_WIDTH)
    if not latent:
        _store_cache(ndk_ref, dk)
        _store_cache(ndv_ref, dv)
    store_slabs(kd_ref, [rope(t) for t in slabs(dk)])
    vd_ref[...] = dv.astype(BF16)

    gq = proj(3 * DIFF_WIDTH, GQA_WIDTH)
    gq = gq * lax.rsqrt(_head_mean_sq(gq) + NORM_EPS)
    qn = qn_ref[...]
    store_slabs(qg_ref, [rope(t * qn) * qk_scale for t in slabs(gq)])

    gkv = proj(3 * DIFF_WIDTH + GQA_WIDTH, 2 * GQA_KV_WIDTH)
    k, v = gkv[:, :GQA_KV_WIDTH], gkv[:, GQA_KV_WIDTH:]
    k = k * lax.rsqrt(_head_mean_sq(k) + NORM_EPS) * kn_ref[...]
    if not latent:
        _store_cache(ngk_ref, k)
        _store_cache(ngv_ref, v)
    k = rope(k)
    k_sw, v_sw = pltpu.roll(k, HEAD_DIM, 1), pltpu.roll(v, HEAD_DIM, 1)
    store_slabs(kk_ref, [jnp.where(low_half, k, k_sw), jnp.where(low_half, k_sw, k)])
    store_slabs(vv_ref, [jnp.where(low_half, v, v_sw), jnp.where(low_half, v_sw, v)])

    cv = proj(3 * DIFF_WIDTH + GQA_WIDTH + 2 * GQA_KV_WIDTH, 2 * CONV_WIDTH)
    u_ref[...] = cv[:, :CONV_WIDTH] * jax.nn.sigmoid(cv[:, CONV_WIDTH:])


def _pre(x2d, mod, layer, norm1, w_mix, qn, kn, rope_tabs, new_cache, *, latent, seq, tile):
    n_tok = x2d.shape[0]
    tiles_per_seq = seq // tile if latent else 1
    row = (lambda i: 1 + i // tiles_per_seq) if latent else (lambda i: 0)
    tok = lambda w: pl.BlockSpec((tile, w), lambda i: (i, 0))
    f32_weights = w_mix.dtype == F32
    w_spec = (pl.BlockSpec((1, D_MODEL, MIX_WIDTH), lambda i: (layer, 0, 0), pipeline_mode=pl.Buffered(1))
              if f32_weights else _resident((D_MODEL, MIX_WIDTH)))
    in_specs = [tok(D_MODEL),
                pl.BlockSpec((1, 1, 1, N_MOD * D_MODEL), lambda i: (layer, row(i), 0, 0)),
                _resident((1, D_MODEL)), w_spec,
                _resident((1, LANES)), _resident((1, LANES))]
    args = [x2d, mod, norm1, w_mix, qn, kn]
    widths = [(512, BF16), (512, BF16), (512, BF16), (512, BF16), (256, BF16), (256, BF16), (512, F32)]
    out_specs = [tok(w) for w, _ in widths]
    out_shape = [jax.ShapeDtypeStruct((n_tok, w), dt) for w, dt in widths]
    aliases = {}
    if latent:
        in_specs += [pl.BlockSpec((tile, LANES), lambda i: (i % tiles_per_seq, 0))] * 2
        args += list(rope_tabs)
    else:
        per_tile = tile // seq
        for k, w in enumerate((512, 512, GQA_KV_WIDTH, GQA_KV_WIDTH)):
            if new_cache is not None:
                aliases[len(args)] = len(out_shape)
                in_specs.append(pl.BlockSpec(memory_space=pl.ANY))
                args.append(new_cache[k])
            out_specs.append(pl.BlockSpec((per_tile, 1, seq, w), lambda i: (i, layer, 0, 0)))
            out_shape.append(jax.ShapeDtypeStruct((n_tok // seq, DEPTH, seq, w), F32))
    return pl.pallas_call(
        functools.partial(_pre_kernel, latent=latent, n_aliased=len(aliases), f32_weights=f32_weights),
        grid=(n_tok // tile,),
        in_specs=in_specs,
        out_specs=out_specs,
        out_shape=out_shape,
        scratch_shapes=[pltpu.VMEM((D_MODEL, MIX_WIDTH), BF16)] if f32_weights else [],
        input_output_aliases=aliases,
        compiler_params=_params(1),
        name="pre_latent" if latent else "pre_ctx",
    )(*args)


def _attend(q, keys, values):
    scores = [_dot_nt(q, k) for k in keys]
    m = functools.reduce(jnp.maximum, [jnp.max(s, axis=-1, keepdims=True) for s in scores])
    return functools.reduce(jnp.add, [_dot(jnp.exp2(s - m).astype(BF16), v) for s, v in zip(scores, values)])


def _attn_kernel(*refs, latent, lam_init, cast_splits, n_seqs):
    n_core = 12 if latent else 8
    if latent:
        (qd_ref, kd_ref, vd_ref, qg_ref, kk_ref, vv_ref, cdk_ref, cdv_ref, cgk_ref, cgv_ref,
         lp_ref, sub_ref) = refs[:n_core]
    else:
        qd_ref, kd_ref, vd_ref, qg_ref, kk_ref, vv_ref, lp_ref, sub_ref = refs[:n_core]
    cast_in = refs[n_core:n_core + len(cast_splits)]
    da_ref, ga_ref = refs[n_core + len(cast_splits):n_core + len(cast_splits) + 2]
    cast_out = iter(refs[n_core + len(cast_splits) + 2:])
    for w_ref, splits in zip(cast_in, cast_splits):
        for lo, hi in splits:
            next(cast_out)[...] = w_ref[0, :, lo:hi].astype(BF16)

    rows = qd_ref.shape[0] // n_seqs
    kv_rows = kd_ref.shape[0] // n_seqs
    low_half = _lane_iota((rows, LANES)) < HEAD_DIM
    zero = jnp.zeros((), BF16)

    lp = lp_ref[...]
    lam = (jnp.exp(jnp.sum(lp[0:1] * lp[1:2], axis=-1, keepdims=True))
           - jnp.exp(jnp.sum(lp[2:3] * lp[3:4], axis=-1, keepdims=True)) + lam_init)
    sub_gain = sub_ref[...] * (1.0 - lam_init)

    def with_ones(v):
        return jnp.concatenate([v, jnp.ones_like(v)], axis=1)

    def value_slabs(v_dup):
        low = _lane_iota(v_dup.shape) < HEAD_DIM
        one = jnp.ones((), v_dup.dtype)
        return jnp.where(low, v_dup, one), jnp.where(low, one, v_dup)

    if latent:
        ck, cv = cgk_ref[0, 0], cgv_ref[0, 0]
        ck_sw, cv_sw = pltpu.roll(ck, HEAD_DIM, 1), pltpu.roll(cv, HEAD_DIM, 1)
        low_c = _lane_iota(ck.shape) < HEAD_DIM
        cache_k = [jnp.where(low_c, ck, ck_sw).astype(BF16), jnp.where(low_c, ck_sw, ck).astype(BF16)]
        cache_v = [value_slabs(jnp.where(low_c, cv, cv_sw).astype(BF16)),
                   value_slabs(jnp.where(low_c, cv_sw, cv).astype(BF16))]
    for s in range(n_seqs):
        qs = slice(s * rows, (s + 1) * rows)
        ks = slice(s * kv_rows, (s + 1) * kv_rows)
        for h in range(DIFF_HEADS):
            sl = slice(h * LANES, (h + 1) * LANES)
            q = qd_ref[qs, sl]
            keys, values = [kd_ref[ks, sl]], [with_ones(vd_ref[ks, sl])]
            if latent:
                keys.append(cdk_ref[0, 0, :, sl].astype(BF16))
                values.append(with_ones(cdv_ref[0, 0, :, sl].astype(BF16)))
            r1 = _attend(jnp.where(low_half, q, zero), keys, values)
            r2 = _attend(jnp.where(low_half, zero, q), keys, values)
            o = r1[:, :LANES] / r1[:, LANES:] - lam * (r2[:, :LANES] / r2[:, LANES:])
            da_ref[qs, sl] = _rms(o, sub_gain).astype(BF16)

        for n in range(GQA_KV_HEADS):
            kv_sl = slice(n * LANES, (n + 1) * LANES)
            keys = [kk_ref[ks, kv_sl]]
            values = [value_slabs(vv_ref[ks, kv_sl])]
            if latent:
                keys.append(cache_k[n])
                values.append(cache_v[n])
            for j in range(2):
                sl = slice((2 * n + j) * LANES, (2 * n + j + 1) * LANES)
                q = qg_ref[qs, sl]
                r_even = _attend(jnp.where(low_half, q, zero), keys, [v[0] for v in values])
                r_odd = _attend(jnp.where(low_half, zero, q), keys, [v[1] for v in values])
                r = jnp.where(low_half, r_even, r_odd)
                denom = jnp.where(low_half, pltpu.roll(r_even, HEAD_DIM, 1), pltpu.roll(r_odd, HEAD_DIM, 1))
                ga_ref[qs, sl] = (r / denom).astype(BF16)


def _attention(pre_outs, caches, layer, lam_params, subln, casts, *, latent, n_seq, seq, q_tile, seqs_per_step):
    qd, kd, vd, qg, kk, vv = pre_outs
    tiles = seq // q_tile
    assert seqs_per_step == 1 or tiles == 1
    n_seq //= seqs_per_step
    n_steps = n_seq * tiles
    q_spec = lambda w: pl.BlockSpec((seqs_per_step * q_tile, w), lambda b, i: (b * tiles + i, 0))
    kv_spec = lambda w: pl.BlockSpec((seqs_per_step * seq, w), lambda b, i: (b, 0))
    in_specs = [q_spec(512), kv_spec(512), kv_spec(512), q_spec(512), kv_spec(256), kv_spec(256)]
    args = [qd, kd, vd, qg, kk, vv]
    if latent:
        for c in caches:
            in_specs.append(pl.BlockSpec((1, 1) + c.shape[2:], lambda b, i: (b, layer, 0, 0)))
            args.append(c)
    in_specs += [_resident((4, HEAD_DIM)), _resident((1, LANES))]
    args += [lam_params, subln]
    out_specs = [q_spec(512), q_spec(512)]
    out_shape = [jax.ShapeDtypeStruct(qd.shape, BF16)] * 2
    for w, w_layer, splits in casts:
        _, n_rows, n_cols = w.shape
        blk = n_rows // n_steps
        assert n_rows % n_steps == 0 and blk % 16 == 0
        in_specs.append(pl.BlockSpec((1, blk, n_cols), lambda b, i, w_layer=w_layer: (w_layer, b * tiles + i, 0)))
        args.append(w)
        for lo, hi in splits:
            out_specs.append(pl.BlockSpec((blk, hi - lo), lambda b, i: (b * tiles + i, 0)))
            out_shape.append(jax.ShapeDtypeStruct((n_rows, hi - lo), BF16))
    lam_init = 0.8 - 0.6 * math.exp(-0.3 * layer)
    return pl.pallas_call(
        functools.partial(_attn_kernel, latent=latent, lam_init=lam_init,
                          cast_splits=tuple(tuple(s) for _, _, s in casts), n_seqs=seqs_per_step),
        grid=(n_seq, tiles),
        in_specs=in_specs,
        out_specs=out_specs,
        out_shape=out_shape,
        compiler_params=_params(2),
        name="attn_latent" if latent else "attn_ctx",
    )(*args)


def _conv_branch(u_ref, head, tail, w_ref, bias, gain, beta, win_ref, y_ref, seg):
    n_seg = u_ref.shape[0] // seg
    pitch = seg + 2 * CONV_HALO
    lane_slabs = [slice(s * LANES, (s + 1) * LANES) for s in range(CONV_WIDTH // LANES)]
    zeros = jnp.zeros((CONV_HALO, LANES), F32)
    for k in range(n_seg):
        for s, ls in enumerate(lane_slabs):
            win_ref[s, k * pitch:k * pitch + CONV_HALO, :] = head[:, ls] if k == 0 else zeros
            win_ref[s, k * pitch + CONV_HALO:(k + 1) * pitch - CONV_HALO, :] = u_ref[k * seg:(k + 1) * seg, ls]
            win_ref[s, (k + 1) * pitch - CONV_HALO:(k + 1) * pitch, :] = tail[:, ls] if k == n_seg - 1 else zeros

    first = CONV_HALO - CONV_KSIZE // 2
    group = 16
    rows_per_group = group * SUBLANES
    for k in range(n_seg):
        for s, ls in enumerate(lane_slabs):
            for r0 in range(0, seg, rows_per_group):
                offs = [r + t for r in range(r0, r0 + rows_per_group, SUBLANES * CONV_ROW_STRIDE)
                        for t in range(CONV_ROW_STRIDE)]
                accs = [jnp.zeros((SUBLANES, LANES), F32)] * group
                for j in range(CONV_KSIZE):
                    w_tap = jnp.broadcast_to(w_ref[j:j + 1, ls], (SUBLANES, LANES))
                    for a, off in enumerate(offs):
                        start = k * pitch + off + first + j
                        accs[a] = accs[a] + win_ref[s, pl.ds(start, SUBLANES, stride=CONV_ROW_STRIDE), :] * w_tap
                for a, off in enumerate(offs):
                    y_ref[s, pl.ds(k * seg + off, SUBLANES, stride=CONV_ROW_STRIDE), :] = accs[a]
    acc = jnp.concatenate([y_ref[s] for s in range(len(lane_slabs))], axis=1) + bias
    mu = jnp.mean(acc, axis=-1, keepdims=True)
    xc = acc - mu
    y = xc * lax.rsqrt(jnp.mean(xc * xc, axis=-1, keepdims=True) + NORM_EPS) * gain + beta
    return y * jax.nn.sigmoid(y)


def _post_kernel(*refs, last, seg, tiles_per_seq):
    halo = tiles_per_seq > 1
    if halo:
        x_ref, da_ref, ga_ref, u_ref, uprev_ref, unext_ref = refs[:6]
        refs = refs[6:]
    else:
        x_ref, da_ref, ga_ref, u_ref = refs[:4]
        refs = refs[4:]
    (mod_ref, n1_ref, n2_ref, fn_ref, cw_ref, cb_ref, cg_ref, cbeta_ref,
     wg_ref, wda_ref, wga_ref, wco_ref, wo_ref, w1_ref, w2_ref, o_ref, win_ref, y_ref) = refs

    if halo:
        pos = pl.program_id(0) % tiles_per_seq
        head = jnp.where(pos > 0, uprev_ref[...], 0.0)
        tail = jnp.where(pos < tiles_per_seq - 1, unext_ref[...], 0.0)
    else:
        head = tail = jnp.zeros((CONV_HALO, CONV_WIDTH), F32)
    ca = _conv_branch(u_ref, head, tail, cw_ref, cb_ref[...], cg_ref[...], cbeta_ref[...], win_ref, y_ref, seg)

    x = x_ref[...]
    mod = mod_ref[0, 0]
    m = lambda k: mod[:, k * D_MODEL:(k + 1) * D_MODEL]
    h = _modulated_norm(x, n1_ref[...], m(0), m(1))

    branches = ((da_ref[...], wda_ref), (ga_ref[...], wga_ref), (ca.astype(BF16), wco_ref))
    merged = None
    for j, (act, w_ref) in enumerate(branches):
        gate = jax.nn.sigmoid(_dot(h, wg_ref[:, j * D_MODEL:(j + 1) * D_MODEL]))
        term = gate * _dot(act, w_ref[...])
        merged = term if merged is None else merged + term
    x = x + m(2) * _dot(merged.astype(BF16), wo_ref[...])

    h2 = _modulated_norm(x, n2_ref[...], m(3), m(4))
    hid = 1024
    f = None
    for c in range(0, MLP_HIDDEN, hid):
        a = jnp.maximum(_dot(h2, w1_ref[:, c:c + hid]), 0.0)
        term = _dot((a * a).astype(BF16), w2_ref[c:c + hid, :])
        f = term if f is None else f + term
    x = x + m(5) * f
    o_ref[...] = _rms(x, fn_ref[...]) if last else x


def _post(x2d, da, ga, u, mod, layer, norm1, norm2, final_norm, conv_params, weights, *, latent, seq, tile, last):
    n_tok = x2d.shape[0]
    assert seq % tile == 0 or tile % seq == 0
    tiles_per_seq = max(seq // tile, 1)
    seg = min(seq, tile)
    row = (lambda i: 1 + i // tiles_per_seq) if latent else (lambda i: 0)
    tok = lambda w: pl.BlockSpec((tile, w), lambda i: (i, 0))
    in_specs = [tok(D_MODEL), tok(512), tok(512), tok(CONV_WIDTH)]
    args = [x2d, da, ga, u]
    if tiles_per_seq > 1:
        per_tile, n_halo = tile // CONV_HALO, n_tok // CONV_HALO
        in_specs += [pl.BlockSpec((CONV_HALO, CONV_WIDTH), lambda i: (jnp.maximum(i * per_tile - 1, 0), 0)),
                     pl.BlockSpec((CONV_HALO, CONV_WIDTH),
                                  lambda i: (jnp.minimum((i + 1) * per_tile, n_halo - 1), 0))]
        args += [u, u]
    in_specs += [pl.BlockSpec((1, 1, 1, N_MOD * D_MODEL), lambda i: (layer, row(i), 0, 0)),
                 _resident((1, D_MODEL)), _resident((1, D_MODEL)), _resident((1, D_MODEL))]
    in_specs += [_resident(p.shape) for p in conv_params] + [_resident(w.shape) for w in weights]
    n_slab = CONV_WIDTH // LANES
    return pl.pallas_call(
        functools.partial(_post_kernel, last=last, seg=seg, tiles_per_seq=tiles_per_seq),
        grid=(n_tok // tile,),
        in_specs=in_specs,
        out_specs=tok(D_MODEL),
        out_shape=jax.ShapeDtypeStruct((n_tok, D_MODEL), F32),
        scratch_shapes=[pltpu.VMEM((n_slab, (tile // seg) * (seg + 2 * CONV_HALO), LANES), F32),
                        pltpu.VMEM((n_slab, tile, LANES), F32)],
        compiler_params=_params(1),
        name="post_latent" if latent else "post_ctx",
    )(*args, mod, norm1, norm2, final_norm, *conv_params, *weights)


def _rope_tables(n_tokens):
    n_rows = n_tokens // GRID_W
    row = jnp.repeat(jnp.arange(n_rows), GRID_W).astype(F32)
    col = jnp.tile(jnp.arange(GRID_W), n_rows).astype(F32)
    axis_dim = HEAD_DIM // 2
    freqs = ROPE_THETA ** (-jnp.arange(0, axis_dim, 2, dtype=F32) / axis_dim)
    ang_r = row[:, None] * freqs[None, :]
    ang_c = col[:, None] * freqs[None, :]
    ang = jnp.concatenate([ang_r, ang_r, ang_c, ang_c], axis=-1)
    sign = jnp.tile(jnp.repeat(jnp.array([-1.0, 1.0], F32), HEAD_DIM // 4), 2)
    return jnp.tile(jnp.cos(ang), (1, 2)), jnp.tile(jnp.sin(ang) * sign, (1, 2))


def kernel(x_prompt, x_sample, cache_diff_k, cache_diff_v, cache_gqa_k, cache_gqa_v, c, c_ctx, w_ada, b_ada, norm1, norm2, w_in, diff_lq1, diff_lk1, diff_lq2, diff_lk2, diff_subln, w_diff_o, gqa_q_norm, gqa_k_norm, w_gqa_o, conv_dw, conv_dw_b, conv_ln_g, conv_ln_b, w_conv_o, w_o, w_mlp1, w_mlp2, final_norm):
    n_ctx, s_ctx, _ = x_prompt.shape
    n_lat, s_lat, _ = x_sample.shape
    past = cache_diff_k.shape[2]
    assert n_lat + 1 <= MOD_ROWS

    cvecs = jnp.concatenate([c_ctx[None], c, jnp.zeros((MOD_ROWS - 1 - n_lat, D_MODEL), F32)], axis=0)
    mod = _modulation(cvecs, w_ada, b_ada).reshape(DEPTH, MOD_ROWS, 1, N_MOD * D_MODEL)

    caches = (cache_diff_k.reshape(n_lat, DEPTH, past, 512), cache_diff_v.reshape(n_lat, DEPTH, past, 512),
              cache_gqa_k.reshape(n_lat, DEPTH, past, GQA_KV_WIDTH),
              cache_gqa_v.reshape(n_lat, DEPTH, past, GQA_KV_WIDTH))
    rope_tabs = _rope_tables(s_lat)
    row_vec = lambda p: p.reshape(1, -1)
    fn = row_vec(final_norm)

    groups = (dict(latent=False, n_seq=n_ctx, seq=s_ctx, tile=512, q_tile=s_ctx, seqs_per_step=2),
              dict(latent=True, n_seq=n_lat, seq=s_lat, tile=512, q_tile=512, seqs_per_step=1))
    xs = [x_prompt.reshape(n_ctx * s_ctx, D_MODEL), x_sample.reshape(n_lat * s_lat, D_MODEL)]
    new_cache = None

    mix_cols, gate_cols = (0, MIX_WIDTH), (MIX_WIDTH, MIX_WIDTH + GATE_WIDTH)
    whole = lambda w: ((0, w.shape[2]),)
    w_mix, w_gate = w_in, None

    for l in range(DEPTH):
        qn = row_vec(jnp.tile(gqa_q_norm[l], 2))
        kn = row_vec(jnp.tile(gqa_k_norm[l], 2))
        lam_params = jnp.stack([diff_lq1[l], diff_lk1[l], diff_lq2[l], diff_lk2[l]])
        n1, n2 = row_vec(norm1[l]), row_vec(norm2[l])
        conv_params = [conv_dw[l], row_vec(conv_dw_b[l]), row_vec(conv_ln_g[l]), row_vec(conv_ln_b[l])]
        next_in = [(w_in, l + 1, (mix_cols, gate_cols))] if l + 1 < DEPTH else []
        this_gate = [(w_in, l, (gate_cols,))] if w_gate is None else []
        casts = ([(w_mlp1, l, whole(w_mlp1))] + next_in,
                 [(w, l, whole(w)) for w in (w_diff_o, w_gqa_o, w_conv_o, w_o, w_mlp2)] + this_gate)

        pre_outs = []
        for gi, g in enumerate(groups):
            outs = _pre(xs[gi], mod, l, n1, w_mix, qn, kn, rope_tabs, new_cache, latent=g["latent"],
                        seq=g["seq"], tile=g["tile"])
            if not g["latent"]:
                new_cache = outs[7:]
            pre_outs.append(outs)
        attn_outs = [_attention(pre_outs[gi][:6], caches, l, lam_params, row_vec(diff_subln[l]), casts[gi],
                                latent=g["latent"], n_seq=g["n_seq"], seq=g["seq"], q_tile=g["q_tile"],
                                seqs_per_step=g["seqs_per_step"]) for gi, g in enumerate(groups)]
        w_mlp1_b, *next_in_b = attn_outs[0][2:]
        w_da_b, w_ga_b, w_co_b, w_o_b, w_mlp2_b, *this_gate_b = attn_outs[1][2:]
        if this_gate_b:
            w_gate = this_gate_b[0]
        post_w = [w_gate, w_da_b, w_ga_b, w_co_b, w_o_b, w_mlp1_b, w_mlp2_b]
        for gi, g in enumerate(groups):
            da, ga = attn_outs[gi][:2]
            xs[gi] = _post(xs[gi], da, ga, pre_outs[gi][6], mod, l, n1, n2, fn, conv_params, post_w,
                           latent=g["latent"], seq=g["seq"], tile=g["tile"], last=(l == DEPTH - 1))
        if next_in_b:
            w_mix, w_gate = next_in_b

    ndk, ndv, ngk, ngv = new_cache
    lead = (n_ctx, DEPTH, s_ctx)
    return (xs[0].reshape(n_ctx, s_ctx, D_MODEL), xs[1].reshape(n_lat, s_lat, D_MODEL),
            ndk.reshape(lead + (DIFF_HEADS, 2, HEAD_DIM)), ndv.reshape(lead + (DIFF_HEADS, 2 * HEAD_DIM)),
            ngk.reshape(lead + (GQA_KV_HEADS, HEAD_DIM)), ngv.reshape(lead + (GQA_KV_HEADS, HEAD_DIM)))
```

```python
import functools
import math

import jax
import jax.numpy as jnp
from jax import lax
from jax.experimental import pallas as pl
from jax.experimental.pallas import tpu as pltpu

D_MODEL = 1024
DEPTH = 2
GRID_W = 64
ROPE_THETA = 10000.0
NORM_EPS = 1e-6

DIFF_HEADS = 4
HEAD_DIM = 64
DIFF_WIDTH = 512
GQA_KV_HEADS = 2
GQA_WIDTH = 512
GQA_KV_WIDTH = 128
CONV_WIDTH = 512
CONV_KSIZE = 31
CONV_HALO = 16
N_BRANCH = 3
MLP_HIDDEN = 4 * D_MODEL
N_MOD = 6

MIX_WIDTH = 3 * 512 + 512 + 2 * 128 + 2 * CONV_WIDTH
GATE_WIDTH = N_BRANCH * D_MODEL

LANES = 128
SUBLANES = 8
CONV_ROW_STRIDE = 4
MXU_WIDTH = 256
MOD_ROWS = 8
VMEM_LIMIT = 56 * 1024 * 1024

F32 = jnp.float32
BF16 = jnp.bfloat16


def _dot(a, b):
    return jnp.dot(a, b, preferred_element_type=F32)


def _dot_nt(a, b):
    return lax.dot_general(a, b, (((1,), (1,)), ((), ())), preferred_element_type=F32)


def _rms(x, gain):
    return x * lax.rsqrt(jnp.mean(x * x, axis=-1, keepdims=True) + NORM_EPS) * gain


def _modulated_norm(x, gain, shift, scale):
    return (_rms(x, gain) * (1.0 + scale) + shift).astype(BF16)


def _lane_iota(shape):
    return lax.broadcasted_iota(jnp.int32, shape, len(shape) - 1)


def _resident(shape):
    nd = len(shape)
    return pl.BlockSpec(shape, lambda *_: (0,) * nd, pipeline_mode=pl.Buffered(1))


def _params(n_axes):
    return pltpu.CompilerParams(dimension_semantics=("arbitrary",) * n_axes,
                                vmem_limit_bytes=VMEM_LIMIT)


def _mod_kernel(c_ref, w_ref, b_ref, o_ref):
    c = c_ref[...]
    s = (c * jax.nn.sigmoid(c)).astype(BF16)
    o_ref[0] = _dot(s, w_ref[0].astype(BF16)) + b_ref[0]


def _modulation(cvecs, w_ada, b_ada):
    width = N_MOD * D_MODEL
    tn = 1536
    return pl.pallas_call(
        _mod_kernel,
        grid=(DEPTH, width // tn),
        in_specs=[pl.BlockSpec((MOD_ROWS, D_MODEL), lambda l, j: (0, 0)),
                  pl.BlockSpec((1, D_MODEL, tn), lambda l, j: (l, 0, j)),
                  pl.BlockSpec((1, 1, tn), lambda l, j: (l, 0, j))],
        out_specs=pl.BlockSpec((1, MOD_ROWS, tn), lambda l, j: (l, 0, j)),
        out_shape=jax.ShapeDtypeStruct((DEPTH, MOD_ROWS, width), F32),
        compiler_params=_params(2),
        name="adaln_mod",
    )(cvecs, w_ada, b_ada.reshape(DEPTH, 1, width))


def _rope(x, cos, sin_signed, first_half):
    rot = jnp.where(first_half, pltpu.roll(x, LANES - 16, 1), pltpu.roll(x, 16, 1))
    return x * cos + rot * sin_signed


def _head_mean_sq(x):
    width = min(x.shape[1], MXU_WIDTH)
    r = lax.broadcasted_iota(jnp.int32, (width, width), 0) // HEAD_DIM
    c = lax.broadcasted_iota(jnp.int32, (width, width), 1) // HEAD_DIM
    blockdiag = jnp.where(r == c, 1.0 / HEAD_DIM, 0.0).astype(BF16)
    parts = []
    for lo in range(0, x.shape[1], width):
        sq = x[:, lo:lo + width] * x[:, lo:lo + width]
        parts.append(_dot(sq.astype(BF16), blockdiag))
    return parts[0] if len(parts) == 1 else jnp.concatenate(parts, axis=1)


def _store_cache(ref, val):
    seq = ref.shape[2]
    for b in range(ref.shape[0]):
        ref[b, 0] = val[b * seq:(b + 1) * seq]


def _cast_specs(casts, n_steps, step_of):
    in_specs, args, out_specs, out_shape = [], [], [], []
    for w, w_layer, splits in casts:
        _, n_rows, n_cols = w.shape
        blk = n_rows // n_steps
        assert n_rows % n_steps == 0 and blk % 16 == 0
        in_specs.append(pl.BlockSpec((1, blk, n_cols), lambda *g, w_layer=w_layer: (w_layer, step_of(*g), 0)))
        args.append(w)
        for lo, hi in splits:
            out_specs.append(pl.BlockSpec((blk, hi - lo), lambda *g: (step_of(*g), 0)))
            out_shape.append(jax.ShapeDtypeStruct((n_rows, hi - lo), BF16))
    return in_specs, args, out_specs, out_shape, tuple(tuple(s) for _, _, s in casts)


def _run_casts(in_refs, out_refs, cast_splits):
    out_refs = iter(out_refs)
    for w_ref, splits in zip(in_refs, cast_splits):
        for lo, hi in splits:
            next(out_refs)[...] = w_ref[0, :, lo:hi].astype(BF16)


def _pre_kernel(*refs, latent, n_aliased, f32_weights, cast_splits):
    if f32_weights:
        *refs, w_bf16_ref = refs
    n_in = 8 if latent else 6
    n_out = 7 if latent else 11
    cast_in = refs[n_in + n_aliased:n_in + n_aliased + len(cast_splits)]
    outs = refs[n_in + n_aliased + len(cast_splits):]
    _run_casts(cast_in, outs[n_out:], cast_splits)
    if latent:
        x_ref, mod_ref, n1_ref, w_ref, qn_ref, kn_ref, cos_ref, sin_ref = refs[:n_in]
        qd_ref, kd_ref, vd_ref, qg_ref, kk_ref, vv_ref, u_ref = outs[:n_out]
    else:
        x_ref, mod_ref, n1_ref, w_ref, qn_ref, kn_ref = refs[:n_in]
        (qd_ref, kd_ref, vd_ref, qg_ref, kk_ref, vv_ref, u_ref,
         ndk_ref, ndv_ref, ngk_ref, ngv_ref) = outs[:n_out]

    if f32_weights:
        @pl.when(pl.program_id(0) == 0)
        def _():
            w_bf16_ref[...] = w_ref[0].astype(BF16)
        w_ref = w_bf16_ref

    mod = mod_ref[0, 0]
    h = _modulated_norm(x_ref[...], n1_ref[...], mod[:, 0:D_MODEL], mod[:, D_MODEL:2 * D_MODEL])
    rows = h.shape[0]

    lane = _lane_iota((rows, LANES))
    low_half = lane < HEAD_DIM
    if latent:
        cos, sin_signed = cos_ref[...], sin_ref[...]
        first_half = ((lane % HEAD_DIM) // 16) % 2 == 0
        rope = lambda t: _rope(t, cos, sin_signed, first_half)
    else:
        rope = lambda t: t

    qk_scale = HEAD_DIM ** -0.5 * math.log2(math.e)

    def proj(lo, width):
        return _dot(h, w_ref[:, lo:lo + width])

    def slabs(t):
        return [t[:, s:s + LANES] for s in range(0, t.shape[1], LANES)]

    def store_slabs(ref, parts):
        for s, p in enumerate(parts):
            ref[:, s * LANES:(s + 1) * LANES] = p.astype(ref.dtype)

    store_slabs(qd_ref, [rope(t) * qk_scale for t in slabs(proj(0, DIFF_WIDTH))])
    dk = proj(DIFF_WIDTH, DIFF_WIDTH)
    dv = proj(2 * DIFF_WIDTH, DIFF_WIDTH)
    if not latent:
        _store_cache(ndk_ref, dk)
        _store_cache(ndv_ref, dv)
    store_slabs(kd_ref, [rope(t) for t in slabs(dk)])
    vd_ref[...] = dv.astype(BF16)

    gq = proj(3 * DIFF_WIDTH, GQA_WIDTH)
    gq = gq * lax.rsqrt(_head_mean_sq(gq) + NORM_EPS)
    qn = qn_ref[...]
    store_slabs(qg_ref, [rope(t * qn) * qk_scale for t in slabs(gq)])

    gkv = proj(3 * DIFF_WIDTH + GQA_WIDTH, 2 * GQA_KV_WIDTH)
    k, v = gkv[:, :GQA_KV_WIDTH], gkv[:, GQA_KV_WIDTH:]
    k = k * lax.rsqrt(_head_mean_sq(k) + NORM_EPS) * kn_ref[...]
    if not latent:
        _store_cache(ngk_ref, k)
        _store_cache(ngv_ref, v)
    k = rope(k)
    k_sw, v_sw = pltpu.roll(k, HEAD_DIM, 1), pltpu.roll(v, HEAD_DIM, 1)
    store_slabs(kk_ref, [jnp.where(low_half, k, k_sw), jnp.where(low_half, k_sw, k)])
    store_slabs(vv_ref, [jnp.where(low_half, v, v_sw), jnp.where(low_half, v_sw, v)])

    cv = proj(3 * DIFF_WIDTH + GQA_WIDTH + 2 * GQA_KV_WIDTH, 2 * CONV_WIDTH)
    u_ref[...] = cv[:, :CONV_WIDTH] * jax.nn.sigmoid(cv[:, CONV_WIDTH:])


def _pre(x2d, mod, layer, norm1, w_mix, qn, kn, rope_tabs, new_cache, casts, *, latent, seq, tile):
    n_tok = x2d.shape[0]
    tiles_per_seq = seq // tile if latent else 1
    row = (lambda i: 1 + i // tiles_per_seq) if latent else (lambda i: 0)
    tok = lambda w: pl.BlockSpec((tile, w), lambda i: (i, 0))
    f32_weights = w_mix.dtype == F32
    w_spec = (pl.BlockSpec((1, D_MODEL, MIX_WIDTH), lambda i: (layer, 0, 0), pipeline_mode=pl.Buffered(1))
              if f32_weights else _resident((D_MODEL, MIX_WIDTH)))
    in_specs = [tok(D_MODEL),
                pl.BlockSpec((1, 1, 1, N_MOD * D_MODEL), lambda i: (layer, row(i), 0, 0)),
                _resident((1, D_MODEL)), w_spec,
                _resident((1, LANES)), _resident((1, LANES))]
    args = [x2d, mod, norm1, w_mix, qn, kn]
    widths = [(512, BF16), (512, BF16), (512, BF16), (512, BF16), (256, BF16), (256, BF16), (512, F32)]
    out_specs = [tok(w) for w, _ in widths]
    out_shape = [jax.ShapeDtypeStruct((n_tok, w), dt) for w, dt in widths]
    aliases = {}
    if latent:
        in_specs += [pl.BlockSpec((tile, LANES), lambda i: (i % tiles_per_seq, 0))] * 2
        args += list(rope_tabs)
    else:
        per_tile = tile // seq
        for k, w in enumerate((512, 512, GQA_KV_WIDTH, GQA_KV_WIDTH)):
            if new_cache is not None:
                aliases[len(args)] = len(out_shape)
                in_specs.append(pl.BlockSpec(memory_space=pl.ANY))
                args.append(new_cache[k])
            out_specs.append(pl.BlockSpec((per_tile, 1, seq, w), lambda i: (i, layer, 0, 0)))
            out_shape.append(jax.ShapeDtypeStruct((n_tok // seq, DEPTH, seq, w), F32))
    c_in, c_args, c_out, c_shape, cast_splits = _cast_specs(casts, n_tok // tile, lambda i: i)
    in_specs, args, out_specs, out_shape = in_specs + c_in, args + c_args, out_specs + c_out, out_shape + c_shape
    return pl.pallas_call(
        functools.partial(_pre_kernel, latent=latent, n_aliased=len(aliases), f32_weights=f32_weights,
                          cast_splits=cast_splits),
        grid=(n_tok // tile,),
        in_specs=in_specs,
        out_specs=out_specs,
        out_shape=out_shape,
        scratch_shapes=[pltpu.VMEM((D_MODEL, MIX_WIDTH), BF16)] if f32_weights else [],
        input_output_aliases=aliases,
        compiler_params=_params(1),
        name="pre_latent" if latent else "pre_ctx",
    )(*args)


def _attend(q, keys, values):
    scores = [_dot_nt(q, k) for k in keys]
    m = functools.reduce(jnp.maximum, [jnp.max(s, axis=-1, keepdims=True) for s in scores])
    return functools.reduce(jnp.add, [_dot(jnp.exp2(s - m).astype(BF16), v) for s, v in zip(scores, values)])


def _attn_kernel(*refs, latent, lam_init, cast_splits, n_seqs):
    n_core = 12 if latent else 8
    if latent:
        (qd_ref, kd_ref, vd_ref, qg_ref, kk_ref, vv_ref, cdk_ref, cdv_ref, cgk_ref, cgv_ref,
         lp_ref, sub_ref) = refs[:n_core]
    else:
        qd_ref, kd_ref, vd_ref, qg_ref, kk_ref, vv_ref, lp_ref, sub_ref = refs[:n_core]
    cast_in = refs[n_core:n_core + len(cast_splits)]
    da_ref, ga_ref = refs[n_core + len(cast_splits):n_core + len(cast_splits) + 2]
    _run_casts(cast_in, refs[n_core + len(cast_splits) + 2:], cast_splits)

    rows = qd_ref.shape[0] // n_seqs
    kv_rows = kd_ref.shape[0] // n_seqs
    low_half = _lane_iota((rows, LANES)) < HEAD_DIM
    zero = jnp.zeros((), BF16)

    lp = lp_ref[...]
    lam = (jnp.exp(jnp.sum(lp[0:1] * lp[1:2], axis=-1, keepdims=True))
           - jnp.exp(jnp.sum(lp[2:3] * lp[3:4], axis=-1, keepdims=True)) + lam_init)
    sub_gain = sub_ref[...] * (1.0 - lam_init)

    def with_ones(v):
        return jnp.concatenate([v, jnp.ones_like(v)], axis=1)

    def value_slabs(v_dup):
        low = _lane_iota(v_dup.shape) < HEAD_DIM
        one = jnp.ones((), v_dup.dtype)
        return jnp.concatenate([jnp.where(low, v_dup, one), jnp.where(low, one, v_dup)], axis=1)

    def both_halves(q):
        return jnp.concatenate([jnp.where(low_half, q, zero), jnp.where(low_half, zero, q)], axis=0)

    if latent:
        ck, cv = cgk_ref[0, 0], cgv_ref[0, 0]
        ck_sw, cv_sw = pltpu.roll(ck, HEAD_DIM, 1), pltpu.roll(cv, HEAD_DIM, 1)
        low_c = _lane_iota(ck.shape) < HEAD_DIM
        cache_k = [jnp.where(low_c, ck, ck_sw).astype(BF16), jnp.where(low_c, ck_sw, ck).astype(BF16)]
        cache_v = [value_slabs(jnp.where(low_c, cv, cv_sw).astype(BF16)),
                   value_slabs(jnp.where(low_c, cv_sw, cv).astype(BF16))]
    for s in range(n_seqs):
        qs = slice(s * rows, (s + 1) * rows)
        ks = slice(s * kv_rows, (s + 1) * kv_rows)
        for h in range(DIFF_HEADS):
            sl = slice(h * LANES, (h + 1) * LANES)
            q = qd_ref[qs, sl]
            keys, values = [kd_ref[ks, sl]], [with_ones(vd_ref[ks, sl])]
            if latent:
                keys.append(cdk_ref[0, 0, :, sl].astype(BF16))
                values.append(with_ones(cdv_ref[0, 0, :, sl].astype(BF16)))
            r = _attend(both_halves(q), keys, values)
            r1, r2 = r[:rows], r[rows:]
            o = r1[:, :LANES] / r1[:, LANES:] - lam * (r2[:, :LANES] / r2[:, LANES:])
            da_ref[qs, sl] = _rms(o, sub_gain).astype(BF16)

        for n in range(GQA_KV_HEADS):
            kv_sl = slice(n * LANES, (n + 1) * LANES)
            keys = [kk_ref[ks, kv_sl]]
            values = [value_slabs(vv_ref[ks, kv_sl])]
            if latent:
                keys.append(cache_k[n])
                values.append(cache_v[n])
            for j in range(2):
                sl = slice((2 * n + j) * LANES, (2 * n + j + 1) * LANES)
                q = qg_ref[qs, sl]
                r = _attend(both_halves(q), keys, values)
                r_even, r_odd = r[:rows, :LANES], r[rows:, LANES:]
                r = jnp.where(low_half, r_even, r_odd)
                denom = jnp.where(low_half, pltpu.roll(r_even, HEAD_DIM, 1), pltpu.roll(r_odd, HEAD_DIM, 1))
                ga_ref[qs, sl] = (r / denom).astype(BF16)


def _attention(pre_outs, caches, layer, lam_params, subln, casts, *, latent, n_seq, seq, q_tile, seqs_per_step):
    qd, kd, vd, qg, kk, vv = pre_outs
    tiles = seq // q_tile
    assert seqs_per_step == 1 or tiles == 1
    n_seq //= seqs_per_step
    n_steps = n_seq * tiles
    q_spec = lambda w: pl.BlockSpec((seqs_per_step * q_tile, w), lambda b, i: (b * tiles + i, 0))
    kv_spec = lambda w: pl.BlockSpec((seqs_per_step * seq, w), lambda b, i: (b, 0))
    in_specs = [q_spec(512), kv_spec(512), kv_spec(512), q_spec(512), kv_spec(256), kv_spec(256)]
    args = [qd, kd, vd, qg, kk, vv]
    if latent:
        for c in caches:
            in_specs.append(pl.BlockSpec((1, 1) + c.shape[2:], lambda b, i: (b, layer, 0, 0)))
            args.append(c)
    in_specs += [_resident((4, HEAD_DIM)), _resident((1, LANES))]
    args += [lam_params, subln]
    out_specs = [q_spec(512), q_spec(512)]
    out_shape = [jax.ShapeDtypeStruct(qd.shape, BF16)] * 2
    c_in, c_args, c_out, c_shape, cast_splits = _cast_specs(casts, n_steps, lambda b, i: b * tiles + i)
    in_specs, args, out_specs, out_shape = in_specs + c_in, args + c_args, out_specs + c_out, out_shape + c_shape
    lam_init = 0.8 - 0.6 * math.exp(-0.3 * layer)
    return pl.pallas_call(
        functools.partial(_attn_kernel, latent=latent, lam_init=lam_init, cast_splits=cast_splits,
                          n_seqs=seqs_per_step),
        grid=(n_seq, tiles),
        in_specs=in_specs,
        out_specs=out_specs,
        out_shape=out_shape,
        compiler_params=_params(2),
        name="attn_latent" if latent else "attn_ctx",
    )(*args)


def _conv_branch(u_ref, head, tail, w_ref, bias, gain, beta, win_ref, y_ref, seg):
    n_seg = u_ref.shape[0] // seg
    pitch = seg + 2 * CONV_HALO
    lane_slabs = [slice(s * LANES, (s + 1) * LANES) for s in range(CONV_WIDTH // LANES)]
    zeros = jnp.zeros((CONV_HALO, LANES), F32)
    for k in range(n_seg):
        for s, ls in enumerate(lane_slabs):
            win_ref[s, k * pitch:k * pitch + CONV_HALO, :] = head[:, ls] if k == 0 else zeros
            win_ref[s, k * pitch + CONV_HALO:(k + 1) * pitch - CONV_HALO, :] = u_ref[k * seg:(k + 1) * seg, ls]
            win_ref[s, (k + 1) * pitch - CONV_HALO:(k + 1) * pitch, :] = tail[:, ls] if k == n_seg - 1 else zeros

    first = CONV_HALO - CONV_KSIZE // 2
    group = 16
    rows_per_group = group * SUBLANES
    for k in range(n_seg):
        for s, ls in enumerate(lane_slabs):
            for r0 in range(0, seg, rows_per_group):
                offs = [r + t for r in range(r0, r0 + rows_per_group, SUBLANES * CONV_ROW_STRIDE)
                        for t in range(CONV_ROW_STRIDE)]
                accs = [jnp.zeros((SUBLANES, LANES), F32)] * group
                for j in range(CONV_KSIZE):
                    w_tap = jnp.broadcast_to(w_ref[j:j + 1, ls], (SUBLANES, LANES))
                    for a, off in enumerate(offs):
                        start = k * pitch + off + first + j
                        accs[a] = accs[a] + win_ref[s, pl.ds(start, SUBLANES, stride=CONV_ROW_STRIDE), :] * w_tap
                for a, off in enumerate(offs):
                    y_ref[s, pl.ds(k * seg + off, SUBLANES, stride=CONV_ROW_STRIDE), :] = accs[a]
    acc = jnp.concatenate([y_ref[s] for s in range(len(lane_slabs))], axis=1) + bias
    mu = jnp.mean(acc, axis=-1, keepdims=True)
    xc = acc - mu
    y = xc * lax.rsqrt(jnp.mean(xc * xc, axis=-1, keepdims=True) + NORM_EPS) * gain + beta
    return y * jax.nn.sigmoid(y)


def _post_kernel(*refs, last, seg, tiles_per_seq):
    halo = tiles_per_seq > 1
    if halo:
        x_ref, da_ref, ga_ref, u_ref, uprev_ref, unext_ref = refs[:6]
        refs = refs[6:]
    else:
        x_ref, da_ref, ga_ref, u_ref = refs[:4]
        refs = refs[4:]
    (mod_ref, n1_ref, n2_ref, fn_ref, cw_ref, cb_ref, cg_ref, cbeta_ref,
     wg_ref, wda_ref, wga_ref, wco_ref, wo_ref, w1_ref, w2_ref, o_ref, win_ref, y_ref) = refs

    if halo:
        pos = pl.program_id(0) % tiles_per_seq
        head = jnp.where(pos > 0, uprev_ref[...], 0.0)
        tail = jnp.where(pos < tiles_per_seq - 1, unext_ref[...], 0.0)
    else:
        head = tail = jnp.zeros((CONV_HALO, CONV_WIDTH), F32)
    ca = _conv_branch(u_ref, head, tail, cw_ref, cb_ref[...], cg_ref[...], cbeta_ref[...], win_ref, y_ref, seg)

    x = x_ref[...]
    mod = mod_ref[0, 0]
    m = lambda k: mod[:, k * D_MODEL:(k + 1) * D_MODEL]
    h = _modulated_norm(x, n1_ref[...], m(0), m(1))

    branches = ((da_ref[...], wda_ref), (ga_ref[...], wga_ref), (ca.astype(BF16), wco_ref))
    merged = None
    for j, (act, w_ref) in enumerate(branches):
        gate = jax.nn.sigmoid(_dot(h, wg_ref[:, j * D_MODEL:(j + 1) * D_MODEL]))
        term = gate * _dot(act, w_ref[...])
        merged = term if merged is None else merged + term
    x = x + m(2) * _dot(merged.astype(BF16), wo_ref[...])

    h2 = _modulated_norm(x, n2_ref[...], m(3), m(4))
    hid = 1024
    f = None
    for c in range(0, MLP_HIDDEN, hid):
        a = jnp.maximum(_dot(h2, w1_ref[:, c:c + hid]), 0.0)
        term = _dot((a * a).astype(BF16), w2_ref[c:c + hid, :])
        f = term if f is None else f + term
    x = x + m(5) * f
    o_ref[...] = _rms(x, fn_ref[...]) if last else x


def _post(x2d, da, ga, u, mod, layer, norm1, norm2, final_norm, conv_params, weights, *, latent, seq, tile, last):
    n_tok = x2d.shape[0]
    assert seq % tile == 0 or tile % seq == 0
    tiles_per_seq = max(seq // tile, 1)
    seg = min(seq, tile)
    row = (lambda i: 1 + i // tiles_per_seq) if latent else (lambda i: 0)
    tok = lambda w: pl.BlockSpec((tile, w), lambda i: (i, 0))
    in_specs = [tok(D_MODEL), tok(512), tok(512), tok(CONV_WIDTH)]
    args = [x2d, da, ga, u]
    if tiles_per_seq > 1:
        per_tile, n_halo = tile // CONV_HALO, n_tok // CONV_HALO
        in_specs += [pl.BlockSpec((CONV_HALO, CONV_WIDTH), lambda i: (jnp.maximum(i * per_tile - 1, 0), 0)),
                     pl.BlockSpec((CONV_HALO, CONV_WIDTH),
                                  lambda i: (jnp.minimum((i + 1) * per_tile, n_halo - 1), 0))]
        args += [u, u]
    in_specs += [pl.BlockSpec((1, 1, 1, N_MOD * D_MODEL), lambda i: (layer, row(i), 0, 0)),
                 _resident((1, D_MODEL)), _resident((1, D_MODEL)), _resident((1, D_MODEL))]
    in_specs += [_resident(p.shape) for p in conv_params] + [_resident(w.shape) for w in weights]
    n_slab = CONV_WIDTH // LANES
    return pl.pallas_call(
        functools.partial(_post_kernel, last=last, seg=seg, tiles_per_seq=tiles_per_seq),
        grid=(n_tok // tile,),
        in_specs=in_specs,
        out_specs=tok(D_MODEL),
        out_shape=jax.ShapeDtypeStruct((n_tok, D_MODEL), F32),
        scratch_shapes=[pltpu.VMEM((n_slab, (tile // seg) * (seg + 2 * CONV_HALO), LANES), F32),
                        pltpu.VMEM((n_slab, tile, LANES), F32)],
        compiler_params=_params(1),
        name="post_latent" if latent else "post_ctx",
    )(*args, mod, norm1, norm2, final_norm, *conv_params, *weights)


def _rope_tables(n_tokens):
    n_rows = n_tokens // GRID_W
    row = jnp.repeat(jnp.arange(n_rows), GRID_W).astype(F32)
    col = jnp.tile(jnp.arange(GRID_W), n_rows).astype(F32)
    axis_dim = HEAD_DIM // 2
    freqs = ROPE_THETA ** (-jnp.arange(0, axis_dim, 2, dtype=F32) / axis_dim)
    ang_r = row[:, None] * freqs[None, :]
    ang_c = col[:, None] * freqs[None, :]
    ang = jnp.concatenate([ang_r, ang_r, ang_c, ang_c], axis=-1)
    sign = jnp.tile(jnp.repeat(jnp.array([-1.0, 1.0], F32), HEAD_DIM // 4), 2)
    return jnp.tile(jnp.cos(ang), (1, 2)), jnp.tile(jnp.sin(ang) * sign, (1, 2))


def kernel(x_prompt, x_sample, cache_diff_k, cache_diff_v, cache_gqa_k, cache_gqa_v, c, c_ctx, w_ada, b_ada, norm1, norm2, w_in, diff_lq1, diff_lk1, diff_lq2, diff_lk2, diff_subln, w_diff_o, gqa_q_norm, gqa_k_norm, w_gqa_o, conv_dw, conv_dw_b, conv_ln_g, conv_ln_b, w_conv_o, w_o, w_mlp1, w_mlp2, final_norm):
    n_ctx, s_ctx, _ = x_prompt.shape
    n_lat, s_lat, _ = x_sample.shape
    past = cache_diff_k.shape[2]
    assert n_lat + 1 <= MOD_ROWS

    cvecs = jnp.concatenate([c_ctx[None], c, jnp.zeros((MOD_ROWS - 1 - n_lat, D_MODEL), F32)], axis=0)
    mod = _modulation(cvecs, w_ada, b_ada).reshape(DEPTH, MOD_ROWS, 1, N_MOD * D_MODEL)

    caches = (cache_diff_k.reshape(n_lat, DEPTH, past, 512), cache_diff_v.reshape(n_lat, DEPTH, past, 512),
              cache_gqa_k.reshape(n_lat, DEPTH, past, GQA_KV_WIDTH),
              cache_gqa_v.reshape(n_lat, DEPTH, past, GQA_KV_WIDTH))
    rope_tabs = _rope_tables(s_lat)
    row_vec = lambda p: p.reshape(1, -1)
    fn = row_vec(final_norm)

    groups = (dict(latent=False, n_seq=n_ctx, seq=s_ctx, tile=512, q_tile=s_ctx, seqs_per_step=1),
              dict(latent=True, n_seq=n_lat, seq=s_lat, tile=512, q_tile=256, seqs_per_step=1))
    xs = [x_prompt.reshape(n_ctx * s_ctx, D_MODEL), x_sample.reshape(n_lat * s_lat, D_MODEL)]
    new_cache = None

    mix_cols, gate_cols = (0, MIX_WIDTH), (MIX_WIDTH, MIX_WIDTH + GATE_WIDTH)
    whole = lambda w: ((0, w.shape[2]),)
    w_mix, w_gate = w_in, None

    for l in range(DEPTH):
        qn = row_vec(jnp.tile(gqa_q_norm[l], 2))
        kn = row_vec(jnp.tile(gqa_k_norm[l], 2))
        lam_params = jnp.stack([diff_lq1[l], diff_lk1[l], diff_lq2[l], diff_lk2[l]])
        n1, n2 = row_vec(norm1[l]), row_vec(norm2[l])
        conv_params = [conv_dw[l], row_vec(conv_dw_b[l]), row_vec(conv_ln_g[l]), row_vec(conv_ln_b[l])]
        next_in = [(w_in, l + 1, (mix_cols, gate_cols))] if l + 1 < DEPTH else []
        this_gate = [(w_in, l, (gate_cols,))] if w_gate is None else []
        pre_casts = ([(w_mlp1, l, whole(w_mlp1))], [(w_mlp2, l, whole(w_mlp2))])
        attn_casts = ([(w, l, whole(w)) for w in (w_diff_o, w_gqa_o, w_conv_o, w_o)] + this_gate + next_in, [])

        pre_outs = []
        for gi, g in enumerate(groups):
            outs = _pre(xs[gi], mod, l, n1, w_mix, qn, kn, rope_tabs, new_cache, pre_casts[gi],
                        latent=g["latent"], seq=g["seq"], tile=g["tile"])
            if not g["latent"]:
                new_cache = outs[7:11]
            pre_outs.append(outs)
        attn_outs = [_attention(pre_outs[gi][:6], caches, l, lam_params, row_vec(diff_subln[l]), attn_casts[gi],
                                latent=g["latent"], n_seq=g["n_seq"], seq=g["seq"], q_tile=g["q_tile"],
                                seqs_per_step=g["seqs_per_step"]) for gi, g in enumerate(groups)]
        w_da_b, w_ga_b, w_co_b, w_o_b, *ctx_converted = attn_outs[0][2:]
        if this_gate:
            w_gate = ctx_converted.pop(0)
        post_w = [w_gate, w_da_b, w_ga_b, w_co_b, w_o_b, pre_outs[0][-1], pre_outs[1][-1]]
        next_in_b = ctx_converted
        for gi, g in enumerate(groups):
            da, ga = attn_outs[gi][:2]
            xs[gi] = _post(xs[gi], da, ga, pre_outs[gi][6], mod, l, n1, n2, fn, conv_params, post_w,
                           latent=g["latent"], seq=g["seq"], tile=g["tile"], last=(l == DEPTH - 1))
        if next_in_b:
            w_mix, w_gate = next_in_b

    ndk, ndv, ngk, ngv = new_cache
    lead = (n_ctx, DEPTH, s_ctx)
    return (xs[0].reshape(n_ctx, s_ctx, D_MODEL), xs[1].reshape(n_lat, s_lat, D_MODEL),
            ndk.reshape(lead + (DIFF_HEADS, 2, HEAD_DIM)), ndv.reshape(lead + (DIFF_HEADS, 2 * HEAD_DIM)),
            ngk.reshape(lead + (GQA_KV_HEADS, HEAD_DIM)), ngv.reshape(lead + (GQA_KV_HEADS, HEAD_DIM)))
```

```python
import functools
import math

import jax
import jax.numpy as jnp
from jax import lax
from jax.experimental import pallas as pl
from jax.experimental.pallas import tpu as pltpu

D_MODEL = 1024
DEPTH = 2
GRID_W = 64
ROPE_THETA = 10000.0
NORM_EPS = 1e-6

DIFF_HEADS = 4
HEAD_DIM = 64
DIFF_WIDTH = 512
GQA_KV_HEADS = 2
GQA_WIDTH = 512
GQA_KV_WIDTH = 128
CONV_WIDTH = 512
CONV_KSIZE = 31
CONV_HALO = 16
N_BRANCH = 3
MLP_HIDDEN = 4 * D_MODEL
N_MOD = 6

MIX_WIDTH = 3 * 512 + 512 + 2 * 128 + 2 * CONV_WIDTH
GATE_WIDTH = N_BRANCH * D_MODEL

LANES = 128
SUBLANES = 8
CONV_ROW_STRIDE = 4
MXU_WIDTH = 256
MOD_ROWS = 8
VMEM_LIMIT = 56 * 1024 * 1024

F32 = jnp.float32
BF16 = jnp.bfloat16


def _dot(a, b):
    return jnp.dot(a, b, preferred_element_type=F32)


def _dot_nt(a, b):
    return lax.dot_general(a, b, (((1,), (1,)), ((), ())), preferred_element_type=F32)


def _rms(x, gain):
    return x * lax.rsqrt(jnp.mean(x * x, axis=-1, keepdims=True) + NORM_EPS) * gain


def _modulated_norm(x, gain, shift, scale):
    return (_rms(x, gain) * (1.0 + scale) + shift).astype(BF16)


def _lane_iota(shape):
    return lax.broadcasted_iota(jnp.int32, shape, len(shape) - 1)


def _resident(shape):
    nd = len(shape)
    return pl.BlockSpec(shape, lambda *_: (0,) * nd, pipeline_mode=pl.Buffered(1))


def _params(n_axes):
    return pltpu.CompilerParams(dimension_semantics=("arbitrary",) * n_axes,
                                vmem_limit_bytes=VMEM_LIMIT)


def _mod_kernel(c_ref, w_ref, b_ref, o_ref):
    c = c_ref[...]
    s = (c * jax.nn.sigmoid(c)).astype(BF16)
    o_ref[0] = _dot(s, w_ref[0].astype(BF16)) + b_ref[0]


def _modulation(cvecs, w_ada, b_ada):
    width = N_MOD * D_MODEL
    tn = 1536
    return pl.pallas_call(
        _mod_kernel,
        grid=(DEPTH, width // tn),
        in_specs=[pl.BlockSpec((MOD_ROWS, D_MODEL), lambda l, j: (0, 0)),
                  pl.BlockSpec((1, D_MODEL, tn), lambda l, j: (l, 0, j)),
                  pl.BlockSpec((1, 1, tn), lambda l, j: (l, 0, j))],
        out_specs=pl.BlockSpec((1, MOD_ROWS, tn), lambda l, j: (l, 0, j)),
        out_shape=jax.ShapeDtypeStruct((DEPTH, MOD_ROWS, width), F32),
        compiler_params=_params(2),
        name="adaln_mod",
    )(cvecs, w_ada, b_ada.reshape(DEPTH, 1, width))


def _rope(x, cos, sin_signed, first_half):
    rot = jnp.where(first_half, pltpu.roll(x, LANES - 16, 1), pltpu.roll(x, 16, 1))
    return x * cos + rot * sin_signed


def _head_mean_sq(x):
    width = min(x.shape[1], MXU_WIDTH)
    r = lax.broadcasted_iota(jnp.int32, (width, width), 0) // HEAD_DIM
    c = lax.broadcasted_iota(jnp.int32, (width, width), 1) // HEAD_DIM
    blockdiag = jnp.where(r == c, 1.0 / HEAD_DIM, 0.0).astype(BF16)
    parts = []
    for lo in range(0, x.shape[1], width):
        sq = x[:, lo:lo + width] * x[:, lo:lo + width]
        parts.append(_dot(sq.astype(BF16), blockdiag))
    return parts[0] if len(parts) == 1 else jnp.concatenate(parts, axis=1)


def _store_cache(ref, val):
    seq = ref.shape[2]
    for b in range(ref.shape[0]):
        ref[b, 0] = val[b * seq:(b + 1) * seq]


def _cast_specs(casts, n_steps, step_of):
    in_specs, args, out_specs, out_shape = [], [], [], []
    for w, w_layer, splits in casts:
        _, n_rows, n_cols = w.shape
        blk = n_rows // n_steps
        assert n_rows % n_steps == 0 and blk % 16 == 0
        in_specs.append(pl.BlockSpec((1, blk, n_cols), lambda *g, w_layer=w_layer: (w_layer, step_of(*g), 0)))
        args.append(w)
        for lo, hi in splits:
            out_specs.append(pl.BlockSpec((blk, hi - lo), lambda *g: (step_of(*g), 0)))
            out_shape.append(jax.ShapeDtypeStruct((n_rows, hi - lo), BF16))
    return in_specs, args, out_specs, out_shape, tuple(tuple(s) for _, _, s in casts)


def _run_casts(in_refs, out_refs, cast_splits):
    out_refs = iter(out_refs)
    for w_ref, splits in zip(in_refs, cast_splits):
        for lo, hi in splits:
            next(out_refs)[...] = w_ref[0, :, lo:hi].astype(BF16)


def _pre_kernel(*refs, latent, n_aliased, f32_weights, cast_splits):
    if f32_weights:
        *refs, w_bf16_ref = refs
    n_in = 8 if latent else 6
    n_out = 7 if latent else 11
    cast_in = refs[n_in + n_aliased:n_in + n_aliased + len(cast_splits)]
    outs = refs[n_in + n_aliased + len(cast_splits):]
    _run_casts(cast_in, outs[n_out:], cast_splits)
    if latent:
        x_ref, mod_ref, n1_ref, w_ref, qn_ref, kn_ref, cos_ref, sin_ref = refs[:n_in]
        qd_ref, kd_ref, vd_ref, qg_ref, kk_ref, vv_ref, u_ref = outs[:n_out]
    else:
        x_ref, mod_ref, n1_ref, w_ref, qn_ref, kn_ref = refs[:n_in]
        (qd_ref, kd_ref, vd_ref, qg_ref, kk_ref, vv_ref, u_ref,
         ndk_ref, ndv_ref, ngk_ref, ngv_ref) = outs[:n_out]

    if f32_weights:
        @pl.when(pl.program_id(0) == 0)
        def _():
            w_bf16_ref[...] = w_ref[0].astype(BF16)
        w_ref = w_bf16_ref

    mod = mod_ref[0, 0]
    h = _modulated_norm(x_ref[...], n1_ref[...], mod[:, 0:D_MODEL], mod[:, D_MODEL:2 * D_MODEL])
    rows = h.shape[0]

    lane = _lane_iota((rows, LANES))
    low_half = lane < HEAD_DIM
    if latent:
        cos, sin_signed = cos_ref[...], sin_ref[...]
        first_half = ((lane % HEAD_DIM) // 16) % 2 == 0
        rope = lambda t: _rope(t, cos, sin_signed, first_half)
    else:
        rope = lambda t: t

    qk_scale = HEAD_DIM ** -0.5 * math.log2(math.e)

    def proj(lo, width):
        return _dot(h, w_ref[:, lo:lo + width])

    def slabs(t):
        return [t[:, s:s + LANES] for s in range(0, t.shape[1], LANES)]

    def store_slabs(ref, parts):
        for s, p in enumerate(parts):
            ref[:, s * LANES:(s + 1) * LANES] = p.astype(ref.dtype)

    store_slabs(qd_ref, [rope(t) * qk_scale for t in slabs(proj(0, DIFF_WIDTH))])
    dk = proj(DIFF_WIDTH, DIFF_WIDTH)
    dv = proj(2 * DIFF_WIDTH, DIFF_WIDTH)
    if not latent:
        _store_cache(ndk_ref, dk)
        _store_cache(ndv_ref, dv)
    store_slabs(kd_ref, [rope(t) for t in slabs(dk)])
    vd_ref[...] = dv.astype(BF16)

    gq = proj(3 * DIFF_WIDTH, GQA_WIDTH)
    gq = gq * lax.rsqrt(_head_mean_sq(gq) + NORM_EPS)
    qn = qn_ref[...]
    store_slabs(qg_ref, [rope(t * qn) * qk_scale for t in slabs(gq)])

    gkv = proj(3 * DIFF_WIDTH + GQA_WIDTH, 2 * GQA_KV_WIDTH)
    k, v = gkv[:, :GQA_KV_WIDTH], gkv[:, GQA_KV_WIDTH:]
    k = k * lax.rsqrt(_head_mean_sq(k) + NORM_EPS) * kn_ref[...]
    if not latent:
        _store_cache(ngk_ref, k)
        _store_cache(ngv_ref, v)
    k = rope(k)
    k_sw, v_sw = pltpu.roll(k, HEAD_DIM, 1), pltpu.roll(v, HEAD_DIM, 1)
    store_slabs(kk_ref, [jnp.where(low_half, k, k_sw), jnp.where(low_half, k_sw, k)])
    store_slabs(vv_ref, [jnp.where(low_half, v, v_sw), jnp.where(low_half, v_sw, v)])

    cv = proj(3 * DIFF_WIDTH + GQA_WIDTH + 2 * GQA_KV_WIDTH, 2 * CONV_WIDTH)
    u_ref[...] = cv[:, :CONV_WIDTH] * jax.nn.sigmoid(cv[:, CONV_WIDTH:])


def _pre(x2d, mod, layer, norm1, w_mix, qn, kn, rope_tabs, new_cache, casts, *, latent, seq, tile):
    n_tok = x2d.shape[0]
    tiles_per_seq = seq // tile if latent else 1
    row = (lambda i: 1 + i // tiles_per_seq) if latent else (lambda i: 0)
    tok = lambda w: pl.BlockSpec((tile, w), lambda i: (i, 0))
    f32_weights = w_mix.dtype == F32
    w_spec = (pl.BlockSpec((1, D_MODEL, MIX_WIDTH), lambda i: (layer, 0, 0), pipeline_mode=pl.Buffered(1))
              if f32_weights else _resident((D_MODEL, MIX_WIDTH)))
    in_specs = [tok(D_MODEL),
                pl.BlockSpec((1, 1, 1, N_MOD * D_MODEL), lambda i: (layer, row(i), 0, 0)),
                _resident((1, D_MODEL)), w_spec,
                _resident((1, LANES)), _resident((1, LANES))]
    args = [x2d, mod, norm1, w_mix, qn, kn]
    widths = [(512, BF16), (512, BF16), (512, BF16), (512, BF16), (256, BF16), (256, BF16), (512, F32)]
    out_specs = [tok(w) for w, _ in widths]
    out_shape = [jax.ShapeDtypeStruct((n_tok, w), dt) for w, dt in widths]
    aliases = {}
    if latent:
        in_specs += [pl.BlockSpec((tile, LANES), lambda i: (i % tiles_per_seq, 0))] * 2
        args += list(rope_tabs)
    else:
        per_tile = tile // seq
        for k, w in enumerate((512, 512, GQA_KV_WIDTH, GQA_KV_WIDTH)):
            if new_cache is not None:
                aliases[len(args)] = len(out_shape)
                in_specs.append(pl.BlockSpec(memory_space=pl.ANY))
                args.append(new_cache[k])
            out_specs.append(pl.BlockSpec((per_tile, 1, seq, w), lambda i: (i, layer, 0, 0)))
            out_shape.append(jax.ShapeDtypeStruct((n_tok // seq, DEPTH, seq, w), F32))
    c_in, c_args, c_out, c_shape, cast_splits = _cast_specs(casts, n_tok // tile, lambda i: i)
    in_specs, args, out_specs, out_shape = in_specs + c_in, args + c_args, out_specs + c_out, out_shape + c_shape
    return pl.pallas_call(
        functools.partial(_pre_kernel, latent=latent, n_aliased=len(aliases), f32_weights=f32_weights,
                          cast_splits=cast_splits),
        grid=(n_tok // tile,),
        in_specs=in_specs,
        out_specs=out_specs,
        out_shape=out_shape,
        scratch_shapes=[pltpu.VMEM((D_MODEL, MIX_WIDTH), BF16)] if f32_weights else [],
        input_output_aliases=aliases,
        compiler_params=_params(1),
        name="pre_latent" if latent else "pre_ctx",
    )(*args)


def _attend(q, keys, values):
    scores = [_dot_nt(q, k) for k in keys]
    m = functools.reduce(jnp.maximum, [jnp.max(s, axis=-1, keepdims=True) for s in scores])
    return functools.reduce(jnp.add, [_dot(jnp.exp2(s - m).astype(BF16), v) for s, v in zip(scores, values)])


def _attn_kernel(*refs, latent, lam_init, cast_splits, n_seqs):
    n_core = 12 if latent else 8
    if latent:
        (qd_ref, kd_ref, vd_ref, qg_ref, kk_ref, vv_ref, cdk_ref, cdv_ref, cgk_ref, cgv_ref,
         lp_ref, sub_ref) = refs[:n_core]
    else:
        qd_ref, kd_ref, vd_ref, qg_ref, kk_ref, vv_ref, lp_ref, sub_ref = refs[:n_core]
    cast_in = refs[n_core:n_core + len(cast_splits)]
    da_ref, ga_ref = refs[n_core + len(cast_splits):n_core + len(cast_splits) + 2]
    _run_casts(cast_in, refs[n_core + len(cast_splits) + 2:], cast_splits)

    rows = qd_ref.shape[0] // n_seqs
    kv_rows = kd_ref.shape[0] // n_seqs
    low_half = _lane_iota((rows, LANES)) < HEAD_DIM
    zero = jnp.zeros((), BF16)

    lp = lp_ref[...]
    lam = (jnp.exp(jnp.sum(lp[0:1] * lp[1:2], axis=-1, keepdims=True))
           - jnp.exp(jnp.sum(lp[2:3] * lp[3:4], axis=-1, keepdims=True)) + lam_init)
    sub_gain = sub_ref[...] * (1.0 - lam_init)

    def with_ones(v):
        return jnp.concatenate([v, jnp.ones_like(v)], axis=1)

    def value_slabs(v_dup):
        low = _lane_iota(v_dup.shape) < HEAD_DIM
        one = jnp.ones((), v_dup.dtype)
        return jnp.concatenate([jnp.where(low, v_dup, one), jnp.where(low, one, v_dup)], axis=1)

    def both_halves(q):
        return jnp.concatenate([jnp.where(low_half, q, zero), jnp.where(low_half, zero, q)], axis=0)

    if latent:
        ck, cv = cgk_ref[0, 0], cgv_ref[0, 0]
        ck_sw, cv_sw = pltpu.roll(ck, HEAD_DIM, 1), pltpu.roll(cv, HEAD_DIM, 1)
        low_c = _lane_iota(ck.shape) < HEAD_DIM
        cache_k = [jnp.where(low_c, ck, ck_sw).astype(BF16), jnp.where(low_c, ck_sw, ck).astype(BF16)]
        cache_v = [value_slabs(jnp.where(low_c, cv, cv_sw).astype(BF16)),
                   value_slabs(jnp.where(low_c, cv_sw, cv).astype(BF16))]
    for s in range(n_seqs):
        qs = slice(s * rows, (s + 1) * rows)
        ks = slice(s * kv_rows, (s + 1) * kv_rows)
        for h in range(DIFF_HEADS):
            sl = slice(h * LANES, (h + 1) * LANES)
            q = qd_ref[qs, sl]
            keys, values = [kd_ref[ks, sl]], [with_ones(vd_ref[ks, sl])]
            if latent:
                keys.append(cdk_ref[0, 0, :, sl].astype(BF16))
                values.append(with_ones(cdv_ref[0, 0, :, sl].astype(BF16)))
            r = _attend(both_halves(q), keys, values)
            r1, r2 = r[:rows], r[rows:]
            o = r1[:, :LANES] / r1[:, LANES:] - lam * (r2[:, :LANES] / r2[:, LANES:])
            da_ref[qs, sl] = _rms(o, sub_gain).astype(BF16)

        for n in range(GQA_KV_HEADS):
            kv_sl = slice(n * LANES, (n + 1) * LANES)
            keys = [kk_ref[ks, kv_sl]]
            values = [value_slabs(vv_ref[ks, kv_sl])]
            if latent:
                keys.append(cache_k[n])
                values.append(cache_v[n])
            for j in range(2):
                sl = slice((2 * n + j) * LANES, (2 * n + j + 1) * LANES)
                q = qg_ref[qs, sl]
                r = _attend(both_halves(q), keys, values)
                r_even, r_odd = r[:rows, :LANES], r[rows:, LANES:]
                r = jnp.where(low_half, r_even, r_odd)
                denom = jnp.where(low_half, pltpu.roll(r_even, HEAD_DIM, 1), pltpu.roll(r_odd, HEAD_DIM, 1))
                ga_ref[qs, sl] = (r / denom).astype(BF16)


def _attention(pre_outs, caches, layer, lam_params, subln, casts, *, latent, n_seq, seq, q_tile, seqs_per_step):
    qd, kd, vd, qg, kk, vv = pre_outs
    tiles = seq // q_tile
    assert seqs_per_step == 1 or tiles == 1
    n_seq //= seqs_per_step
    n_steps = n_seq * tiles
    q_spec = lambda w: pl.BlockSpec((seqs_per_step * q_tile, w), lambda b, i: (b * tiles + i, 0))
    kv_spec = lambda w: pl.BlockSpec((seqs_per_step * seq, w), lambda b, i: (b, 0))
    in_specs = [q_spec(512), kv_spec(512), kv_spec(512), q_spec(512), kv_spec(256), kv_spec(256)]
    args = [qd, kd, vd, qg, kk, vv]
    if latent:
        for c in caches:
            in_specs.append(pl.BlockSpec((1, 1) + c.shape[2:], lambda b, i: (b, layer, 0, 0)))
            args.append(c)
    in_specs += [_resident((4, HEAD_DIM)), _resident((1, LANES))]
    args += [lam_params, subln]
    out_specs = [q_spec(512), q_spec(512)]
    out_shape = [jax.ShapeDtypeStruct(qd.shape, BF16)] * 2
    c_in, c_args, c_out, c_shape, cast_splits = _cast_specs(casts, n_steps, lambda b, i: b * tiles + i)
    in_specs, args, out_specs, out_shape = in_specs + c_in, args + c_args, out_specs + c_out, out_shape + c_shape
    lam_init = 0.8 - 0.6 * math.exp(-0.3 * layer)
    return pl.pallas_call(
        functools.partial(_attn_kernel, latent=latent, lam_init=lam_init, cast_splits=cast_splits,
                          n_seqs=seqs_per_step),
        grid=(n_seq, tiles),
        in_specs=in_specs,
        out_specs=out_specs,
        out_shape=out_shape,
        compiler_params=_params(2),
        name="attn_latent" if latent else "attn_ctx",
    )(*args)


def _conv_branch(u_ref, head, tail, w_ref, bias, gain, beta, win_ref, y_ref, seg):
    n_seg = u_ref.shape[0] // seg
    pitch = seg + 2 * CONV_HALO
    lane_slabs = [slice(s * LANES, (s + 1) * LANES) for s in range(CONV_WIDTH // LANES)]
    zeros = jnp.zeros((CONV_HALO, LANES), F32)
    for k in range(n_seg):
        for s, ls in enumerate(lane_slabs):
            win_ref[s, k * pitch:k * pitch + CONV_HALO, :] = head[:, ls] if k == 0 else zeros
            win_ref[s, k * pitch + CONV_HALO:(k + 1) * pitch - CONV_HALO, :] = u_ref[k * seg:(k + 1) * seg, ls]
            win_ref[s, (k + 1) * pitch - CONV_HALO:(k + 1) * pitch, :] = tail[:, ls] if k == n_seg - 1 else zeros

    first = CONV_HALO - CONV_KSIZE // 2
    group = 16
    rows_per_group = group * SUBLANES
    for k in range(n_seg):
        for s, ls in enumerate(lane_slabs):
            for r0 in range(0, seg, rows_per_group):
                offs = [r + t for r in range(r0, r0 + rows_per_group, SUBLANES * CONV_ROW_STRIDE)
                        for t in range(CONV_ROW_STRIDE)]
                accs = [jnp.zeros((SUBLANES, LANES), F32)] * group
                for j in range(CONV_KSIZE):
                    w_tap = jnp.broadcast_to(w_ref[j:j + 1, ls], (SUBLANES, LANES))
                    for a, off in enumerate(offs):
                        start = k * pitch + off + first + j
                        accs[a] = accs[a] + win_ref[s, pl.ds(start, SUBLANES, stride=CONV_ROW_STRIDE), :] * w_tap
                for a, off in enumerate(offs):
                    y_ref[s, pl.ds(k * seg + off, SUBLANES, stride=CONV_ROW_STRIDE), :] = accs[a]
    acc = jnp.concatenate([y_ref[s] for s in range(len(lane_slabs))], axis=1) + bias
    mu = jnp.mean(acc, axis=-1, keepdims=True)
    xc = acc - mu
    y = xc * lax.rsqrt(jnp.mean(xc * xc, axis=-1, keepdims=True) + NORM_EPS) * gain + beta
    return y * jax.nn.sigmoid(y)


def _post_kernel(*refs, last, seg, tiles_per_seq):
    halo = tiles_per_seq > 1
    if halo:
        x_ref, da_ref, ga_ref, u0_ref, tail0_ref, un_ref, headn_ref, tailn_ref = refs[:8]
        refs = refs[8:]
    else:
        x_ref, da_ref, ga_ref, u0_ref, un_ref = refs[:5]
        refs = refs[5:]
    (mod_ref, n1_ref, n2_ref, fn_ref, cw_ref, cb_ref, cg_ref, cbeta_ref,
     wg_ref, wda_ref, wga_ref, wco_ref, wo_ref, w1_ref, w2_ref, o_ref, win_ref, y_ref, ca_ref) = refs
    step, n_steps = pl.program_id(0), pl.num_programs(0)
    no_halo = jnp.zeros((CONV_HALO, CONV_WIDTH), F32)

    def conv(u_ref, head, tail):
        y = _conv_branch(u_ref, head, tail, cw_ref, cb_ref[...], cg_ref[...], cbeta_ref[...], win_ref, y_ref, seg)
        return y.astype(BF16)

    @pl.when(step == 0)
    def _():
        ca_ref[...] = conv(u0_ref, no_halo, tail0_ref[...] if halo else no_halo)

    ca = ca_ref[...]
    if halo:
        pos = jnp.minimum(step + 1, n_steps - 1) % tiles_per_seq
        head = jnp.where(pos > 0, headn_ref[...], 0.0)
        tail = jnp.where(pos < tiles_per_seq - 1, tailn_ref[...], 0.0)
    else:
        head = tail = no_halo
    ca_next = conv(un_ref, head, tail)

    x = x_ref[...]
    mod = mod_ref[0, 0]
    m = lambda k: mod[:, k * D_MODEL:(k + 1) * D_MODEL]
    h = _modulated_norm(x, n1_ref[...], m(0), m(1))

    branches = ((da_ref[...], wda_ref), (ga_ref[...], wga_ref), (ca, wco_ref))
    merged = None
    for j, (act, w_ref) in enumerate(branches):
        gate = jax.nn.sigmoid(_dot(h, wg_ref[:, j * D_MODEL:(j + 1) * D_MODEL]))
        term = gate * _dot(act, w_ref[...])
        merged = term if merged is None else merged + term
    x = x + m(2) * _dot(merged.astype(BF16), wo_ref[...])

    h2 = _modulated_norm(x, n2_ref[...], m(3), m(4))
    hid = 1024
    f = None
    for c in range(0, MLP_HIDDEN, hid):
        a = jnp.maximum(_dot(h2, w1_ref[:, c:c + hid]), 0.0)
        term = _dot((a * a).astype(BF16), w2_ref[c:c + hid, :])
        f = term if f is None else f + term
    x = x + m(5) * f
    o_ref[...] = _rms(x, fn_ref[...]) if last else x
    ca_ref[...] = ca_next


def _post(x2d, da, ga, u, mod, layer, norm1, norm2, final_norm, conv_params, weights, *, latent, seq, tile, last):
    n_tok = x2d.shape[0]
    assert seq % tile == 0 or tile % seq == 0
    tiles_per_seq = max(seq // tile, 1)
    seg = min(seq, tile)
    row = (lambda i: 1 + i // tiles_per_seq) if latent else (lambda i: 0)
    tok = lambda w: pl.BlockSpec((tile, w), lambda i: (i, 0))
    n_tiles = n_tok // tile
    nxt = lambda i: jnp.minimum(i + 1, n_tiles - 1)
    halo_blk = lambda f: pl.BlockSpec((CONV_HALO, CONV_WIDTH), lambda i: (f(i), 0))
    per_tile, n_halo = tile // CONV_HALO, n_tok // CONV_HALO
    u_first, u_next = pl.BlockSpec((tile, CONV_WIDTH), lambda i: (0, 0)), pl.BlockSpec((tile, CONV_WIDTH), lambda i: (nxt(i), 0))
    if tiles_per_seq > 1:
        conv_specs = [u_first, halo_blk(lambda i: per_tile), u_next,
                      halo_blk(lambda i: nxt(i) * per_tile - 1),
                      halo_blk(lambda i: jnp.minimum((nxt(i) + 1) * per_tile, n_halo - 1))]
    else:
        conv_specs = [u_first, u_next]
    in_specs = [tok(D_MODEL), tok(512), tok(512)] + conv_specs
    args = [x2d, da, ga] + [u] * len(conv_specs)
    in_specs += [pl.BlockSpec((1, 1, 1, N_MOD * D_MODEL), lambda i: (layer, row(i), 0, 0)),
                 _resident((1, D_MODEL)), _resident((1, D_MODEL)), _resident((1, D_MODEL))]
    in_specs += [_resident(p.shape) for p in conv_params] + [_resident(w.shape) for w in weights]
    n_slab = CONV_WIDTH // LANES
    return pl.pallas_call(
        functools.partial(_post_kernel, last=last, seg=seg, tiles_per_seq=tiles_per_seq),
        grid=(n_tok // tile,),
        in_specs=in_specs,
        out_specs=tok(D_MODEL),
        out_shape=jax.ShapeDtypeStruct((n_tok, D_MODEL), F32),
        scratch_shapes=[pltpu.VMEM((n_slab, (tile // seg) * (seg + 2 * CONV_HALO), LANES), F32),
                        pltpu.VMEM((n_slab, tile, LANES), F32), pltpu.VMEM((tile, CONV_WIDTH), BF16)],
        compiler_params=_params(1),
        name="post_latent" if latent else "post_ctx",
    )(*args, mod, norm1, norm2, final_norm, *conv_params, *weights)


def _rope_tables(n_tokens):
    n_rows = n_tokens // GRID_W
    row = jnp.repeat(jnp.arange(n_rows), GRID_W).astype(F32)
    col = jnp.tile(jnp.arange(GRID_W), n_rows).astype(F32)
    axis_dim = HEAD_DIM // 2
    freqs = ROPE_THETA ** (-jnp.arange(0, axis_dim, 2, dtype=F32) / axis_dim)
    ang_r = row[:, None] * freqs[None, :]
    ang_c = col[:, None] * freqs[None, :]
    ang = jnp.concatenate([ang_r, ang_r, ang_c, ang_c], axis=-1)
    sign = jnp.tile(jnp.repeat(jnp.array([-1.0, 1.0], F32), HEAD_DIM // 4), 2)
    return jnp.tile(jnp.cos(ang), (1, 2)), jnp.tile(jnp.sin(ang) * sign, (1, 2))


def kernel(x_prompt, x_sample, cache_diff_k, cache_diff_v, cache_gqa_k, cache_gqa_v, c, c_ctx, w_ada, b_ada, norm1, norm2, w_in, diff_lq1, diff_lk1, diff_lq2, diff_lk2, diff_subln, w_diff_o, gqa_q_norm, gqa_k_norm, w_gqa_o, conv_dw, conv_dw_b, conv_ln_g, conv_ln_b, w_conv_o, w_o, w_mlp1, w_mlp2, final_norm):
    n_ctx, s_ctx, _ = x_prompt.shape
    n_lat, s_lat, _ = x_sample.shape
    past = cache_diff_k.shape[2]
    assert n_lat + 1 <= MOD_ROWS

    cvecs = jnp.concatenate([c_ctx[None], c, jnp.zeros((MOD_ROWS - 1 - n_lat, D_MODEL), F32)], axis=0)
    mod = _modulation(cvecs, w_ada, b_ada).reshape(DEPTH, MOD_ROWS, 1, N_MOD * D_MODEL)

    caches = (cache_diff_k.reshape(n_lat, DEPTH, past, 512), cache_diff_v.reshape(n_lat, DEPTH, past, 512),
              cache_gqa_k.reshape(n_lat, DEPTH, past, GQA_KV_WIDTH),
              cache_gqa_v.reshape(n_lat, DEPTH, past, GQA_KV_WIDTH))
    rope_tabs = _rope_tables(s_lat)
    row_vec = lambda p: p.reshape(1, -1)
    fn = row_vec(final_norm)

    groups = (dict(latent=False, n_seq=n_ctx, seq=s_ctx, tile=512, q_tile=s_ctx, seqs_per_step=1),
              dict(latent=True, n_seq=n_lat, seq=s_lat, tile=512, q_tile=512, seqs_per_step=1))
    xs = [x_prompt.reshape(n_ctx * s_ctx, D_MODEL), x_sample.reshape(n_lat * s_lat, D_MODEL)]
    new_cache = None

    mix_cols, gate_cols = (0, MIX_WIDTH), (MIX_WIDTH, MIX_WIDTH + GATE_WIDTH)
    whole = lambda w: ((0, w.shape[2]),)
    w_mix, w_gate = w_in, None

    for l in range(DEPTH):
        qn = row_vec(jnp.tile(gqa_q_norm[l], 2))
        kn = row_vec(jnp.tile(gqa_k_norm[l], 2))
        lam_params = jnp.stack([diff_lq1[l], diff_lk1[l], diff_lq2[l], diff_lk2[l]])
        n1, n2 = row_vec(norm1[l]), row_vec(norm2[l])
        conv_params = [conv_dw[l], row_vec(conv_dw_b[l]), row_vec(conv_ln_g[l]), row_vec(conv_ln_b[l])]
        next_in = [(w_in, l + 1, (mix_cols, gate_cols))] if l + 1 < DEPTH else []
        this_gate = [(w_in, l, (gate_cols,))] if w_gate is None else []
        pre_casts = ([(w_mlp1, l, whole(w_mlp1))], [(w_mlp2, l, whole(w_mlp2))])
        attn_casts = ([(w, l, whole(w)) for w in (w_diff_o, w_gqa_o, w_conv_o, w_o)] + this_gate + next_in, [])

        pre_outs = []
        for gi, g in enumerate(groups):
            outs = _pre(xs[gi], mod, l, n1, w_mix, qn, kn, rope_tabs, new_cache, pre_casts[gi],
                        latent=g["latent"], seq=g["seq"], tile=g["tile"])
            if not g["latent"]:
                new_cache = outs[7:11]
            pre_outs.append(outs)
        attn_outs = [_attention(pre_outs[gi][:6], caches, l, lam_params, row_vec(diff_subln[l]), attn_casts[gi],
                                latent=g["latent"], n_seq=g["n_seq"], seq=g["seq"], q_tile=g["q_tile"],
                                seqs_per_step=g["seqs_per_step"]) for gi, g in enumerate(groups)]
        w_da_b, w_ga_b, w_co_b, w_o_b, *ctx_converted = attn_outs[0][2:]
        if this_gate:
            w_gate = ctx_converted.pop(0)
        post_w = [w_gate, w_da_b, w_ga_b, w_co_b, w_o_b, pre_outs[0][-1], pre_outs[1][-1]]
        next_in_b = ctx_converted
        for gi, g in enumerate(groups):
            da, ga = attn_outs[gi][:2]
            xs[gi] = _post(xs[gi], da, ga, pre_outs[gi][6], mod, l, n1, n2, fn, conv_params, post_w,
                           latent=g["latent"], seq=g["seq"], tile=g["tile"], last=(l == DEPTH - 1))
        if next_in_b:
            w_mix, w_gate = next_in_b

    ndk, ndv, ngk, ngv = new_cache
    lead = (n_ctx, DEPTH, s_ctx)
    return (xs[0].reshape(n_ctx, s_ctx, D_MODEL), xs[1].reshape(n_lat, s_lat, D_MODEL),
            ndk.reshape(lead + (DIFF_HEADS, 2, HEAD_DIM)), ndv.reshape(lead + (DIFF_HEADS, 2 * HEAD_DIM)),
            ngk.reshape(lead + (GQA_KV_HEADS, HEAD_DIM)), ngv.reshape(lead + (GQA_KV_HEADS, HEAD_DIM)))
```

```python
import functools
import math

import jax
import jax.numpy as jnp
from jax import lax
from jax.experimental import pallas as pl
from jax.experimental.pallas import tpu as pltpu

D_MODEL = 1024
DEPTH = 2
GRID_W = 64
ROPE_THETA = 10000.0
NORM_EPS = 1e-6

DIFF_HEADS = 4
HEAD_DIM = 64
DIFF_WIDTH = 512
GQA_KV_HEADS = 2
GQA_WIDTH = 512
GQA_KV_WIDTH = 128
CONV_WIDTH = 512
CONV_KSIZE = 31
CONV_HALO = 16
N_BRANCH = 3
MLP_HIDDEN = 4 * D_MODEL
N_MOD = 6

MIX_WIDTH = 3 * 512 + 512 + 2 * 128 + 2 * CONV_WIDTH
GATE_WIDTH = N_BRANCH * D_MODEL

LANES = 128
SUBLANES = 8
CONV_ROW_STRIDE = 4
MXU_WIDTH = 256
MOD_ROWS = 8
VMEM_LIMIT = 56 * 1024 * 1024

F32 = jnp.float32
BF16 = jnp.bfloat16


def _dot(a, b):
    return jnp.dot(a, b, preferred_element_type=F32)


def _dot_nt(a, b):
    return lax.dot_general(a, b, (((1,), (1,)), ((), ())), preferred_element_type=F32)


def _rms(x, gain):
    return x * lax.rsqrt(jnp.mean(x * x, axis=-1, keepdims=True) + NORM_EPS) * gain


def _modulated_norm(x, gain, shift, scale):
    return (_rms(x, gain) * (1.0 + scale) + shift).astype(BF16)


def _lane_iota(shape):
    return lax.broadcasted_iota(jnp.int32, shape, len(shape) - 1)


def _resident(shape):
    nd = len(shape)
    return pl.BlockSpec(shape, lambda *_: (0,) * nd, pipeline_mode=pl.Buffered(1))


def _params(n_axes):
    return pltpu.CompilerParams(dimension_semantics=("arbitrary",) * n_axes,
                                vmem_limit_bytes=VMEM_LIMIT)


def _mod_kernel(c_ref, w_ref, b_ref, o_ref):
    c = c_ref[...]
    s = (c * jax.nn.sigmoid(c)).astype(BF16)
    o_ref[0] = _dot(s, w_ref[0].astype(BF16)) + b_ref[0]


def _modulation(cvecs, w_ada, b_ada):
    width = N_MOD * D_MODEL
    tn = 1536
    return pl.pallas_call(
        _mod_kernel,
        grid=(DEPTH, width // tn),
        in_specs=[pl.BlockSpec((MOD_ROWS, D_MODEL), lambda l, j: (0, 0)),
                  pl.BlockSpec((1, D_MODEL, tn), lambda l, j: (l, 0, j)),
                  pl.BlockSpec((1, 1, tn), lambda l, j: (l, 0, j))],
        out_specs=pl.BlockSpec((1, MOD_ROWS, tn), lambda l, j: (l, 0, j)),
        out_shape=jax.ShapeDtypeStruct((DEPTH, MOD_ROWS, width), F32),
        compiler_params=_params(2),
        name="adaln_mod",
    )(cvecs, w_ada, b_ada.reshape(DEPTH, 1, width))


def _rope(x, cos, sin_signed, first_half):
    rot = jnp.where(first_half, pltpu.roll(x, LANES - 16, 1), pltpu.roll(x, 16, 1))
    return x * cos + rot * sin_signed


def _head_mean_sq(x):
    width = min(x.shape[1], MXU_WIDTH)
    r = lax.broadcasted_iota(jnp.int32, (width, width), 0) // HEAD_DIM
    c = lax.broadcasted_iota(jnp.int32, (width, width), 1) // HEAD_DIM
    blockdiag = jnp.where(r == c, 1.0 / HEAD_DIM, 0.0).astype(BF16)
    parts = []
    for lo in range(0, x.shape[1], width):
        sq = x[:, lo:lo + width] * x[:, lo:lo + width]
        parts.append(_dot(sq.astype(BF16), blockdiag))
    return parts[0] if len(parts) == 1 else jnp.concatenate(parts, axis=1)


def _store_cache(ref, val):
    seq = ref.shape[2]
    for b in range(ref.shape[0]):
        ref[b, 0] = val[b * seq:(b + 1) * seq]


def _cast_specs(casts, n_steps, step_of):
    in_specs, args, out_specs, out_shape = [], [], [], []
    for w, w_layer, splits in casts:
        _, n_rows, n_cols = w.shape
        blk = n_rows // n_steps
        assert n_rows % n_steps == 0 and blk % 16 == 0
        in_specs.append(pl.BlockSpec((1, blk, n_cols), lambda *g, w_layer=w_layer: (w_layer, step_of(*g), 0)))
        args.append(w)
        for lo, hi in splits:
            out_specs.append(pl.BlockSpec((blk, hi - lo), lambda *g: (step_of(*g), 0)))
            out_shape.append(jax.ShapeDtypeStruct((n_rows, hi - lo), BF16))
    return in_specs, args, out_specs, out_shape, tuple(tuple(s) for _, _, s in casts)


def _run_casts(in_refs, out_refs, cast_splits):
    out_refs = iter(out_refs)
    for w_ref, splits in zip(in_refs, cast_splits):
        for lo, hi in splits:
            next(out_refs)[...] = w_ref[0, :, lo:hi].astype(BF16)


def _pre_kernel(*refs, latent, n_aliased, f32_weights, cast_splits):
    if f32_weights:
        *refs, w_bf16_ref = refs
    n_in = 8 if latent else 6
    n_out = 7 if latent else 11
    cast_in = refs[n_in + n_aliased:n_in + n_aliased + len(cast_splits)]
    outs = refs[n_in + n_aliased + len(cast_splits):]
    _run_casts(cast_in, outs[n_out:], cast_splits)
    if latent:
        x_ref, mod_ref, n1_ref, w_ref, qn_ref, kn_ref, cos_ref, sin_ref = refs[:n_in]
        qd_ref, kd_ref, vd_ref, qg_ref, kk_ref, vv_ref, u_ref = outs[:n_out]
    else:
        x_ref, mod_ref, n1_ref, w_ref, qn_ref, kn_ref = refs[:n_in]
        (qd_ref, kd_ref, vd_ref, qg_ref, kk_ref, vv_ref, u_ref,
         ndk_ref, ndv_ref, ngk_ref, ngv_ref) = outs[:n_out]

    if f32_weights:
        @pl.when(pl.program_id(0) == 0)
        def _():
            w_bf16_ref[...] = w_ref[0].astype(BF16)
        w_ref = w_bf16_ref

    mod = mod_ref[0, 0]
    h = _modulated_norm(x_ref[...], n1_ref[...], mod[:, 0:D_MODEL], mod[:, D_MODEL:2 * D_MODEL])
    rows = h.shape[0]

    lane = _lane_iota((rows, LANES))
    low_half = lane < HEAD_DIM
    if latent:
        cos, sin_signed = cos_ref[...], sin_ref[...]
        first_half = ((lane % HEAD_DIM) // 16) % 2 == 0
        rope = lambda t: _rope(t, cos, sin_signed, first_half)
    else:
        rope = lambda t: t

    qk_scale = HEAD_DIM ** -0.5 * math.log2(math.e)

    def proj(lo, width):
        return _dot(h, w_ref[:, lo:lo + width])

    def slabs(t):
        return [t[:, s:s + LANES] for s in range(0, t.shape[1], LANES)]

    def store_slabs(ref, parts):
        for s, p in enumerate(parts):
            ref[:, s * LANES:(s + 1) * LANES] = p.astype(ref.dtype)

    store_slabs(qd_ref, [rope(t) * qk_scale for t in slabs(proj(0, DIFF_WIDTH))])
    dk = proj(DIFF_WIDTH, DIFF_WIDTH)
    dv = proj(2 * DIFF_WIDTH, DIFF_WIDTH)
    if not latent:
        _store_cache(ndk_ref, dk)
        _store_cache(ndv_ref, dv)
    store_slabs(kd_ref, [rope(t) for t in slabs(dk)])
    vd_ref[...] = dv.astype(BF16)

    gq = proj(3 * DIFF_WIDTH, GQA_WIDTH)
    gq = gq * lax.rsqrt(_head_mean_sq(gq) + NORM_EPS)
    qn = qn_ref[...]
    store_slabs(qg_ref, [rope(t * qn) * qk_scale for t in slabs(gq)])

    gkv = proj(3 * DIFF_WIDTH + GQA_WIDTH, 2 * GQA_KV_WIDTH)
    k, v = gkv[:, :GQA_KV_WIDTH], gkv[:, GQA_KV_WIDTH:]
    k = k * lax.rsqrt(_head_mean_sq(k) + NORM_EPS) * kn_ref[...]
    if not latent:
        _store_cache(ngk_ref, k)
        _store_cache(ngv_ref, v)
    k = rope(k)
    k_sw, v_sw = pltpu.roll(k, HEAD_DIM, 1), pltpu.roll(v, HEAD_DIM, 1)
    store_slabs(kk_ref, [jnp.where(low_half, k, k_sw), jnp.where(low_half, k_sw, k)])
    store_slabs(vv_ref, [jnp.where(low_half, v, v_sw), jnp.where(low_half, v_sw, v)])

    cv = proj(3 * DIFF_WIDTH + GQA_WIDTH + 2 * GQA_KV_WIDTH, 2 * CONV_WIDTH)
    u_ref[...] = cv[:, :CONV_WIDTH] * jax.nn.sigmoid(cv[:, CONV_WIDTH:])


def _pre(x2d, mod, layer, norm1, w_mix, qn, kn, rope_tabs, new_cache, casts, *, latent, seq, tile):
    n_tok = x2d.shape[0]
    tiles_per_seq = seq // tile if latent else 1
    row = (lambda i: 1 + i // tiles_per_seq) if latent else (lambda i: 0)
    tok = lambda w: pl.BlockSpec((tile, w), lambda i: (i, 0))
    f32_weights = w_mix.dtype == F32
    w_spec = (pl.BlockSpec((1, D_MODEL, MIX_WIDTH), lambda i: (layer, 0, 0), pipeline_mode=pl.Buffered(1))
              if f32_weights else _resident((D_MODEL, MIX_WIDTH)))
    in_specs = [tok(D_MODEL),
                pl.BlockSpec((1, 1, 1, N_MOD * D_MODEL), lambda i: (layer, row(i), 0, 0)),
                _resident((1, D_MODEL)), w_spec,
                _resident((1, LANES)), _resident((1, LANES))]
    args = [x2d, mod, norm1, w_mix, qn, kn]
    widths = [(512, BF16), (512, BF16), (512, BF16), (512, BF16), (256, BF16), (256, BF16), (512, F32)]
    out_specs = [tok(w) for w, _ in widths]
    out_shape = [jax.ShapeDtypeStruct((n_tok, w), dt) for w, dt in widths]
    aliases = {}
    if latent:
        in_specs += [pl.BlockSpec((tile, LANES), lambda i: (i % tiles_per_seq, 0))] * 2
        args += list(rope_tabs)
    else:
        per_tile = tile // seq
        for k, w in enumerate((512, 512, GQA_KV_WIDTH, GQA_KV_WIDTH)):
            if new_cache is not None:
                aliases[len(args)] = len(out_shape)
                in_specs.append(pl.BlockSpec(memory_space=pl.ANY))
                args.append(new_cache[k])
            out_specs.append(pl.BlockSpec((per_tile, 1, seq, w), lambda i: (i, layer, 0, 0)))
            out_shape.append(jax.ShapeDtypeStruct((n_tok // seq, DEPTH, seq, w), F32))
    c_in, c_args, c_out, c_shape, cast_splits = _cast_specs(casts, n_tok // tile, lambda i: i)
    in_specs, args, out_specs, out_shape = in_specs + c_in, args + c_args, out_specs + c_out, out_shape + c_shape
    return pl.pallas_call(
        functools.partial(_pre_kernel, latent=latent, n_aliased=len(aliases), f32_weights=f32_weights,
                          cast_splits=cast_splits),
        grid=(n_tok // tile,),
        in_specs=in_specs,
        out_specs=out_specs,
        out_shape=out_shape,
        scratch_shapes=[pltpu.VMEM((D_MODEL, MIX_WIDTH), BF16)] if f32_weights else [],
        input_output_aliases=aliases,
        compiler_params=_params(1),
        name="pre_latent" if latent else "pre_ctx",
    )(*args)


def _attend(q, keys, values):
    scores = [_dot_nt(q, k) for k in keys]
    m = functools.reduce(jnp.maximum, [jnp.max(s, axis=-1, keepdims=True) for s in scores])
    return functools.reduce(jnp.add, [_dot(jnp.exp2(s - m).astype(BF16), v) for s, v in zip(scores, values)])


def _attn_kernel(*refs, latent, lam_init, cast_splits, n_seqs, tiles_per_seq):
    *refs, win_ref, y_ref = refs
    n_core = 12 if latent else 8
    if latent:
        (qd_ref, kd_ref, vd_ref, qg_ref, kk_ref, vv_ref, cdk_ref, cdv_ref, cgk_ref, cgv_ref,
         lp_ref, sub_ref) = refs[:n_core]
    else:
        qd_ref, kd_ref, vd_ref, qg_ref, kk_ref, vv_ref, lp_ref, sub_ref = refs[:n_core]
    halo = tiles_per_seq > 1
    n_conv = 7 if halo else 5
    if halo:
        u_ref, uprev_ref, unext_ref, cw_ref, cb_ref, cg_ref, cbeta_ref = refs[n_core:n_core + n_conv]
        pos = pl.program_id(1)
        head = jnp.where(pos > 0, uprev_ref[...], 0.0)
        tail = jnp.where(pos < tiles_per_seq - 1, unext_ref[...], 0.0)
    else:
        u_ref, cw_ref, cb_ref, cg_ref, cbeta_ref = refs[n_core:n_core + n_conv]
        head = tail = jnp.zeros((CONV_HALO, CONV_WIDTH), F32)
    n_in = n_core + n_conv + len(cast_splits)
    cast_in = refs[n_core + n_conv:n_in]
    da_ref, ga_ref, ca_ref = refs[n_in:n_in + 3]
    _run_casts(cast_in, refs[n_in + 3:], cast_splits)
    seg = u_ref.shape[0] // n_seqs
    ca_ref[...] = _conv_branch(u_ref, head, tail, cw_ref, cb_ref[...], cg_ref[...], cbeta_ref[...],
                               win_ref, y_ref, seg).astype(BF16)

    rows = qd_ref.shape[0] // n_seqs
    kv_rows = kd_ref.shape[0] // n_seqs
    low_half = _lane_iota((rows, LANES)) < HEAD_DIM
    zero = jnp.zeros((), BF16)

    lp = lp_ref[...]
    lam = (jnp.exp(jnp.sum(lp[0:1] * lp[1:2], axis=-1, keepdims=True))
           - jnp.exp(jnp.sum(lp[2:3] * lp[3:4], axis=-1, keepdims=True)) + lam_init)
    sub_gain = sub_ref[...] * (1.0 - lam_init)

    def with_ones(v):
        return jnp.concatenate([v, jnp.ones_like(v)], axis=1)

    def value_slabs(v_dup):
        low = _lane_iota(v_dup.shape) < HEAD_DIM
        one = jnp.ones((), v_dup.dtype)
        return jnp.where(low, v_dup, one), jnp.where(low, one, v_dup)

    if latent:
        ck, cv = cgk_ref[0, 0], cgv_ref[0, 0]
        ck_sw, cv_sw = pltpu.roll(ck, HEAD_DIM, 1), pltpu.roll(cv, HEAD_DIM, 1)
        low_c = _lane_iota(ck.shape) < HEAD_DIM
        cache_k = [jnp.where(low_c, ck, ck_sw).astype(BF16), jnp.where(low_c, ck_sw, ck).astype(BF16)]
        cache_v = [value_slabs(jnp.where(low_c, cv, cv_sw).astype(BF16)),
                   value_slabs(jnp.where(low_c, cv_sw, cv).astype(BF16))]
    for s in range(n_seqs):
        qs = slice(s * rows, (s + 1) * rows)
        ks = slice(s * kv_rows, (s + 1) * kv_rows)
        for h in range(DIFF_HEADS):
            sl = slice(h * LANES, (h + 1) * LANES)
            q = qd_ref[qs, sl]
            keys, values = [kd_ref[ks, sl]], [with_ones(vd_ref[ks, sl])]
            if latent:
                keys.append(cdk_ref[0, 0, :, sl].astype(BF16))
                values.append(with_ones(cdv_ref[0, 0, :, sl].astype(BF16)))
            r1 = _attend(jnp.where(low_half, q, zero), keys, values)
            r2 = _attend(jnp.where(low_half, zero, q), keys, values)
            o = r1[:, :LANES] / r1[:, LANES:] - lam * (r2[:, :LANES] / r2[:, LANES:])
            da_ref[qs, sl] = _rms(o, sub_gain).astype(BF16)

        for n in range(GQA_KV_HEADS):
            kv_sl = slice(n * LANES, (n + 1) * LANES)
            keys = [kk_ref[ks, kv_sl]]
            values = [value_slabs(vv_ref[ks, kv_sl])]
            if latent:
                keys.append(cache_k[n])
                values.append(cache_v[n])
            for j in range(2):
                sl = slice((2 * n + j) * LANES, (2 * n + j + 1) * LANES)
                q = qg_ref[qs, sl]
                r_even = _attend(jnp.where(low_half, q, zero), keys, [v[0] for v in values])
                r_odd = _attend(jnp.where(low_half, zero, q), keys, [v[1] for v in values])
                r = jnp.where(low_half, r_even, r_odd)
                denom = jnp.where(low_half, pltpu.roll(r_even, HEAD_DIM, 1), pltpu.roll(r_odd, HEAD_DIM, 1))
                ga_ref[qs, sl] = (r / denom).astype(BF16)


def _attention(pre_outs, caches, layer, lam_params, subln, conv_params, casts, *, latent, n_seq, seq, q_tile,
               seqs_per_step):
    qd, kd, vd, qg, kk, vv, u = pre_outs
    tiles = seq // q_tile
    assert seqs_per_step == 1 or tiles == 1
    n_seq //= seqs_per_step
    n_steps = n_seq * tiles
    q_spec = lambda w: pl.BlockSpec((seqs_per_step * q_tile, w), lambda b, i: (b * tiles + i, 0))
    kv_spec = lambda w: pl.BlockSpec((seqs_per_step * seq, w), lambda b, i: (b, 0))
    in_specs = [q_spec(512), kv_spec(512), kv_spec(512), q_spec(512), kv_spec(256), kv_spec(256)]
    args = [qd, kd, vd, qg, kk, vv]
    if latent:
        for c in caches:
            in_specs.append(pl.BlockSpec((1, 1) + c.shape[2:], lambda b, i: (b, layer, 0, 0)))
            args.append(c)
    in_specs += [_resident((4, HEAD_DIM)), _resident((1, LANES))]
    args += [lam_params, subln]
    in_specs.append(q_spec(CONV_WIDTH))
    args.append(u)
    if tiles > 1:
        per_tile, n_halo = q_tile // CONV_HALO, u.shape[0] // CONV_HALO
        in_specs += [pl.BlockSpec((CONV_HALO, CONV_WIDTH),
                                  lambda b, i: (jnp.maximum((b * tiles + i) * per_tile - 1, 0), 0)),
                     pl.BlockSpec((CONV_HALO, CONV_WIDTH),
                                  lambda b, i: (jnp.minimum((b * tiles + i + 1) * per_tile, n_halo - 1), 0))]
        args += [u, u]
    in_specs += [_resident(p.shape) for p in conv_params]
    args += list(conv_params)
    out_specs = [q_spec(512), q_spec(512), q_spec(CONV_WIDTH)]
    out_shape = [jax.ShapeDtypeStruct(qd.shape, BF16)] * 3
    c_in, c_args, c_out, c_shape, cast_splits = _cast_specs(casts, n_steps, lambda b, i: b * tiles + i)
    in_specs, args, out_specs, out_shape = in_specs + c_in, args + c_args, out_specs + c_out, out_shape + c_shape
    lam_init = 0.8 - 0.6 * math.exp(-0.3 * layer)
    rows, n_slab = seqs_per_step * q_tile, CONV_WIDTH // LANES
    return pl.pallas_call(
        functools.partial(_attn_kernel, latent=latent, lam_init=lam_init, cast_splits=cast_splits,
                          n_seqs=seqs_per_step, tiles_per_seq=tiles),
        grid=(n_seq, tiles),
        in_specs=in_specs,
        out_specs=out_specs,
        out_shape=out_shape,
        scratch_shapes=[pltpu.VMEM((n_slab, rows + seqs_per_step * 2 * CONV_HALO, LANES), F32),
                        pltpu.VMEM((n_slab, rows, LANES), F32)],
        compiler_params=_params(2),
        name="attn_latent" if latent else "attn_ctx",
    )(*args)


def _conv_branch(u_ref, head, tail, w_ref, bias, gain, beta, win_ref, y_ref, seg):
    n_seg = u_ref.shape[0] // seg
    pitch = seg + 2 * CONV_HALO
    lane_slabs = [slice(s * LANES, (s + 1) * LANES) for s in range(CONV_WIDTH // LANES)]
    zeros = jnp.zeros((CONV_HALO, LANES), F32)
    for k in range(n_seg):
        for s, ls in enumerate(lane_slabs):
            win_ref[s, k * pitch:k * pitch + CONV_HALO, :] = head[:, ls] if k == 0 else zeros
            win_ref[s, k * pitch + CONV_HALO:(k + 1) * pitch - CONV_HALO, :] = u_ref[k * seg:(k + 1) * seg, ls]
            win_ref[s, (k + 1) * pitch - CONV_HALO:(k + 1) * pitch, :] = tail[:, ls] if k == n_seg - 1 else zeros

    first = CONV_HALO - CONV_KSIZE // 2
    group = 16
    rows_per_group = group * SUBLANES
    for k in range(n_seg):
        for s, ls in enumerate(lane_slabs):
            for r0 in range(0, seg, rows_per_group):
                offs = [r + t for r in range(r0, r0 + rows_per_group, SUBLANES * CONV_ROW_STRIDE)
                        for t in range(CONV_ROW_STRIDE)]
                accs = [jnp.zeros((SUBLANES, LANES), F32)] * group
                for j in range(CONV_KSIZE):
                    w_tap = jnp.broadcast_to(w_ref[j:j + 1, ls], (SUBLANES, LANES))
                    for a, off in enumerate(offs):
                        start = k * pitch + off + first + j
                        accs[a] = accs[a] + win_ref[s, pl.ds(start, SUBLANES, stride=CONV_ROW_STRIDE), :] * w_tap
                for a, off in enumerate(offs):
                    y_ref[s, pl.ds(k * seg + off, SUBLANES, stride=CONV_ROW_STRIDE), :] = accs[a]
    acc = jnp.concatenate([y_ref[s] for s in range(len(lane_slabs))], axis=1) + bias
    mu = jnp.mean(acc, axis=-1, keepdims=True)
    xc = acc - mu
    y = xc * lax.rsqrt(jnp.mean(xc * xc, axis=-1, keepdims=True) + NORM_EPS) * gain + beta
    return y * jax.nn.sigmoid(y)


def _post_kernel(x_ref, da_ref, ga_ref, ca_ref, mod_ref, n1_ref, n2_ref, fn_ref,
                 wg_ref, wda_ref, wga_ref, wco_ref, wo_ref, w1_ref, w2_ref, o_ref, *, last):
    x = x_ref[...]
    mod = mod_ref[0, 0]
    m = lambda k: mod[:, k * D_MODEL:(k + 1) * D_MODEL]
    h = _modulated_norm(x, n1_ref[...], m(0), m(1))

    branches = ((da_ref, wda_ref), (ga_ref, wga_ref), (ca_ref, wco_ref))
    merged = None
    for j, (act_ref, w_ref) in enumerate(branches):
        gate = jax.nn.sigmoid(_dot(h, wg_ref[:, j * D_MODEL:(j + 1) * D_MODEL]))
        term = gate * _dot(act_ref[...], w_ref[...])
        merged = term if merged is None else merged + term
    x = x + m(2) * _dot(merged.astype(BF16), wo_ref[...])

    h2 = _modulated_norm(x, n2_ref[...], m(3), m(4))
    hid = 1024
    f = None
    for c in range(0, MLP_HIDDEN, hid):
        a = jnp.maximum(_dot(h2, w1_ref[:, c:c + hid]), 0.0)
        term = _dot((a * a).astype(BF16), w2_ref[c:c + hid, :])
        f = term if f is None else f + term
    x = x + m(5) * f
    o_ref[...] = _rms(x, fn_ref[...]) if last else x


def _post(x2d, da, ga, ca, mod, layer, norm1, norm2, final_norm, weights, *, latent, seq, tile, last):
    n_tok = x2d.shape[0]
    tiles_per_seq = seq // tile if latent else 1
    row = (lambda i: 1 + i // tiles_per_seq) if latent else (lambda i: 0)
    tok = lambda w: pl.BlockSpec((tile, w), lambda i: (i, 0))
    in_specs = [tok(D_MODEL), tok(512), tok(512), tok(CONV_WIDTH),
                pl.BlockSpec((1, 1, 1, N_MOD * D_MODEL), lambda i: (layer, row(i), 0, 0)),
                _resident((1, D_MODEL)), _resident((1, D_MODEL)), _resident((1, D_MODEL))]
    in_specs += [_resident(w.shape) for w in weights]
    return pl.pallas_call(
        functools.partial(_post_kernel, last=last),
        grid=(n_tok // tile,),
        in_specs=in_specs,
        out_specs=tok(D_MODEL),
        out_shape=jax.ShapeDtypeStruct((n_tok, D_MODEL), F32),
        compiler_params=_params(1),
        name="post_latent" if latent else "post_ctx",
    )(x2d, da, ga, ca, mod, norm1, norm2, final_norm, *weights)


def _rope_tables(n_tokens):
    n_rows = n_tokens // GRID_W
    row = jnp.repeat(jnp.arange(n_rows), GRID_W).astype(F32)
    col = jnp.tile(jnp.arange(GRID_W), n_rows).astype(F32)
    axis_dim = HEAD_DIM // 2
    freqs = ROPE_THETA ** (-jnp.arange(0, axis_dim, 2, dtype=F32) / axis_dim)
    ang_r = row[:, None] * freqs[None, :]
    ang_c = col[:, None] * freqs[None, :]
    ang = jnp.concatenate([ang_r, ang_r, ang_c, ang_c], axis=-1)
    sign = jnp.tile(jnp.repeat(jnp.array([-1.0, 1.0], F32), HEAD_DIM // 4), 2)
    return jnp.tile(jnp.cos(ang), (1, 2)), jnp.tile(jnp.sin(ang) * sign, (1, 2))


def kernel(x_prompt, x_sample, cache_diff_k, cache_diff_v, cache_gqa_k, cache_gqa_v, c, c_ctx, w_ada, b_ada, norm1, norm2, w_in, diff_lq1, diff_lk1, diff_lq2, diff_lk2, diff_subln, w_diff_o, gqa_q_norm, gqa_k_norm, w_gqa_o, conv_dw, conv_dw_b, conv_ln_g, conv_ln_b, w_conv_o, w_o, w_mlp1, w_mlp2, final_norm):
    n_ctx, s_ctx, _ = x_prompt.shape
    n_lat, s_lat, _ = x_sample.shape
    past = cache_diff_k.shape[2]
    assert n_lat + 1 <= MOD_ROWS

    cvecs = jnp.concatenate([c_ctx[None], c, jnp.zeros((MOD_ROWS - 1 - n_lat, D_MODEL), F32)], axis=0)
    mod = _modulation(cvecs, w_ada, b_ada).reshape(DEPTH, MOD_ROWS, 1, N_MOD * D_MODEL)

    caches = (cache_diff_k.reshape(n_lat, DEPTH, past, 512), cache_diff_v.reshape(n_lat, DEPTH, past, 512),
              cache_gqa_k.reshape(n_lat, DEPTH, past, GQA_KV_WIDTH),
              cache_gqa_v.reshape(n_lat, DEPTH, past, GQA_KV_WIDTH))
    rope_tabs = _rope_tables(s_lat)
    row_vec = lambda p: p.reshape(1, -1)
    fn = row_vec(final_norm)

    groups = (dict(latent=False, n_seq=n_ctx, seq=s_ctx, tile=512, q_tile=s_ctx, seqs_per_step=1),
              dict(latent=True, n_seq=n_lat, seq=s_lat, tile=512, q_tile=512, seqs_per_step=1))
    xs = [x_prompt.reshape(n_ctx * s_ctx, D_MODEL), x_sample.reshape(n_lat * s_lat, D_MODEL)]
    new_cache = None

    mix_cols, gate_cols = (0, MIX_WIDTH), (MIX_WIDTH, MIX_WIDTH + GATE_WIDTH)
    whole = lambda w: ((0, w.shape[2]),)
    w_mix, w_gate = w_in, None

    for l in range(DEPTH):
        qn = row_vec(jnp.tile(gqa_q_norm[l], 2))
        kn = row_vec(jnp.tile(gqa_k_norm[l], 2))
        lam_params = jnp.stack([diff_lq1[l], diff_lk1[l], diff_lq2[l], diff_lk2[l]])
        n1, n2 = row_vec(norm1[l]), row_vec(norm2[l])
        conv_params = [conv_dw[l], row_vec(conv_dw_b[l]), row_vec(conv_ln_g[l]), row_vec(conv_ln_b[l])]
        next_in = [(w_in, l + 1, (mix_cols, gate_cols))] if l + 1 < DEPTH else []
        this_gate = [(w_in, l, (gate_cols,))] if w_gate is None else []
        pre_casts = ([(w_mlp1, l, whole(w_mlp1))], [(w_mlp2, l, whole(w_mlp2))])
        attn_casts = ([(w, l, whole(w)) for w in (w_diff_o, w_gqa_o, w_conv_o, w_o)] + this_gate + next_in, [])

        pre_outs = []
        for gi, g in enumerate(groups):
            outs = _pre(xs[gi], mod, l, n1, w_mix, qn, kn, rope_tabs, new_cache, pre_casts[gi],
                        latent=g["latent"], seq=g["seq"], tile=g["tile"])
            if not g["latent"]:
                new_cache = outs[7:11]
            pre_outs.append(outs)
        attn_outs = [_attention(pre_outs[gi][:7], caches, l, lam_params, row_vec(diff_subln[l]), conv_params,
                                attn_casts[gi], latent=g["latent"], n_seq=g["n_seq"], seq=g["seq"],
                                q_tile=g["q_tile"], seqs_per_step=g["seqs_per_step"])
                     for gi, g in enumerate(groups)]
        w_da_b, w_ga_b, w_co_b, w_o_b, *ctx_converted = attn_outs[0][3:]
        if this_gate:
            w_gate = ctx_converted.pop(0)
        post_w = [w_gate, w_da_b, w_ga_b, w_co_b, w_o_b, pre_outs[0][-1], pre_outs[1][-1]]
        next_in_b = ctx_converted
        for gi, g in enumerate(groups):
            da, ga, ca = attn_outs[gi][:3]
            xs[gi] = _post(xs[gi], da, ga, ca, mod, l, n1, n2, fn, post_w, latent=g["latent"], seq=g["seq"],
                           tile=g["tile"], last=(l == DEPTH - 1))
        if next_in_b:
            w_mix, w_gate = next_in_b

    ndk, ndv, ngk, ngv = new_cache
    lead = (n_ctx, DEPTH, s_ctx)
    return (xs[0].reshape(n_ctx, s_ctx, D_MODEL), xs[1].reshape(n_lat, s_lat, D_MODEL),
            ndk.reshape(lead + (DIFF_HEADS, 2, HEAD_DIM)), ndv.reshape(lead + (DIFF_HEADS, 2 * HEAD_DIM)),
            ngk.reshape(lead + (GQA_KV_HEADS, HEAD_DIM)), ngv.reshape(lead + (GQA_KV_HEADS, HEAD_DIM)))
```

```python
import functools
import math

import jax
import jax.numpy as jnp
from jax import lax
from jax.experimental import pallas as pl
from jax.experimental.pallas import tpu as pltpu

D_MODEL = 1024
DEPTH = 2
GRID_W = 64
ROPE_THETA = 10000.0
NORM_EPS = 1e-6

DIFF_HEADS = 4
HEAD_DIM = 64
DIFF_WIDTH = 512
GQA_KV_HEADS = 2
GQA_WIDTH = 512
GQA_KV_WIDTH = 128
CONV_WIDTH = 512
CONV_KSIZE = 31
CONV_HALO = 16
N_BRANCH = 3
MLP_HIDDEN = 4 * D_MODEL
N_MOD = 6

MIX_WIDTH = 3 * 512 + 512 + 2 * 128 + 2 * CONV_WIDTH
GATE_WIDTH = N_BRANCH * D_MODEL

LANES = 128
SUBLANES = 8
CONV_ROW_STRIDE = 4
MXU_WIDTH = 256
MOD_ROWS = 8
VMEM_LIMIT = 56 * 1024 * 1024

F32 = jnp.float32
BF16 = jnp.bfloat16


def _dot(a, b):
    return jnp.dot(a, b, preferred_element_type=F32)


def _dot_nt(a, b):
    return lax.dot_general(a, b, (((1,), (1,)), ((), ())), preferred_element_type=F32)


def _rms(x, gain):
    return x * lax.rsqrt(jnp.mean(x * x, axis=-1, keepdims=True) + NORM_EPS) * gain


def _modulated_norm(x, gain, shift, scale):
    return (_rms(x, gain) * (1.0 + scale) + shift).astype(BF16)


def _lane_iota(shape):
    return lax.broadcasted_iota(jnp.int32, shape, len(shape) - 1)


def _resident(shape):
    nd = len(shape)
    return pl.BlockSpec(shape, lambda *_: (0,) * nd, pipeline_mode=pl.Buffered(1))


def _params(n_axes):
    return pltpu.CompilerParams(dimension_semantics=("arbitrary",) * n_axes,
                                vmem_limit_bytes=VMEM_LIMIT)


def _mod_kernel(c_ref, w_ref, b_ref, o_ref):
    c = c_ref[...]
    s = (c * jax.nn.sigmoid(c)).astype(BF16)
    o_ref[0] = _dot(s, w_ref[0].astype(BF16)) + b_ref[0]


def _modulation(cvecs, w_ada, b_ada):
    width = N_MOD * D_MODEL
    tn = 1536
    return pl.pallas_call(
        _mod_kernel,
        grid=(DEPTH, width // tn),
        in_specs=[pl.BlockSpec((MOD_ROWS, D_MODEL), lambda l, j: (0, 0)),
                  pl.BlockSpec((1, D_MODEL, tn), lambda l, j: (l, 0, j)),
                  pl.BlockSpec((1, 1, tn), lambda l, j: (l, 0, j))],
        out_specs=pl.BlockSpec((1, MOD_ROWS, tn), lambda l, j: (l, 0, j)),
        out_shape=jax.ShapeDtypeStruct((DEPTH, MOD_ROWS, width), F32),
        compiler_params=_params(2),
        name="adaln_mod",
    )(cvecs, w_ada, b_ada.reshape(DEPTH, 1, width))


def _rope(x, cos, sin_signed, first_half):
    rot = jnp.where(first_half, pltpu.roll(x, LANES - 16, 1), pltpu.roll(x, 16, 1))
    return x * cos + rot * sin_signed


def _head_mean_sq(x):
    width = min(x.shape[1], MXU_WIDTH)
    r = lax.broadcasted_iota(jnp.int32, (width, width), 0) // HEAD_DIM
    c = lax.broadcasted_iota(jnp.int32, (width, width), 1) // HEAD_DIM
    blockdiag = jnp.where(r == c, 1.0 / HEAD_DIM, 0.0).astype(BF16)
    parts = []
    for lo in range(0, x.shape[1], width):
        sq = x[:, lo:lo + width] * x[:, lo:lo + width]
        parts.append(_dot(sq.astype(BF16), blockdiag))
    return parts[0] if len(parts) == 1 else jnp.concatenate(parts, axis=1)


def _store_cache(ref, val):
    seq = ref.shape[2]
    for b in range(ref.shape[0]):
        ref[b, 0] = val[b * seq:(b + 1) * seq]


def _cast_specs(casts, n_steps, step_of):
    in_specs, args, out_specs, out_shape = [], [], [], []
    for w, w_layer, splits in casts:
        _, n_rows, n_cols = w.shape
        blk = n_rows // n_steps
        assert n_rows % n_steps == 0 and blk % 16 == 0
        in_specs.append(pl.BlockSpec((1, blk, n_cols), lambda *g, w_layer=w_layer: (w_layer, step_of(*g), 0)))
        args.append(w)
        for lo, hi in splits:
            out_specs.append(pl.BlockSpec((blk, hi - lo), lambda *g: (step_of(*g), 0)))
            out_shape.append(jax.ShapeDtypeStruct((n_rows, hi - lo), BF16))
    return in_specs, args, out_specs, out_shape, tuple(tuple(s) for _, _, s in casts)


def _run_casts(in_refs, out_refs, cast_splits):
    out_refs = iter(out_refs)
    for w_ref, splits in zip(in_refs, cast_splits):
        for lo, hi in splits:
            next(out_refs)[...] = w_ref[0, :, lo:hi].astype(BF16)


def _pre_kernel(*refs, latent, n_aliased, f32_weights, cast_splits):
    if f32_weights:
        *refs, w_bf16_ref = refs
    n_in = 8 if latent else 6
    n_out = 7 if latent else 11
    cast_in = refs[n_in + n_aliased:n_in + n_aliased + len(cast_splits)]
    outs = refs[n_in + n_aliased + len(cast_splits):]
    _run_casts(cast_in, outs[n_out:], cast_splits)
    if latent:
        x_ref, mod_ref, n1_ref, w_ref, qn_ref, kn_ref, cos_ref, sin_ref = refs[:n_in]
        qd_ref, kd_ref, vd_ref, qg_ref, kk_ref, vv_ref, u_ref = outs[:n_out]
    else:
        x_ref, mod_ref, n1_ref, w_ref, qn_ref, kn_ref = refs[:n_in]
        (qd_ref, kd_ref, vd_ref, qg_ref, kk_ref, vv_ref, u_ref,
         ndk_ref, ndv_ref, ngk_ref, ngv_ref) = outs[:n_out]

    if f32_weights:
        @pl.when(pl.program_id(0) == 0)
        def _():
            w_bf16_ref[...] = w_ref[0].astype(BF16)
        w_ref = w_bf16_ref

    mod = mod_ref[0, 0]
    h = _modulated_norm(x_ref[...], n1_ref[...], mod[:, 0:D_MODEL], mod[:, D_MODEL:2 * D_MODEL])
    rows = h.shape[0]

    lane = _lane_iota((rows, LANES))
    low_half = lane < HEAD_DIM
    if latent:
        cos, sin_signed = cos_ref[...], sin_ref[...]
        first_half = ((lane % HEAD_DIM) // 16) % 2 == 0
        rope = lambda t: _rope(t, cos, sin_signed, first_half)
    else:
        rope = lambda t: t

    qk_scale = HEAD_DIM ** -0.5 * math.log2(math.e)

    def proj(lo, width):
        return _dot(h, w_ref[:, lo:lo + width])

    def slabs(t):
        return [t[:, s:s + LANES] for s in range(0, t.shape[1], LANES)]

    def store_slabs(ref, parts):
        for s, p in enumerate(parts):
            ref[:, s * LANES:(s + 1) * LANES] = p.astype(ref.dtype)

    store_slabs(qd_ref, [rope(t) * qk_scale for t in slabs(proj(0, DIFF_WIDTH))])
    dk = proj(DIFF_WIDTH, DIFF_WIDTH)
    dv = proj(2 * DIFF_WIDTH, DIFF_WIDTH)
    if not latent:
        _store_cache(ndk_ref, dk)
        _store_cache(ndv_ref, dv)
    store_slabs(kd_ref, [rope(t) for t in slabs(dk)])
    vd_ref[...] = dv.astype(BF16)

    gq = proj(3 * DIFF_WIDTH, GQA_WIDTH)
    gq = gq * lax.rsqrt(_head_mean_sq(gq) + NORM_EPS)
    qn = qn_ref[...]
    store_slabs(qg_ref, [rope(t * qn) * qk_scale for t in slabs(gq)])

    gkv = proj(3 * DIFF_WIDTH + GQA_WIDTH, 2 * GQA_KV_WIDTH)
    k, v = gkv[:, :GQA_KV_WIDTH], gkv[:, GQA_KV_WIDTH:]
    k = k * lax.rsqrt(_head_mean_sq(k) + NORM_EPS) * kn_ref[...]
    if not latent:
        _store_cache(ngk_ref, k)
        _store_cache(ngv_ref, v)
    k = rope(k)
    k_sw, v_sw = pltpu.roll(k, HEAD_DIM, 1), pltpu.roll(v, HEAD_DIM, 1)
    store_slabs(kk_ref, [jnp.where(low_half, k, k_sw), jnp.where(low_half, k_sw, k)])
    store_slabs(vv_ref, [jnp.where(low_half, v, v_sw), jnp.where(low_half, v_sw, v)])

    cv = proj(3 * DIFF_WIDTH + GQA_WIDTH + 2 * GQA_KV_WIDTH, 2 * CONV_WIDTH)
    u_ref[...] = cv[:, :CONV_WIDTH] * jax.nn.sigmoid(cv[:, CONV_WIDTH:])


def _pre(x2d, mod, layer, norm1, w_mix, qn, kn, rope_tabs, new_cache, casts, *, latent, seq, tile):
    n_tok = x2d.shape[0]
    tiles_per_seq = seq // tile if latent else 1
    row = (lambda i: 1 + i // tiles_per_seq) if latent else (lambda i: 0)
    tok = lambda w: pl.BlockSpec((tile, w), lambda i: (i, 0))
    f32_weights = w_mix.dtype == F32
    w_spec = (pl.BlockSpec((1, D_MODEL, MIX_WIDTH), lambda i: (layer, 0, 0), pipeline_mode=pl.Buffered(1))
              if f32_weights else _resident((D_MODEL, MIX_WIDTH)))
    in_specs = [tok(D_MODEL),
                pl.BlockSpec((1, 1, 1, N_MOD * D_MODEL), lambda i: (layer, row(i), 0, 0)),
                _resident((1, D_MODEL)), w_spec,
                _resident((1, LANES)), _resident((1, LANES))]
    args = [x2d, mod, norm1, w_mix, qn, kn]
    widths = [(512, BF16), (512, BF16), (512, BF16), (512, BF16), (256, BF16), (256, BF16), (512, F32)]
    out_specs = [tok(w) for w, _ in widths]
    out_shape = [jax.ShapeDtypeStruct((n_tok, w), dt) for w, dt in widths]
    aliases = {}
    if latent:
        in_specs += [pl.BlockSpec((tile, LANES), lambda i: (i % tiles_per_seq, 0))] * 2
        args += list(rope_tabs)
    else:
        per_tile = tile // seq
        for k, w in enumerate((512, 512, GQA_KV_WIDTH, GQA_KV_WIDTH)):
            if new_cache is not None:
                aliases[len(args)] = len(out_shape)
                in_specs.append(pl.BlockSpec(memory_space=pl.ANY))
                args.append(new_cache[k])
            out_specs.append(pl.BlockSpec((per_tile, 1, seq, w), lambda i: (i, layer, 0, 0)))
            out_shape.append(jax.ShapeDtypeStruct((n_tok // seq, DEPTH, seq, w), F32))
    c_in, c_args, c_out, c_shape, cast_splits = _cast_specs(casts, n_tok // tile, lambda i: i)
    in_specs, args, out_specs, out_shape = in_specs + c_in, args + c_args, out_specs + c_out, out_shape + c_shape
    return pl.pallas_call(
        functools.partial(_pre_kernel, latent=latent, n_aliased=len(aliases), f32_weights=f32_weights,
                          cast_splits=cast_splits),
        grid=(n_tok // tile,),
        in_specs=in_specs,
        out_specs=out_specs,
        out_shape=out_shape,
        scratch_shapes=[pltpu.VMEM((D_MODEL, MIX_WIDTH), BF16)] if f32_weights else [],
        input_output_aliases=aliases,
        compiler_params=_params(1),
        name="pre_latent" if latent else "pre_ctx",
    )(*args)


def _attend(q, keys, values):
    scores = [_dot_nt(q, k) for k in keys]
    m = functools.reduce(jnp.maximum, [jnp.max(s, axis=-1, keepdims=True) for s in scores])
    return functools.reduce(jnp.add, [_dot(jnp.exp2(s - m).astype(BF16), v) for s, v in zip(scores, values)])


def _attend_rowsum(q, keys, values):
    scores = [_dot_nt(q, k) for k in keys]
    m = functools.reduce(jnp.maximum, [jnp.max(s, axis=-1, keepdims=True) for s in scores])
    exps = [jnp.exp2(s - m) for s in scores]
    denom = functools.reduce(jnp.add, [jnp.sum(e, axis=-1, keepdims=True) for e in exps])
    out = functools.reduce(jnp.add, [_dot(e.astype(BF16), v) for e, v in zip(exps, values)])
    return out / denom


def _attn_kernel(*refs, latent, lam_init, cast_splits, n_seqs):
    n_core = 12 if latent else 8
    if latent:
        (qd_ref, kd_ref, vd_ref, qg_ref, kk_ref, vv_ref, cdk_ref, cdv_ref, cgk_ref, cgv_ref,
         lp_ref, sub_ref) = refs[:n_core]
    else:
        qd_ref, kd_ref, vd_ref, qg_ref, kk_ref, vv_ref, lp_ref, sub_ref = refs[:n_core]
    n_in = n_core + len(cast_splits)
    da_ref, ga_ref = refs[n_in:n_in + 2]
    _run_casts(refs[n_core:n_in], refs[n_in + 2:], cast_splits)

    rows = qd_ref.shape[0] // n_seqs
    kv_rows = kd_ref.shape[0] // n_seqs
    low_half = _lane_iota((rows, LANES)) < HEAD_DIM
    zero = jnp.zeros((), BF16)

    lp = lp_ref[...]
    lam = (jnp.exp(jnp.sum(lp[0:1] * lp[1:2], axis=-1, keepdims=True))
           - jnp.exp(jnp.sum(lp[2:3] * lp[3:4], axis=-1, keepdims=True)) + lam_init)
    sub_gain = sub_ref[...] * (1.0 - lam_init)

    def with_ones(v):
        return jnp.concatenate([v, jnp.ones_like(v)], axis=1)

    def value_slabs(v_dup):
        low = _lane_iota(v_dup.shape) < HEAD_DIM
        one = jnp.ones((), v_dup.dtype)
        return jnp.where(low, v_dup, one), jnp.where(low, one, v_dup)

    if latent:
        ck, cv = cgk_ref[0, 0], cgv_ref[0, 0]
        ck_sw, cv_sw = pltpu.roll(ck, HEAD_DIM, 1), pltpu.roll(cv, HEAD_DIM, 1)
        low_c = _lane_iota(ck.shape) < HEAD_DIM
        cache_k = [jnp.where(low_c, ck, ck_sw).astype(BF16), jnp.where(low_c, ck_sw, ck).astype(BF16)]
        cache_v = [value_slabs(jnp.where(low_c, cv, cv_sw).astype(BF16)),
                   value_slabs(jnp.where(low_c, cv_sw, cv).astype(BF16))]
    for s in range(n_seqs):
        qs = slice(s * rows, (s + 1) * rows)
        ks = slice(s * kv_rows, (s + 1) * kv_rows)
        for h in range(DIFF_HEADS):
            sl = slice(h * LANES, (h + 1) * LANES)
            q = qd_ref[qs, sl]
            q1, q2 = jnp.where(low_half, q, zero), jnp.where(low_half, zero, q)
            if latent:
                keys = [kd_ref[ks, sl], cdk_ref[0, 0, :, sl].astype(BF16)]
                values = [with_ones(vd_ref[ks, sl]), with_ones(cdv_ref[0, 0, :, sl].astype(BF16))]
                r1 = _attend(q1, keys, values)
                r2 = _attend(q2, keys, values)
                o = r1[:, :LANES] / r1[:, LANES:] - lam * (r2[:, :LANES] / r2[:, LANES:])
            else:
                keys, values = [kd_ref[ks, sl]], [vd_ref[ks, sl]]
                o = _attend_rowsum(q1, keys, values) - lam * _attend_rowsum(q2, keys, values)
            da_ref[qs, sl] = _rms(o, sub_gain).astype(BF16)

        for n in range(GQA_KV_HEADS):
            kv_sl = slice(n * LANES, (n + 1) * LANES)
            keys = [kk_ref[ks, kv_sl]]
            if latent:
                keys.append(cache_k[n])
                values = [value_slabs(vv_ref[ks, kv_sl]), cache_v[n]]
            for j in range(2):
                sl = slice((2 * n + j) * LANES, (2 * n + j + 1) * LANES)
                q = qg_ref[qs, sl]
                q_even, q_odd = jnp.where(low_half, q, zero), jnp.where(low_half, zero, q)
                if latent:
                    r_even = _attend(q_even, keys, [v[0] for v in values])
                    r_odd = _attend(q_odd, keys, [v[1] for v in values])
                    r = jnp.where(low_half, r_even, r_odd)
                    denom = jnp.where(low_half, pltpu.roll(r_even, HEAD_DIM, 1), pltpu.roll(r_odd, HEAD_DIM, 1))
                    out = r / denom
                else:
                    v_dup = [vv_ref[ks, kv_sl]]
                    out = jnp.where(low_half, _attend_rowsum(q_even, keys, v_dup), _attend_rowsum(q_odd, keys, v_dup))
                ga_ref[qs, sl] = out.astype(BF16)


def _attention(pre_outs, caches, layer, lam_params, subln, casts, *, latent, n_seq, seq, q_tile, seqs_per_step):
    qd, kd, vd, qg, kk, vv = pre_outs
    tiles = seq // q_tile
    assert seqs_per_step == 1 or tiles == 1
    n_seq //= seqs_per_step
    n_steps = n_seq * tiles
    q_spec = lambda w: pl.BlockSpec((seqs_per_step * q_tile, w), lambda b, i: (b * tiles + i, 0))
    kv_spec = lambda w: pl.BlockSpec((seqs_per_step * seq, w), lambda b, i: (b, 0))
    in_specs = [q_spec(512), kv_spec(512), kv_spec(512), q_spec(512), kv_spec(256), kv_spec(256)]
    args = [qd, kd, vd, qg, kk, vv]
    if latent:
        for c in caches:
            in_specs.append(pl.BlockSpec((1, 1) + c.shape[2:], lambda b, i: (b, layer, 0, 0)))
            args.append(c)
    in_specs += [_resident((4, HEAD_DIM)), _resident((1, LANES))]
    args += [lam_params, subln]
    out_specs = [q_spec(512), q_spec(512)]
    out_shape = [jax.ShapeDtypeStruct(qd.shape, BF16)] * 2
    c_in, c_args, c_out, c_shape, cast_splits = _cast_specs(casts, n_steps, lambda b, i: b * tiles + i)
    in_specs, args, out_specs, out_shape = in_specs + c_in, args + c_args, out_specs + c_out, out_shape + c_shape
    lam_init = 0.8 - 0.6 * math.exp(-0.3 * layer)
    return pl.pallas_call(
        functools.partial(_attn_kernel, latent=latent, lam_init=lam_init, cast_splits=cast_splits,
                          n_seqs=seqs_per_step),
        grid=(n_seq, tiles),
        in_specs=in_specs,
        out_specs=out_specs,
        out_shape=out_shape,
        compiler_params=_params(2),
        name="attn_latent" if latent else "attn_ctx",
    )(*args)


def _conv_branch(u_ref, head, tail, w_ref, bias, gain, beta, win_ref, y_ref, seg):
    n_seg = u_ref.shape[0] // seg
    pitch = seg + 2 * CONV_HALO
    lane_slabs = [slice(s * LANES, (s + 1) * LANES) for s in range(CONV_WIDTH // LANES)]
    zeros = jnp.zeros((CONV_HALO, LANES), F32)
    for k in range(n_seg):
        for s, ls in enumerate(lane_slabs):
            win_ref[s, k * pitch:k * pitch + CONV_HALO, :] = head[:, ls] if k == 0 else zeros
            win_ref[s, k * pitch + CONV_HALO:(k + 1) * pitch - CONV_HALO, :] = u_ref[k * seg:(k + 1) * seg, ls]
            win_ref[s, (k + 1) * pitch - CONV_HALO:(k + 1) * pitch, :] = tail[:, ls] if k == n_seg - 1 else zeros

    first = CONV_HALO - CONV_KSIZE // 2
    group = 16
    rows_per_group = group * SUBLANES
    for k in range(n_seg):
        for s, ls in enumerate(lane_slabs):
            for r0 in range(0, seg, rows_per_group):
                offs = [r + t for r in range(r0, r0 + rows_per_group, SUBLANES * CONV_ROW_STRIDE)
                        for t in range(CONV_ROW_STRIDE)]
                accs = [jnp.zeros((SUBLANES, LANES), F32)] * group
                for j in range(CONV_KSIZE):
                    w_tap = jnp.broadcast_to(w_ref[j:j + 1, ls], (SUBLANES, LANES))
                    for a, off in enumerate(offs):
                        start = k * pitch + off + first + j
                        accs[a] = accs[a] + win_ref[s, pl.ds(start, SUBLANES, stride=CONV_ROW_STRIDE), :] * w_tap
                for a, off in enumerate(offs):
                    y_ref[s, pl.ds(k * seg + off, SUBLANES, stride=CONV_ROW_STRIDE), :] = accs[a]
    acc = jnp.concatenate([y_ref[s] for s in range(len(lane_slabs))], axis=1) + bias
    mu = jnp.mean(acc, axis=-1, keepdims=True)
    xc = acc - mu
    y = xc * lax.rsqrt(jnp.mean(xc * xc, axis=-1, keepdims=True) + NORM_EPS) * gain + beta
    return y * jax.nn.sigmoid(y)


def _post_kernel(*refs, last, seg, tiles_per_seq):
    halo = tiles_per_seq > 1
    if halo:
        x_ref, da_ref, ga_ref, u_ref, uprev_ref, unext_ref = refs[:6]
        refs = refs[6:]
    else:
        x_ref, da_ref, ga_ref, u_ref = refs[:4]
        refs = refs[4:]
    (mod_ref, n1_ref, n2_ref, fn_ref, cw_ref, cb_ref, cg_ref, cbeta_ref,
     wg_ref, wda_ref, wga_ref, wco_ref, wo_ref, w1_ref, w2_ref, o_ref, win_ref, y_ref) = refs

    if halo:
        pos = pl.program_id(0) % tiles_per_seq
        head = jnp.where(pos > 0, uprev_ref[...], 0.0)
        tail = jnp.where(pos < tiles_per_seq - 1, unext_ref[...], 0.0)
    else:
        head = tail = jnp.zeros((CONV_HALO, CONV_WIDTH), F32)
    ca = _conv_branch(u_ref, head, tail, cw_ref, cb_ref[...], cg_ref[...], cbeta_ref[...], win_ref, y_ref, seg)

    x = x_ref[...]
    mod = mod_ref[0, 0]
    m = lambda k: mod[:, k * D_MODEL:(k + 1) * D_MODEL]
    h = _modulated_norm(x, n1_ref[...], m(0), m(1))

    branches = ((da_ref[...], wda_ref), (ga_ref[...], wga_ref), (ca.astype(BF16), wco_ref))
    merged = None
    for j, (act, w_ref) in enumerate(branches):
        gate = jax.nn.sigmoid(_dot(h, wg_ref[:, j * D_MODEL:(j + 1) * D_MODEL]))
        term = gate * _dot(act, w_ref[...])
        merged = term if merged is None else merged + term
    x = x + m(2) * _dot(merged.astype(BF16), wo_ref[...])

    h2 = _modulated_norm(x, n2_ref[...], m(3), m(4))
    hid = 1024
    f = None
    for c in range(0, MLP_HIDDEN, hid):
        a = jnp.maximum(_dot(h2, w1_ref[:, c:c + hid]), 0.0)
        term = _dot((a * a).astype(BF16), w2_ref[c:c + hid, :])
        f = term if f is None else f + term
    x = x + m(5) * f
    o_ref[...] = _rms(x, fn_ref[...]) if last else x


def _post(x2d, da, ga, u, mod, layer, norm1, norm2, final_norm, conv_params, weights, *, latent, seq, tile, last):
    n_tok = x2d.shape[0]
    assert seq % tile == 0 or tile % seq == 0
    tiles_per_seq = max(seq // tile, 1)
    seg = min(seq, tile)
    row = (lambda i: 1 + i // tiles_per_seq) if latent else (lambda i: 0)
    tok = lambda w: pl.BlockSpec((tile, w), lambda i: (i, 0))
    in_specs = [tok(D_MODEL), tok(512), tok(512), tok(CONV_WIDTH)]
    args = [x2d, da, ga, u]
    if tiles_per_seq > 1:
        per_tile, n_halo = tile // CONV_HALO, n_tok // CONV_HALO
        in_specs += [pl.BlockSpec((CONV_HALO, CONV_WIDTH), lambda i: (jnp.maximum(i * per_tile - 1, 0), 0)),
                     pl.BlockSpec((CONV_HALO, CONV_WIDTH),
                                  lambda i: (jnp.minimum((i + 1) * per_tile, n_halo - 1), 0))]
        args += [u, u]
    in_specs += [pl.BlockSpec((1, 1, 1, N_MOD * D_MODEL), lambda i: (layer, row(i), 0, 0)),
                 _resident((1, D_MODEL)), _resident((1, D_MODEL)), _resident((1, D_MODEL))]
    in_specs += [_resident(p.shape) for p in conv_params] + [_resident(w.shape) for w in weights]
    n_slab = CONV_WIDTH // LANES
    return pl.pallas_call(
        functools.partial(_post_kernel, last=last, seg=seg, tiles_per_seq=tiles_per_seq),
        grid=(n_tok // tile,),
        in_specs=in_specs,
        out_specs=tok(D_MODEL),
        out_shape=jax.ShapeDtypeStruct((n_tok, D_MODEL), F32),
        scratch_shapes=[pltpu.VMEM((n_slab, (tile // seg) * (seg + 2 * CONV_HALO), LANES), F32),
                        pltpu.VMEM((n_slab, tile, LANES), F32)],
        compiler_params=_params(1),
        name="post_latent" if latent else "post_ctx",
    )(*args, mod, norm1, norm2, final_norm, *conv_params, *weights)


def _rope_tables(n_tokens):
    n_rows = n_tokens // GRID_W
    row = jnp.repeat(jnp.arange(n_rows), GRID_W).astype(F32)
    col = jnp.tile(jnp.arange(GRID_W), n_rows).astype(F32)
    axis_dim = HEAD_DIM // 2
    freqs = ROPE_THETA ** (-jnp.arange(0, axis_dim, 2, dtype=F32) / axis_dim)
    ang_r = row[:, None] * freqs[None, :]
    ang_c = col[:, None] * freqs[None, :]
    ang = jnp.concatenate([ang_r, ang_r, ang_c, ang_c], axis=-1)
    sign = jnp.tile(jnp.repeat(jnp.array([-1.0, 1.0], F32), HEAD_DIM // 4), 2)
    return jnp.tile(jnp.cos(ang), (1, 2)), jnp.tile(jnp.sin(ang) * sign, (1, 2))


def kernel(x_prompt, x_sample, cache_diff_k, cache_diff_v, cache_gqa_k, cache_gqa_v, c, c_ctx, w_ada, b_ada, norm1, norm2, w_in, diff_lq1, diff_lk1, diff_lq2, diff_lk2, diff_subln, w_diff_o, gqa_q_norm, gqa_k_norm, w_gqa_o, conv_dw, conv_dw_b, conv_ln_g, conv_ln_b, w_conv_o, w_o, w_mlp1, w_mlp2, final_norm):
    n_ctx, s_ctx, _ = x_prompt.shape
    n_lat, s_lat, _ = x_sample.shape
    past = cache_diff_k.shape[2]
    assert n_lat + 1 <= MOD_ROWS

    cvecs = jnp.concatenate([c_ctx[None], c, jnp.zeros((MOD_ROWS - 1 - n_lat, D_MODEL), F32)], axis=0)
    mod = _modulation(cvecs, w_ada, b_ada).reshape(DEPTH, MOD_ROWS, 1, N_MOD * D_MODEL)

    caches = (cache_diff_k.reshape(n_lat, DEPTH, past, 512), cache_diff_v.reshape(n_lat, DEPTH, past, 512),
              cache_gqa_k.reshape(n_lat, DEPTH, past, GQA_KV_WIDTH),
              cache_gqa_v.reshape(n_lat, DEPTH, past, GQA_KV_WIDTH))
    rope_tabs = _rope_tables(s_lat)
    row_vec = lambda p: p.reshape(1, -1)
    fn = row_vec(final_norm)

    groups = (dict(latent=False, n_seq=n_ctx, seq=s_ctx, tile=512, q_tile=s_ctx, seqs_per_step=1),
              dict(latent=True, n_seq=n_lat, seq=s_lat, tile=512, q_tile=512, seqs_per_step=1))
    xs = [x_prompt.reshape(n_ctx * s_ctx, D_MODEL), x_sample.reshape(n_lat * s_lat, D_MODEL)]
    new_cache = None

    mix_cols, gate_cols = (0, MIX_WIDTH), (MIX_WIDTH, MIX_WIDTH + GATE_WIDTH)
    whole = lambda w: ((0, w.shape[2]),)
    w_mix, w_gate = w_in, None

    for l in range(DEPTH):
        qn = row_vec(jnp.tile(gqa_q_norm[l], 2))
        kn = row_vec(jnp.tile(gqa_k_norm[l], 2))
        lam_params = jnp.stack([diff_lq1[l], diff_lk1[l], diff_lq2[l], diff_lk2[l]])
        n1, n2 = row_vec(norm1[l]), row_vec(norm2[l])
        conv_params = [conv_dw[l], row_vec(conv_dw_b[l]), row_vec(conv_ln_g[l]), row_vec(conv_ln_b[l])]
        next_in = [(w_in, l + 1, (mix_cols, gate_cols))] if l + 1 < DEPTH else []
        this_gate = [(w_in, l, (gate_cols,))] if w_gate is None else []
        pre_casts = ([(w_mlp1, l, whole(w_mlp1))], [(w_mlp2, l, whole(w_mlp2))])
        attn_casts = ([(w, l, whole(w)) for w in (w_diff_o, w_gqa_o, w_conv_o, w_o)] + this_gate + next_in, [])

        pre_outs = []
        for gi, g in enumerate(groups):
            outs = _pre(xs[gi], mod, l, n1, w_mix, qn, kn, rope_tabs, new_cache, pre_casts[gi],
                        latent=g["latent"], seq=g["seq"], tile=g["tile"])
            if not g["latent"]:
                new_cache = outs[7:11]
            pre_outs.append(outs)
        attn_outs = [_attention(pre_outs[gi][:6], caches, l, lam_params, row_vec(diff_subln[l]), attn_casts[gi],
                                latent=g["latent"], n_seq=g["n_seq"], seq=g["seq"], q_tile=g["q_tile"],
                                seqs_per_step=g["seqs_per_step"]) for gi, g in enumerate(groups)]
        w_da_b, w_ga_b, w_co_b, w_o_b, *ctx_converted = attn_outs[0][2:]
        if this_gate:
            w_gate = ctx_converted.pop(0)
        post_w = [w_gate, w_da_b, w_ga_b, w_co_b, w_o_b, pre_outs[0][-1], pre_outs[1][-1]]
        next_in_b = ctx_converted
        for gi, g in enumerate(groups):
            da, ga = attn_outs[gi][:2]
            xs[gi] = _post(xs[gi], da, ga, pre_outs[gi][6], mod, l, n1, n2, fn, conv_params, post_w,
                           latent=g["latent"], seq=g["seq"], tile=g["tile"], last=(l == DEPTH - 1))
        if next_in_b:
            w_mix, w_gate = next_in_b

    ndk, ndv, ngk, ngv = new_cache
    lead = (n_ctx, DEPTH, s_ctx)
    return (xs[0].reshape(n_ctx, s_ctx, D_MODEL), xs[1].reshape(n_lat, s_lat, D_MODEL),
            ndk.reshape(lead + (DIFF_HEADS, 2, HEAD_DIM)), ndv.reshape(lead + (DIFF_HEADS, 2 * HEAD_DIM)),
            ngk.reshape(lead + (GQA_KV_HEADS, HEAD_DIM)), ngv.reshape(lead + (GQA_KV_HEADS, HEAD_DIM)))
```

```python
import functools
import math

import jax
import jax.numpy as jnp
from jax import lax
from jax.experimental import pallas as pl
from jax.experimental.pallas import tpu as pltpu

D_MODEL = 1024
DEPTH = 2
GRID_W = 64
ROPE_THETA = 10000.0
NORM_EPS = 1e-6

DIFF_HEADS = 4
HEAD_DIM = 64
DIFF_WIDTH = 512
GQA_KV_HEADS = 2
GQA_WIDTH = 512
GQA_KV_WIDTH = 128
CONV_WIDTH = 512
CONV_KSIZE = 31
CONV_HALO = 16
N_BRANCH = 3
MLP_HIDDEN = 4 * D_MODEL
N_MOD = 6

MIX_WIDTH = 3 * 512 + 512 + 2 * 128 + 2 * CONV_WIDTH
GATE_WIDTH = N_BRANCH * D_MODEL

LANES = 128
SUBLANES = 8
CONV_ROW_STRIDE = 4
MXU_WIDTH = 256
MOD_ROWS = 8
VMEM_LIMIT = 56 * 1024 * 1024

F32 = jnp.float32
BF16 = jnp.bfloat16


def _dot(a, b):
    return jnp.dot(a, b, preferred_element_type=F32)


def _dot_nt(a, b):
    return lax.dot_general(a, b, (((1,), (1,)), ((), ())), preferred_element_type=F32)


def _rms(x, gain):
    return x * lax.rsqrt(jnp.mean(x * x, axis=-1, keepdims=True) + NORM_EPS) * gain


def _modulated_norm(x, gain, shift, scale):
    return (_rms(x, gain) * (1.0 + scale) + shift).astype(BF16)


def _lane_iota(shape):
    return lax.broadcasted_iota(jnp.int32, shape, len(shape) - 1)


def _resident(shape):
    nd = len(shape)
    return pl.BlockSpec(shape, lambda *_: (0,) * nd, pipeline_mode=pl.Buffered(1))


def _params(n_axes):
    return pltpu.CompilerParams(dimension_semantics=("arbitrary",) * n_axes,
                                vmem_limit_bytes=VMEM_LIMIT)


def _mod_kernel(c_ref, w_ref, b_ref, o_ref):
    c = c_ref[...]
    s = (c * jax.nn.sigmoid(c)).astype(BF16)
    o_ref[0] = _dot(s, w_ref[0].astype(BF16)) + b_ref[0]


def _modulation(cvecs, w_ada, b_ada):
    width = N_MOD * D_MODEL
    tn = 1536
    return pl.pallas_call(
        _mod_kernel,
        grid=(DEPTH, width // tn),
        in_specs=[pl.BlockSpec((MOD_ROWS, D_MODEL), lambda l, j: (0, 0)),
                  pl.BlockSpec((1, D_MODEL, tn), lambda l, j: (l, 0, j)),
                  pl.BlockSpec((1, 1, tn), lambda l, j: (l, 0, j))],
        out_specs=pl.BlockSpec((1, MOD_ROWS, tn), lambda l, j: (l, 0, j)),
        out_shape=jax.ShapeDtypeStruct((DEPTH, MOD_ROWS, width), F32),
        compiler_params=_params(2),
        name="adaln_mod",
    )(cvecs, w_ada, b_ada.reshape(DEPTH, 1, width))


def _rope(x, cos, sin_signed, first_half):
    rot = jnp.where(first_half, pltpu.roll(x, LANES - 16, 1), pltpu.roll(x, 16, 1))
    return x * cos + rot * sin_signed


def _head_mean_sq(x):
    width = min(x.shape[1], MXU_WIDTH)
    r = lax.broadcasted_iota(jnp.int32, (width, width), 0) // HEAD_DIM
    c = lax.broadcasted_iota(jnp.int32, (width, width), 1) // HEAD_DIM
    blockdiag = jnp.where(r == c, 1.0 / HEAD_DIM, 0.0).astype(BF16)
    parts = []
    for lo in range(0, x.shape[1], width):
        sq = x[:, lo:lo + width] * x[:, lo:lo + width]
        parts.append(_dot(sq.astype(BF16), blockdiag))
    return parts[0] if len(parts) == 1 else jnp.concatenate(parts, axis=1)


def _store_cache(ref, val):
    seq = ref.shape[2]
    for b in range(ref.shape[0]):
        ref[b, 0] = val[b * seq:(b + 1) * seq]


def _cast_specs(casts, n_steps, step_of):
    in_specs, args, out_specs, out_shape = [], [], [], []
    for w, w_layer, splits in casts:
        _, n_rows, n_cols = w.shape
        blk = n_rows // n_steps
        assert n_rows % n_steps == 0 and blk % 16 == 0
        in_specs.append(pl.BlockSpec((1, blk, n_cols), lambda *g, w_layer=w_layer: (w_layer, step_of(*g), 0)))
        args.append(w)
        for lo, hi in splits:
            out_specs.append(pl.BlockSpec((blk, hi - lo), lambda *g: (step_of(*g), 0)))
            out_shape.append(jax.ShapeDtypeStruct((n_rows, hi - lo), BF16))
    return in_specs, args, out_specs, out_shape, tuple(tuple(s) for _, _, s in casts)


def _run_casts(in_refs, out_refs, cast_splits):
    out_refs = iter(out_refs)
    for w_ref, splits in zip(in_refs, cast_splits):
        for lo, hi in splits:
            next(out_refs)[...] = w_ref[0, :, lo:hi].astype(BF16)


def _pre_kernel(*refs, latent, n_aliased, f32_weights, cast_splits):
    if f32_weights:
        *refs, w_bf16_ref = refs
    n_in = 8 if latent else 6
    n_out = 7 if latent else 11
    cast_in = refs[n_in + n_aliased:n_in + n_aliased + len(cast_splits)]
    outs = refs[n_in + n_aliased + len(cast_splits):]
    _run_casts(cast_in, outs[n_out:], cast_splits)
    if latent:
        x_ref, mod_ref, n1_ref, w_ref, qn_ref, kn_ref, cos_ref, sin_ref = refs[:n_in]
        qd_ref, kd_ref, vd_ref, qg_ref, kk_ref, vv_ref, u_ref = outs[:n_out]
    else:
        x_ref, mod_ref, n1_ref, w_ref, qn_ref, kn_ref = refs[:n_in]
        (qd_ref, kd_ref, vd_ref, qg_ref, kk_ref, vv_ref, u_ref,
         ndk_ref, ndv_ref, ngk_ref, ngv_ref) = outs[:n_out]

    if f32_weights:
        @pl.when(pl.program_id(0) == 0)
        def _():
            w_bf16_ref[...] = w_ref[0].astype(BF16)
        w_ref = w_bf16_ref

    mod = mod_ref[0, 0]
    h = _modulated_norm(x_ref[...], n1_ref[...], mod[:, 0:D_MODEL], mod[:, D_MODEL:2 * D_MODEL])
    rows = h.shape[0]

    lane = _lane_iota((rows, LANES))
    low_half = lane < HEAD_DIM
    if latent:
        cos, sin_signed = cos_ref[...], sin_ref[...]
        first_half = ((lane % HEAD_DIM) // 16) % 2 == 0
        rope = lambda t: _rope(t, cos, sin_signed, first_half)
    else:
        rope = lambda t: t

    qk_scale = HEAD_DIM ** -0.5 * math.log2(math.e)

    def proj(lo, width):
        return _dot(h, w_ref[:, lo:lo + width])

    def slabs(t):
        return [t[:, s:s + LANES] for s in range(0, t.shape[1], LANES)]

    def store_slabs(ref, parts):
        for s, p in enumerate(parts):
            ref[:, s * LANES:(s + 1) * LANES] = p.astype(ref.dtype)

    store_slabs(qd_ref, [rope(t) * qk_scale for t in slabs(proj(0, DIFF_WIDTH))])
    dk = proj(DIFF_WIDTH, DIFF_WIDTH)
    dv = proj(2 * DIFF_WIDTH, DIFF_WIDTH)
    if not latent:
        _store_cache(ndk_ref, dk)
        _store_cache(ndv_ref, dv)
    store_slabs(kd_ref, [rope(t) for t in slabs(dk)])
    vd_ref[...] = dv.astype(BF16)

    gq = proj(3 * DIFF_WIDTH, GQA_WIDTH)
    gq = gq * lax.rsqrt(_head_mean_sq(gq) + NORM_EPS)
    qn = qn_ref[...]
    store_slabs(qg_ref, [rope(t * qn) * qk_scale for t in slabs(gq)])

    gkv = proj(3 * DIFF_WIDTH + GQA_WIDTH, 2 * GQA_KV_WIDTH)
    k, v = gkv[:, :GQA_KV_WIDTH], gkv[:, GQA_KV_WIDTH:]
    k = k * lax.rsqrt(_head_mean_sq(k) + NORM_EPS) * kn_ref[...]
    if not latent:
        _store_cache(ngk_ref, k)
        _store_cache(ngv_ref, v)
    k = rope(k)
    k_sw, v_sw = pltpu.roll(k, HEAD_DIM, 1), pltpu.roll(v, HEAD_DIM, 1)
    store_slabs(kk_ref, [jnp.where(low_half, k, k_sw), jnp.where(low_half, k_sw, k)])
    store_slabs(vv_ref, [jnp.where(low_half, v, v_sw), jnp.where(low_half, v_sw, v)])

    cv = proj(3 * DIFF_WIDTH + GQA_WIDTH + 2 * GQA_KV_WIDTH, 2 * CONV_WIDTH)
    u_ref[...] = cv[:, :CONV_WIDTH] * jax.nn.sigmoid(cv[:, CONV_WIDTH:])


def _pre(x2d, mod, layer, norm1, w_mix, qn, kn, rope_tabs, new_cache, casts, *, latent, seq, tile):
    n_tok = x2d.shape[0]
    tiles_per_seq = seq // tile if latent else 1
    row = (lambda i: 1 + i // tiles_per_seq) if latent else (lambda i: 0)
    tok = lambda w: pl.BlockSpec((tile, w), lambda i: (i, 0))
    f32_weights = w_mix.dtype == F32
    w_spec = (pl.BlockSpec((1, D_MODEL, MIX_WIDTH), lambda i: (layer, 0, 0), pipeline_mode=pl.Buffered(1))
              if f32_weights else _resident((D_MODEL, MIX_WIDTH)))
    in_specs = [tok(D_MODEL),
                pl.BlockSpec((1, 1, 1, N_MOD * D_MODEL), lambda i: (layer, row(i), 0, 0)),
                _resident((1, D_MODEL)), w_spec,
                _resident((1, LANES)), _resident((1, LANES))]
    args = [x2d, mod, norm1, w_mix, qn, kn]
    widths = [(512, BF16), (512, BF16), (512, BF16), (512, BF16), (256, BF16), (256, BF16), (512, F32)]
    out_specs = [tok(w) for w, _ in widths]
    out_shape = [jax.ShapeDtypeStruct((n_tok, w), dt) for w, dt in widths]
    aliases = {}
    if latent:
        in_specs += [pl.BlockSpec((tile, LANES), lambda i: (i % tiles_per_seq, 0))] * 2
        args += list(rope_tabs)
    else:
        per_tile = tile // seq
        for k, w in enumerate((512, 512, GQA_KV_WIDTH, GQA_KV_WIDTH)):
            if new_cache is not None:
                aliases[len(args)] = len(out_shape)
                in_specs.append(pl.BlockSpec(memory_space=pl.ANY))
                args.append(new_cache[k])
            out_specs.append(pl.BlockSpec((per_tile, 1, seq, w), lambda i: (i, layer, 0, 0)))
            out_shape.append(jax.ShapeDtypeStruct((n_tok // seq, DEPTH, seq, w), F32))
    c_in, c_args, c_out, c_shape, cast_splits = _cast_specs(casts, n_tok // tile, lambda i: i)
    in_specs, args, out_specs, out_shape = in_specs + c_in, args + c_args, out_specs + c_out, out_shape + c_shape
    return pl.pallas_call(
        functools.partial(_pre_kernel, latent=latent, n_aliased=len(aliases), f32_weights=f32_weights,
                          cast_splits=cast_splits),
        grid=(n_tok // tile,),
        in_specs=in_specs,
        out_specs=out_specs,
        out_shape=out_shape,
        scratch_shapes=[pltpu.VMEM((D_MODEL, MIX_WIDTH), BF16)] if f32_weights else [],
        input_output_aliases=aliases,
        compiler_params=_params(1),
        name="pre_latent" if latent else "pre_ctx",
    )(*args)


def _attend(q, keys, values):
    scores = [_dot_nt(q, k) for k in keys]
    m = functools.reduce(jnp.maximum, [jnp.max(s, axis=-1, keepdims=True) for s in scores])
    return functools.reduce(jnp.add, [_dot(jnp.exp2(s - m).astype(BF16), v) for s, v in zip(scores, values)])


def _attend_rowsum(q, keys, values):
    scores = [_dot_nt(q, k) for k in keys]
    m = functools.reduce(jnp.maximum, [jnp.max(s, axis=-1, keepdims=True) for s in scores])
    exps = [jnp.exp2(s - m) for s in scores]
    denom = functools.reduce(jnp.add, [jnp.sum(e, axis=-1, keepdims=True) for e in exps])
    out = functools.reduce(jnp.add, [_dot(e.astype(BF16), v) for e, v in zip(exps, values)])
    return out / denom


def _attn_kernel(*refs, latent, lam_init, cast_splits, n_seqs):
    n_core = 12 if latent else 8
    if latent:
        (qd_ref, kd_ref, vd_ref, qg_ref, kk_ref, vv_ref, cdk_ref, cdv_ref, cgk_ref, cgv_ref,
         lp_ref, sub_ref) = refs[:n_core]
    else:
        qd_ref, kd_ref, vd_ref, qg_ref, kk_ref, vv_ref, lp_ref, sub_ref = refs[:n_core]
    n_in = n_core + len(cast_splits)
    da_ref, ga_ref = refs[n_in:n_in + 2]
    _run_casts(refs[n_core:n_in], refs[n_in + 2:], cast_splits)

    rows = qd_ref.shape[0] // n_seqs
    kv_rows = kd_ref.shape[0] // n_seqs
    low_half = _lane_iota((rows, LANES)) < HEAD_DIM
    zero = jnp.zeros((), BF16)

    lp = lp_ref[...]
    lam = (jnp.exp(jnp.sum(lp[0:1] * lp[1:2], axis=-1, keepdims=True))
           - jnp.exp(jnp.sum(lp[2:3] * lp[3:4], axis=-1, keepdims=True)) + lam_init)
    sub_gain = sub_ref[...] * (1.0 - lam_init)

    def with_ones(v):
        return jnp.concatenate([v, jnp.ones_like(v)], axis=1)

    def value_slabs(v_dup):
        low = _lane_iota(v_dup.shape) < HEAD_DIM
        one = jnp.ones((), v_dup.dtype)
        return jnp.where(low, v_dup, one), jnp.where(low, one, v_dup)

    if latent:
        ck, cv = cgk_ref[0, 0], cgv_ref[0, 0]
        ck_sw, cv_sw = pltpu.roll(ck, HEAD_DIM, 1), pltpu.roll(cv, HEAD_DIM, 1)
        low_c = _lane_iota(ck.shape) < HEAD_DIM
        cache_k = [jnp.where(low_c, ck, ck_sw).astype(BF16), jnp.where(low_c, ck_sw, ck).astype(BF16)]
        cache_v = [value_slabs(jnp.where(low_c, cv, cv_sw).astype(BF16)),
                   value_slabs(jnp.where(low_c, cv_sw, cv).astype(BF16))]
    for s in range(n_seqs):
        qs = slice(s * rows, (s + 1) * rows)
        ks = slice(s * kv_rows, (s + 1) * kv_rows)
        for h in range(DIFF_HEADS):
            sl = slice(h * LANES, (h + 1) * LANES)
            q = qd_ref[qs, sl]
            q1, q2 = jnp.where(low_half, q, zero), jnp.where(low_half, zero, q)
            if latent:
                keys = [kd_ref[ks, sl], cdk_ref[0, 0, :, sl].astype(BF16)]
                values = [with_ones(vd_ref[ks, sl]), with_ones(cdv_ref[0, 0, :, sl].astype(BF16))]
                r1 = _attend(q1, keys, values)
                r2 = _attend(q2, keys, values)
                o = r1[:, :LANES] / r1[:, LANES:] - lam * (r2[:, :LANES] / r2[:, LANES:])
            else:
                keys, values = [kd_ref[ks, sl]], [vd_ref[ks, sl]]
                o = _attend_rowsum(q1, keys, values) - lam * _attend_rowsum(q2, keys, values)
            da_ref[qs, sl] = _rms(o, sub_gain).astype(BF16)

        for n in range(GQA_KV_HEADS):
            kv_sl = slice(n * LANES, (n + 1) * LANES)
            keys = [kk_ref[ks, kv_sl]]
            if latent:
                keys.append(cache_k[n])
                values = [value_slabs(vv_ref[ks, kv_sl]), cache_v[n]]
            for j in range(2):
                sl = slice((2 * n + j) * LANES, (2 * n + j + 1) * LANES)
                q = qg_ref[qs, sl]
                q_even, q_odd = jnp.where(low_half, q, zero), jnp.where(low_half, zero, q)
                if latent:
                    r_even = _attend(q_even, keys, [v[0] for v in values])
                    r_odd = _attend(q_odd, keys, [v[1] for v in values])
                    r = jnp.where(low_half, r_even, r_odd)
                    denom = jnp.where(low_half, pltpu.roll(r_even, HEAD_DIM, 1), pltpu.roll(r_odd, HEAD_DIM, 1))
                    out = r / denom
                else:
                    v_dup = [vv_ref[ks, kv_sl]]
                    out = jnp.where(low_half, _attend_rowsum(q_even, keys, v_dup), _attend_rowsum(q_odd, keys, v_dup))
                ga_ref[qs, sl] = out.astype(BF16)


def _attention(pre_outs, caches, layer, lam_params, subln, casts, *, latent, n_seq, seq, q_tile, seqs_per_step):
    qd, kd, vd, qg, kk, vv = pre_outs
    tiles = seq // q_tile
    assert seqs_per_step == 1 or tiles == 1
    n_seq //= seqs_per_step
    n_steps = n_seq * tiles
    q_spec = lambda w: pl.BlockSpec((seqs_per_step * q_tile, w), lambda b, i: (b * tiles + i, 0))
    kv_spec = lambda w: pl.BlockSpec((seqs_per_step * seq, w), lambda b, i: (b, 0))
    in_specs = [q_spec(512), kv_spec(512), kv_spec(512), q_spec(512), kv_spec(256), kv_spec(256)]
    args = [qd, kd, vd, qg, kk, vv]
    if latent:
        for c in caches:
            in_specs.append(pl.BlockSpec((1, 1) + c.shape[2:], lambda b, i: (b, layer, 0, 0)))
            args.append(c)
    in_specs += [_resident((4, HEAD_DIM)), _resident((1, LANES))]
    args += [lam_params, subln]
    out_specs = [q_spec(512), q_spec(512)]
    out_shape = [jax.ShapeDtypeStruct(qd.shape, BF16)] * 2
    c_in, c_args, c_out, c_shape, cast_splits = _cast_specs(casts, n_steps, lambda b, i: b * tiles + i)
    in_specs, args, out_specs, out_shape = in_specs + c_in, args + c_args, out_specs + c_out, out_shape + c_shape
    lam_init = 0.8 - 0.6 * math.exp(-0.3 * layer)
    return pl.pallas_call(
        functools.partial(_attn_kernel, latent=latent, lam_init=lam_init, cast_splits=cast_splits,
                          n_seqs=seqs_per_step),
        grid=(n_seq, tiles),
        in_specs=in_specs,
        out_specs=out_specs,
        out_shape=out_shape,
        compiler_params=_params(2),
        name="attn_latent" if latent else "attn_ctx",
    )(*args)


def _conv_branch(u_ref, head, tail, w_ref, bias, gain, beta, win_ref, y_ref, seg):
    n_seg = u_ref.shape[0] // seg
    pitch = seg + 2 * CONV_HALO
    lane_slabs = [slice(s * LANES, (s + 1) * LANES) for s in range(CONV_WIDTH // LANES)]
    zeros = jnp.zeros((CONV_HALO, LANES), F32)
    for k in range(n_seg):
        for s, ls in enumerate(lane_slabs):
            win_ref[s, k * pitch:k * pitch + CONV_HALO, :] = head[:, ls] if k == 0 else zeros
            win_ref[s, k * pitch + CONV_HALO:(k + 1) * pitch - CONV_HALO, :] = u_ref[k * seg:(k + 1) * seg, ls]
            win_ref[s, (k + 1) * pitch - CONV_HALO:(k + 1) * pitch, :] = tail[:, ls] if k == n_seg - 1 else zeros

    first = CONV_HALO - CONV_KSIZE // 2
    group = 16
    rows_per_group = group * SUBLANES
    for k in range(n_seg):
        for s, ls in enumerate(lane_slabs):
            for r0 in range(0, seg, rows_per_group):
                offs = [r + t for r in range(r0, r0 + rows_per_group, SUBLANES * CONV_ROW_STRIDE)
                        for t in range(CONV_ROW_STRIDE)]
                accs = [jnp.zeros((SUBLANES, LANES), F32)] * group
                for j in range(CONV_KSIZE):
                    w_tap = jnp.broadcast_to(w_ref[j:j + 1, ls], (SUBLANES, LANES))
                    for a, off in enumerate(offs):
                        start = k * pitch + off + first + j
                        accs[a] = accs[a] + win_ref[s, pl.ds(start, SUBLANES, stride=CONV_ROW_STRIDE), :] * w_tap
                for a, off in enumerate(offs):
                    y_ref[s, pl.ds(k * seg + off, SUBLANES, stride=CONV_ROW_STRIDE), :] = accs[a]
    acc = jnp.concatenate([y_ref[s] for s in range(len(lane_slabs))], axis=1) + bias
    mu = jnp.mean(acc, axis=-1, keepdims=True)
    xc = acc - mu
    y = xc * lax.rsqrt(jnp.mean(xc * xc, axis=-1, keepdims=True) + NORM_EPS) * gain + beta
    return y * jax.nn.sigmoid(y)


def _post_kernel(*refs, last, seg, tiles_per_seq):
    halo = tiles_per_seq > 1
    if halo:
        x_ref, da_ref, ga_ref, u_ref, uprev_ref, unext_ref = refs[:6]
        refs = refs[6:]
    else:
        x_ref, da_ref, ga_ref, u_ref = refs[:4]
        refs = refs[4:]
    (mod_ref, n1_ref, n2_ref, fn_ref, cw_ref, cb_ref, cg_ref, cbeta_ref,
     wg_ref, wda_ref, wga_ref, wco_ref, wo_ref, w1_ref, w2_ref, o_ref, win_ref, y_ref) = refs

    if halo:
        pos = pl.program_id(0) % tiles_per_seq
        head = jnp.where(pos > 0, uprev_ref[...], 0.0)
        tail = jnp.where(pos < tiles_per_seq - 1, unext_ref[...], 0.0)
    else:
        head = tail = jnp.zeros((CONV_HALO, CONV_WIDTH), F32)
    ca = _conv_branch(u_ref, head, tail, cw_ref, cb_ref[...], cg_ref[...], cbeta_ref[...], win_ref, y_ref, seg)

    x = x_ref[...]
    mod = mod_ref[0, 0]
    m = lambda k: mod[:, k * D_MODEL:(k + 1) * D_MODEL]
    h = _modulated_norm(x, n1_ref[...], m(0), m(1))

    branches = ((da_ref[...], wda_ref), (ga_ref[...], wga_ref), (ca.astype(BF16), wco_ref))
    merged = None
    for j, (act, w_ref) in enumerate(branches):
        gate = jax.nn.sigmoid(_dot(h, wg_ref[:, j * D_MODEL:(j + 1) * D_MODEL]))
        term = gate * _dot(act, w_ref[...])
        merged = term if merged is None else merged + term
    x = x + m(2) * _dot(merged.astype(BF16), wo_ref[...])

    h2 = _modulated_norm(x, n2_ref[...], m(3), m(4))
    hid = 1024
    f = None
    for c in range(0, MLP_HIDDEN, hid):
        a = jnp.maximum(_dot(h2, w1_ref[:, c:c + hid]), 0.0)
        term = _dot((a * a).astype(BF16), w2_ref[c:c + hid, :])
        f = term if f is None else f + term
    x = x + m(5) * f
    o_ref[...] = _rms(x, fn_ref[...]) if last else x


def _post(x2d, da, ga, u, mod, layer, norm1, norm2, final_norm, conv_params, weights, *, latent, seq, tile, last):
    n_tok = x2d.shape[0]
    assert seq % tile == 0 or tile % seq == 0
    tiles_per_seq = max(seq // tile, 1)
    seg = min(seq, tile)
    row = (lambda i: 1 + i // tiles_per_seq) if latent else (lambda i: 0)
    tok = lambda w: pl.BlockSpec((tile, w), lambda i: (i, 0))
    in_specs = [tok(D_MODEL), tok(512), tok(512), tok(CONV_WIDTH)]
    args = [x2d, da, ga, u]
    if tiles_per_seq > 1:
        per_tile, n_halo = tile // CONV_HALO, n_tok // CONV_HALO
        in_specs += [pl.BlockSpec((CONV_HALO, CONV_WIDTH), lambda i: (jnp.maximum(i * per_tile - 1, 0), 0)),
                     pl.BlockSpec((CONV_HALO, CONV_WIDTH),
                                  lambda i: (jnp.minimum((i + 1) * per_tile, n_halo - 1), 0))]
        args += [u, u]
    in_specs += [pl.BlockSpec((1, 1, 1, N_MOD * D_MODEL), lambda i: (layer, row(i), 0, 0)),
                 _resident((1, D_MODEL)), _resident((1, D_MODEL)), _resident((1, D_MODEL))]
    in_specs += [_resident(p.shape) for p in conv_params] + [_resident(w.shape) for w in weights]
    n_slab = CONV_WIDTH // LANES
    return pl.pallas_call(
        functools.partial(_post_kernel, last=last, seg=seg, tiles_per_seq=tiles_per_seq),
        grid=(n_tok // tile,),
        in_specs=in_specs,
        out_specs=tok(D_MODEL),
        out_shape=jax.ShapeDtypeStruct((n_tok, D_MODEL), F32),
        scratch_shapes=[pltpu.VMEM((n_slab, (tile // seg) * (seg + 2 * CONV_HALO), LANES), F32),
                        pltpu.VMEM((n_slab, tile, LANES), F32)],
        compiler_params=_params(1),
        name="post_latent" if latent else "post_ctx",
    )(*args, mod, norm1, norm2, final_norm, *conv_params, *weights)


def _rope_tables(n_tokens):
    n_rows = n_tokens // GRID_W
    row = jnp.repeat(jnp.arange(n_rows), GRID_W).astype(F32)
    col = jnp.tile(jnp.arange(GRID_W), n_rows).astype(F32)
    axis_dim = HEAD_DIM // 2
    freqs = ROPE_THETA ** (-jnp.arange(0, axis_dim, 2, dtype=F32) / axis_dim)
    ang_r = row[:, None] * freqs[None, :]
    ang_c = col[:, None] * freqs[None, :]
    ang = jnp.concatenate([ang_r, ang_r, ang_c, ang_c], axis=-1)
    sign = jnp.tile(jnp.repeat(jnp.array([-1.0, 1.0], F32), HEAD_DIM // 4), 2)
    return jnp.tile(jnp.cos(ang), (1, 2)), jnp.tile(jnp.sin(ang) * sign, (1, 2))


def kernel(x_prompt, x_sample, cache_diff_k, cache_diff_v, cache_gqa_k, cache_gqa_v, c, c_ctx, w_ada, b_ada, norm1, norm2, w_in, diff_lq1, diff_lk1, diff_lq2, diff_lk2, diff_subln, w_diff_o, gqa_q_norm, gqa_k_norm, w_gqa_o, conv_dw, conv_dw_b, conv_ln_g, conv_ln_b, w_conv_o, w_o, w_mlp1, w_mlp2, final_norm):
    n_ctx, s_ctx, _ = x_prompt.shape
    n_lat, s_lat, _ = x_sample.shape
    past = cache_diff_k.shape[2]
    assert n_lat + 1 <= MOD_ROWS

    cvecs = jnp.concatenate([c_ctx[None], c, jnp.zeros((MOD_ROWS - 1 - n_lat, D_MODEL), F32)], axis=0)
    mod = _modulation(cvecs, w_ada, b_ada).reshape(DEPTH, MOD_ROWS, 1, N_MOD * D_MODEL)

    caches = (cache_diff_k.reshape(n_lat, DEPTH, past, 512), cache_diff_v.reshape(n_lat, DEPTH, past, 512),
              cache_gqa_k.reshape(n_lat, DEPTH, past, GQA_KV_WIDTH),
              cache_gqa_v.reshape(n_lat, DEPTH, past, GQA_KV_WIDTH))
    rope_tabs = _rope_tables(s_lat)
    row_vec = lambda p: p.reshape(1, -1)
    fn = row_vec(final_norm)

    groups = (dict(latent=False, n_seq=n_ctx, seq=s_ctx, tile=512, q_tile=s_ctx, seqs_per_step=1),
              dict(latent=True, n_seq=n_lat, seq=s_lat, tile=512, q_tile=1024, seqs_per_step=1))
    xs = [x_prompt.reshape(n_ctx * s_ctx, D_MODEL), x_sample.reshape(n_lat * s_lat, D_MODEL)]
    new_cache = None

    mix_cols, gate_cols = (0, MIX_WIDTH), (MIX_WIDTH, MIX_WIDTH + GATE_WIDTH)
    whole = lambda w: ((0, w.shape[2]),)
    w_mix, w_gate = w_in, None

    for l in range(DEPTH):
        qn = row_vec(jnp.tile(gqa_q_norm[l], 2))
        kn = row_vec(jnp.tile(gqa_k_norm[l], 2))
        lam_params = jnp.stack([diff_lq1[l], diff_lk1[l], diff_lq2[l], diff_lk2[l]])
        n1, n2 = row_vec(norm1[l]), row_vec(norm2[l])
        conv_params = [conv_dw[l], row_vec(conv_dw_b[l]), row_vec(conv_ln_g[l]), row_vec(conv_ln_b[l])]
        next_in = [(w_in, l + 1, (mix_cols, gate_cols))] if l + 1 < DEPTH else []
        this_gate = [(w_in, l, (gate_cols,))] if w_gate is None else []
        pre_casts = ([(w_mlp1, l, whole(w_mlp1))], [(w_mlp2, l, whole(w_mlp2))])
        attn_casts = ([(w, l, whole(w)) for w in (w_diff_o, w_gqa_o, w_conv_o, w_o)] + this_gate + next_in, [])

        pre_outs = []
        for gi, g in enumerate(groups):
            outs = _pre(xs[gi], mod, l, n1, w_mix, qn, kn, rope_tabs, new_cache, pre_casts[gi],
                        latent=g["latent"], seq=g["seq"], tile=g["tile"])
            if not g["latent"]:
                new_cache = outs[7:11]
            pre_outs.append(outs)
        attn_outs = [_attention(pre_outs[gi][:6], caches, l, lam_params, row_vec(diff_subln[l]), attn_casts[gi],
                                latent=g["latent"], n_seq=g["n_seq"], seq=g["seq"], q_tile=g["q_tile"],
                                seqs_per_step=g["seqs_per_step"]) for gi, g in enumerate(groups)]
        w_da_b, w_ga_b, w_co_b, w_o_b, *ctx_converted = attn_outs[0][2:]
        if this_gate:
            w_gate = ctx_converted.pop(0)
        post_w = [w_gate, w_da_b, w_ga_b, w_co_b, w_o_b, pre_outs[0][-1], pre_outs[1][-1]]
        next_in_b = ctx_converted
        for gi, g in enumerate(groups):
            da, ga = attn_outs[gi][:2]
            xs[gi] = _post(xs[gi], da, ga, pre_outs[gi][6], mod, l, n1, n2, fn, conv_params, post_w,
                           latent=g["latent"], seq=g["seq"], tile=g["tile"], last=(l == DEPTH - 1))
        if next_in_b:
            w_mix, w_gate = next_in_b

    ndk, ndv, ngk, ngv = new_cache
    lead = (n_ctx, DEPTH, s_ctx)
    return (xs[0].reshape(n_ctx, s_ctx, D_MODEL), xs[1].reshape(n_lat, s_lat, D_MODEL),
            ndk.reshape(lead + (DIFF_HEADS, 2, HEAD_DIM)), ndv.reshape(lead + (DIFF_HEADS, 2 * HEAD_DIM)),
            ngk.reshape(lead + (GQA_KV_HEADS, HEAD_DIM)), ngv.reshape(lead + (GQA_KV_HEADS, HEAD_DIM)))
```

```python
import functools
import math

import jax
import jax.numpy as jnp
from jax import lax
from jax.experimental import pallas as pl
from jax.experimental.pallas import tpu as pltpu

D_MODEL = 1024
DEPTH = 2
GRID_W = 64
ROPE_THETA = 10000.0
NORM_EPS = 1e-6

DIFF_HEADS = 4
HEAD_DIM = 64
DIFF_WIDTH = 512
GQA_KV_HEADS = 2
GQA_WIDTH = 512
GQA_KV_WIDTH = 128
CONV_WIDTH = 512
CONV_KSIZE = 31
CONV_HALO = 16
N_BRANCH = 3
MLP_HIDDEN = 4 * D_MODEL
N_MOD = 6

MIX_WIDTH = 3 * 512 + 512 + 2 * 128 + 2 * CONV_WIDTH
GATE_WIDTH = N_BRANCH * D_MODEL

LANES = 128
SUBLANES = 8
CONV_ROW_STRIDE = 4
MXU_WIDTH = 256
MOD_ROWS = 8
MOD_COL_TILE = 1536
TOKEN_TILE = 512
LATENT_Q_TILE = 512
MLP_CHUNK = 1024
VMEM_LIMIT = 56 * 1024 * 1024

F32 = jnp.float32
BF16 = jnp.bfloat16


def _dot(a, b):
    return jnp.dot(a, b, preferred_element_type=F32)


def _dot_nt(a, b):
    return lax.dot_general(a, b, (((1,), (1,)), ((), ())), preferred_element_type=F32)


def _rms(x, gain):
    return x * lax.rsqrt(jnp.mean(x * x, axis=-1, keepdims=True) + NORM_EPS) * gain


def _modulated_norm(x, gain, shift, scale):
    return (_rms(x, gain) * (1.0 + scale) + shift).astype(BF16)


def _lane_iota(shape):
    return lax.broadcasted_iota(jnp.int32, shape, len(shape) - 1)


def _resident(shape):
    nd = len(shape)
    return pl.BlockSpec(shape, lambda *_: (0,) * nd, pipeline_mode=pl.Buffered(1))


def _params(n_axes):
    return pltpu.CompilerParams(dimension_semantics=("arbitrary",) * n_axes,
                                vmem_limit_bytes=VMEM_LIMIT)


def _mod_kernel(c_ref, w_ref, b_ref, o_ref):
    c = c_ref[...]
    s = (c * jax.nn.sigmoid(c)).astype(BF16)
    o_ref[0] = _dot(s, w_ref[0].astype(BF16)) + b_ref[0]


def _modulation(cvecs, w_ada, b_ada):
    width = N_MOD * D_MODEL
    tn = MOD_COL_TILE
    assert width % tn == 0
    return pl.pallas_call(
        _mod_kernel,
        grid=(DEPTH, width // tn),
        in_specs=[pl.BlockSpec((MOD_ROWS, D_MODEL), lambda l, j: (0, 0)),
                  pl.BlockSpec((1, D_MODEL, tn), lambda l, j: (l, 0, j)),
                  pl.BlockSpec((1, 1, tn), lambda l, j: (l, 0, j))],
        out_specs=pl.BlockSpec((1, MOD_ROWS, tn), lambda l, j: (l, 0, j)),
        out_shape=jax.ShapeDtypeStruct((DEPTH, MOD_ROWS, width), F32),
        compiler_params=_params(2),
        name="adaln_mod",
    )(cvecs, w_ada, b_ada.reshape(DEPTH, 1, width))


def _rope(x, cos, sin_signed, first_half):
    rot = jnp.where(first_half, pltpu.roll(x, LANES - 16, 1), pltpu.roll(x, 16, 1))
    return x * cos + rot * sin_signed


def _head_mean_sq(x):
    width = min(x.shape[1], MXU_WIDTH)
    r = lax.broadcasted_iota(jnp.int32, (width, width), 0) // HEAD_DIM
    c = lax.broadcasted_iota(jnp.int32, (width, width), 1) // HEAD_DIM
    blockdiag = jnp.where(r == c, 1.0 / HEAD_DIM, 0.0).astype(BF16)
    parts = []
    for lo in range(0, x.shape[1], width):
        sq = x[:, lo:lo + width] * x[:, lo:lo + width]
        parts.append(_dot(sq.astype(BF16), blockdiag))
    return parts[0] if len(parts) == 1 else jnp.concatenate(parts, axis=1)


def _store_cache(ref, val):
    seq = ref.shape[2]
    for b in range(ref.shape[0]):
        ref[b, 0] = val[b * seq:(b + 1) * seq]


def _cast_specs(casts, n_steps, step_of):
    in_specs, args, out_specs, out_shape = [], [], [], []
    for w, w_layer, splits in casts:
        _, n_rows, n_cols = w.shape
        blk = n_rows // n_steps
        assert n_rows % n_steps == 0 and blk % 16 == 0
        in_specs.append(pl.BlockSpec((1, blk, n_cols), lambda *g, w_layer=w_layer: (w_layer, step_of(*g), 0)))
        args.append(w)
        for lo, hi in splits:
            out_specs.append(pl.BlockSpec((blk, hi - lo), lambda *g: (step_of(*g), 0)))
            out_shape.append(jax.ShapeDtypeStruct((n_rows, hi - lo), BF16))
    return in_specs, args, out_specs, out_shape, tuple(tuple(s) for _, _, s in casts)


def _run_casts(in_refs, out_refs, cast_splits):
    out_refs = iter(out_refs)
    for w_ref, splits in zip(in_refs, cast_splits):
        for lo, hi in splits:
            next(out_refs)[...] = w_ref[0, :, lo:hi].astype(BF16)


def _pre_kernel(*refs, latent, n_aliased, f32_weights, cast_splits):
    if f32_weights:
        *refs, w_bf16_ref = refs
    n_in = 8 if latent else 6
    n_out = 7 if latent else 11
    cast_in = refs[n_in + n_aliased:n_in + n_aliased + len(cast_splits)]
    outs = refs[n_in + n_aliased + len(cast_splits):]
    _run_casts(cast_in, outs[n_out:], cast_splits)
    if latent:
        x_ref, mod_ref, n1_ref, w_ref, qn_ref, kn_ref, cos_ref, sin_ref = refs[:n_in]
        qd_ref, kd_ref, vd_ref, qg_ref, kk_ref, vv_ref, u_ref = outs[:n_out]
    else:
        x_ref, mod_ref, n1_ref, w_ref, qn_ref, kn_ref = refs[:n_in]
        (qd_ref, kd_ref, vd_ref, qg_ref, kk_ref, vv_ref, u_ref,
         ndk_ref, ndv_ref, ngk_ref, ngv_ref) = outs[:n_out]

    if f32_weights:
        @pl.when(pl.program_id(0) == 0)
        def _():
            w_bf16_ref[...] = w_ref[0].astype(BF16)
        w_ref = w_bf16_ref

    mod = mod_ref[0, 0]
    h = _modulated_norm(x_ref[...], n1_ref[...], mod[:, 0:D_MODEL], mod[:, D_MODEL:2 * D_MODEL])
    rows = h.shape[0]

    lane = _lane_iota((rows, LANES))
    low_half = lane < HEAD_DIM
    if latent:
        cos, sin_signed = cos_ref[...], sin_ref[...]
        first_half = ((lane % HEAD_DIM) // 16) % 2 == 0
        rope = lambda t: _rope(t, cos, sin_signed, first_half)
    else:
        rope = lambda t: t

    qk_scale = HEAD_DIM ** -0.5 * math.log2(math.e)

    def proj(lo, width):
        return _dot(h, w_ref[:, lo:lo + width])

    def slabs(t):
        return [t[:, s:s + LANES] for s in range(0, t.shape[1], LANES)]

    def store_slabs(ref, parts):
        for s, p in enumerate(parts):
            ref[:, s * LANES:(s + 1) * LANES] = p.astype(ref.dtype)

    store_slabs(qd_ref, [rope(t) * qk_scale for t in slabs(proj(0, DIFF_WIDTH))])
    dk = proj(DIFF_WIDTH, DIFF_WIDTH)
    dv = proj(2 * DIFF_WIDTH, DIFF_WIDTH)
    if not latent:
        _store_cache(ndk_ref, dk)
        _store_cache(ndv_ref, dv)
    store_slabs(kd_ref, [rope(t) for t in slabs(dk)])
    vd_ref[...] = dv.astype(BF16)

    gq = proj(3 * DIFF_WIDTH, GQA_WIDTH)
    gq = gq * lax.rsqrt(_head_mean_sq(gq) + NORM_EPS)
    qn = qn_ref[...]
    store_slabs(qg_ref, [rope(t * qn) * qk_scale for t in slabs(gq)])

    gkv = proj(3 * DIFF_WIDTH + GQA_WIDTH, 2 * GQA_KV_WIDTH)
    k, v = gkv[:, :GQA_KV_WIDTH], gkv[:, GQA_KV_WIDTH:]
    k = k * lax.rsqrt(_head_mean_sq(k) + NORM_EPS) * kn_ref[...]
    if not latent:
        _store_cache(ngk_ref, k)
        _store_cache(ngv_ref, v)
    k = rope(k)
    k_sw, v_sw = pltpu.roll(k, HEAD_DIM, 1), pltpu.roll(v, HEAD_DIM, 1)
    store_slabs(kk_ref, [jnp.where(low_half, k, k_sw), jnp.where(low_half, k_sw, k)])
    store_slabs(vv_ref, [jnp.where(low_half, v, v_sw), jnp.where(low_half, v_sw, v)])

    cv = proj(3 * DIFF_WIDTH + GQA_WIDTH + 2 * GQA_KV_WIDTH, 2 * CONV_WIDTH)
    u_ref[...] = cv[:, :CONV_WIDTH] * jax.nn.sigmoid(cv[:, CONV_WIDTH:])


def _pre(x2d, mod, layer, norm1, w_mix, qn, kn, rope_tabs, new_cache, casts, *, latent, seq, tile):
    n_tok = x2d.shape[0]
    tiles_per_seq = seq // tile if latent else 1
    row = (lambda i: 1 + i // tiles_per_seq) if latent else (lambda i: 0)
    tok = lambda w: pl.BlockSpec((tile, w), lambda i: (i, 0))
    f32_weights = w_mix.dtype == F32
    w_spec = (pl.BlockSpec((1, D_MODEL, MIX_WIDTH), lambda i: (layer, 0, 0), pipeline_mode=pl.Buffered(1))
              if f32_weights else _resident((D_MODEL, MIX_WIDTH)))
    in_specs = [tok(D_MODEL),
                pl.BlockSpec((1, 1, 1, N_MOD * D_MODEL), lambda i: (layer, row(i), 0, 0)),
                _resident((1, D_MODEL)), w_spec,
                _resident((1, LANES)), _resident((1, LANES))]
    args = [x2d, mod, norm1, w_mix, qn, kn]
    widths = [(DIFF_WIDTH, BF16), (DIFF_WIDTH, BF16), (DIFF_WIDTH, BF16), (GQA_WIDTH, BF16),
              (GQA_KV_HEADS * LANES, BF16), (GQA_KV_HEADS * LANES, BF16), (CONV_WIDTH, F32)]
    out_specs = [tok(w) for w, _ in widths]
    out_shape = [jax.ShapeDtypeStruct((n_tok, w), dt) for w, dt in widths]
    aliases = {}
    if latent:
        in_specs += [pl.BlockSpec((tile, LANES), lambda i: (i % tiles_per_seq, 0))] * 2
        args += list(rope_tabs)
    else:
        per_tile = tile // seq
        for k, w in enumerate((DIFF_WIDTH, DIFF_WIDTH, GQA_KV_WIDTH, GQA_KV_WIDTH)):
            if new_cache is not None:
                aliases[len(args)] = len(out_shape)
                in_specs.append(pl.BlockSpec(memory_space=pl.ANY))
                args.append(new_cache[k])
            out_specs.append(pl.BlockSpec((per_tile, 1, seq, w), lambda i: (i, layer, 0, 0)))
            out_shape.append(jax.ShapeDtypeStruct((n_tok // seq, DEPTH, seq, w), F32))
    c_in, c_args, c_out, c_shape, cast_splits = _cast_specs(casts, n_tok // tile, lambda i: i)
    in_specs, args, out_specs, out_shape = in_specs + c_in, args + c_args, out_specs + c_out, out_shape + c_shape
    return pl.pallas_call(
        functools.partial(_pre_kernel, latent=latent, n_aliased=len(aliases), f32_weights=f32_weights,
                          cast_splits=cast_splits),
        grid=(n_tok // tile,),
        in_specs=in_specs,
        out_specs=out_specs,
        out_shape=out_shape,
        scratch_shapes=[pltpu.VMEM((D_MODEL, MIX_WIDTH), BF16)] if f32_weights else [],
        input_output_aliases=aliases,
        compiler_params=_params(1),
        name="pre_latent" if latent else "pre_ctx",
    )(*args)


def _attend(q, keys, values):
    scores = [_dot_nt(q, k) for k in keys]
    m = functools.reduce(jnp.maximum, [jnp.max(s, axis=-1, keepdims=True) for s in scores])
    return functools.reduce(jnp.add, [_dot(jnp.exp2(s - m).astype(BF16), v) for s, v in zip(scores, values)])


def _attend_rowsum(q, keys, values):
    scores = [_dot_nt(q, k) for k in keys]
    m = functools.reduce(jnp.maximum, [jnp.max(s, axis=-1, keepdims=True) for s in scores])
    exps = [jnp.exp2(s - m) for s in scores]
    denom = functools.reduce(jnp.add, [jnp.sum(e, axis=-1, keepdims=True) for e in exps])
    out = functools.reduce(jnp.add, [_dot(e.astype(BF16), v) for e, v in zip(exps, values)])
    return out / denom


def _attn_kernel(*refs, latent, lam_init, cast_splits, n_seqs):
    n_core = 12 if latent else 8
    if latent:
        (qd_ref, kd_ref, vd_ref, qg_ref, kk_ref, vv_ref, cdk_ref, cdv_ref, cgk_ref, cgv_ref,
         lp_ref, sub_ref) = refs[:n_core]
    else:
        qd_ref, kd_ref, vd_ref, qg_ref, kk_ref, vv_ref, lp_ref, sub_ref = refs[:n_core]
    n_in = n_core + len(cast_splits)
    da_ref, ga_ref = refs[n_in:n_in + 2]
    _run_casts(refs[n_core:n_in], refs[n_in + 2:], cast_splits)

    rows = qd_ref.shape[0] // n_seqs
    kv_rows = kd_ref.shape[0] // n_seqs
    low_half = _lane_iota((rows, LANES)) < HEAD_DIM
    zero = jnp.zeros((), BF16)

    lp = lp_ref[...]
    lam = (jnp.exp(jnp.sum(lp[0:1] * lp[1:2], axis=-1, keepdims=True))
           - jnp.exp(jnp.sum(lp[2:3] * lp[3:4], axis=-1, keepdims=True)) + lam_init)
    sub_gain = sub_ref[...] * (1.0 - lam_init)

    def with_ones(v):
        return jnp.concatenate([v, jnp.ones_like(v)], axis=1)

    def value_slabs(v_dup):
        low = _lane_iota(v_dup.shape) < HEAD_DIM
        one = jnp.ones((), v_dup.dtype)
        return jnp.where(low, v_dup, one), jnp.where(low, one, v_dup)

    if latent:
        ck, cv = cgk_ref[0, 0], cgv_ref[0, 0]
        ck_sw, cv_sw = pltpu.roll(ck, HEAD_DIM, 1), pltpu.roll(cv, HEAD_DIM, 1)
        low_c = _lane_iota(ck.shape) < HEAD_DIM
        cache_k = [jnp.where(low_c, ck, ck_sw).astype(BF16), jnp.where(low_c, ck_sw, ck).astype(BF16)]
        cache_v = [value_slabs(jnp.where(low_c, cv, cv_sw).astype(BF16)),
                   value_slabs(jnp.where(low_c, cv_sw, cv).astype(BF16))]
    for s in range(n_seqs):
        qs = slice(s * rows, (s + 1) * rows)
        ks = slice(s * kv_rows, (s + 1) * kv_rows)
        for h in range(DIFF_HEADS):
            sl = slice(h * LANES, (h + 1) * LANES)
            q = qd_ref[qs, sl]
            q1, q2 = jnp.where(low_half, q, zero), jnp.where(low_half, zero, q)
            if latent:
                keys = [kd_ref[ks, sl], cdk_ref[0, 0, :, sl].astype(BF16)]
                values = [with_ones(vd_ref[ks, sl]), with_ones(cdv_ref[0, 0, :, sl].astype(BF16))]
                r1 = _attend(q1, keys, values)
                r2 = _attend(q2, keys, values)
                o = r1[:, :LANES] / r1[:, LANES:] - lam * (r2[:, :LANES] / r2[:, LANES:])
            else:
                keys, values = [kd_ref[ks, sl]], [vd_ref[ks, sl]]
                o = _attend_rowsum(q1, keys, values) - lam * _attend_rowsum(q2, keys, values)
            da_ref[qs, sl] = _rms(o, sub_gain).astype(BF16)

        for n in range(GQA_KV_HEADS):
            kv_sl = slice(n * LANES, (n + 1) * LANES)
            keys = [kk_ref[ks, kv_sl]]
            if latent:
                keys.append(cache_k[n])
                values = [value_slabs(vv_ref[ks, kv_sl]), cache_v[n]]
            for j in range(2):
                sl = slice((2 * n + j) * LANES, (2 * n + j + 1) * LANES)
                q = qg_ref[qs, sl]
                q_even, q_odd = jnp.where(low_half, q, zero), jnp.where(low_half, zero, q)
                if latent:
                    r_even = _attend(q_even, keys, [v[0] for v in values])
                    r_odd = _attend(q_odd, keys, [v[1] for v in values])
                    r = jnp.where(low_half, r_even, r_odd)
                    denom = jnp.where(low_half, pltpu.roll(r_even, HEAD_DIM, 1), pltpu.roll(r_odd, HEAD_DIM, 1))
                    out = r / denom
                else:
                    v_dup = [vv_ref[ks, kv_sl]]
                    out = jnp.where(low_half, _attend_rowsum(q_even, keys, v_dup), _attend_rowsum(q_odd, keys, v_dup))
                ga_ref[qs, sl] = out.astype(BF16)


def _attention(pre_outs, caches, layer, lam_params, subln, casts, *, latent, n_seq, seq, q_tile, seqs_per_step):
    qd, kd, vd, qg, kk, vv = pre_outs
    tiles = seq // q_tile
    assert seqs_per_step == 1 or tiles == 1
    n_seq //= seqs_per_step
    n_steps = n_seq * tiles
    q_spec = lambda w: pl.BlockSpec((seqs_per_step * q_tile, w), lambda b, i: (b * tiles + i, 0))
    kv_spec = lambda w: pl.BlockSpec((seqs_per_step * seq, w), lambda b, i: (b, 0))
    kv_dup = GQA_KV_HEADS * LANES
    in_specs = [q_spec(DIFF_WIDTH), kv_spec(DIFF_WIDTH), kv_spec(DIFF_WIDTH), q_spec(GQA_WIDTH),
                kv_spec(kv_dup), kv_spec(kv_dup)]
    args = [qd, kd, vd, qg, kk, vv]
    if latent:
        for c in caches:
            in_specs.append(pl.BlockSpec((1, 1) + c.shape[2:], lambda b, i: (b, layer, 0, 0)))
            args.append(c)
    in_specs += [_resident((4, HEAD_DIM)), _resident((1, LANES))]
    args += [lam_params, subln]
    out_specs = [q_spec(DIFF_WIDTH), q_spec(GQA_WIDTH)]
    out_shape = [jax.ShapeDtypeStruct(qd.shape, BF16)] * 2
    c_in, c_args, c_out, c_shape, cast_splits = _cast_specs(casts, n_steps, lambda b, i: b * tiles + i)
    in_specs, args, out_specs, out_shape = in_specs + c_in, args + c_args, out_specs + c_out, out_shape + c_shape
    lam_init = 0.8 - 0.6 * math.exp(-0.3 * layer)
    return pl.pallas_call(
        functools.partial(_attn_kernel, latent=latent, lam_init=lam_init, cast_splits=cast_splits,
                          n_seqs=seqs_per_step),
        grid=(n_seq, tiles),
        in_specs=in_specs,
        out_specs=out_specs,
        out_shape=out_shape,
        compiler_params=_params(2),
        name="attn_latent" if latent else "attn_ctx",
    )(*args)


def _conv_branch(u_ref, head, tail, w_ref, bias, gain, beta, win_ref, y_ref, seg):
    n_seg = u_ref.shape[0] // seg
    pitch = seg + 2 * CONV_HALO
    lane_slabs = [slice(s * LANES, (s + 1) * LANES) for s in range(CONV_WIDTH // LANES)]
    zeros = jnp.zeros((CONV_HALO, LANES), F32)
    for k in range(n_seg):
        for s, ls in enumerate(lane_slabs):
            win_ref[s, k * pitch:k * pitch + CONV_HALO, :] = head[:, ls] if k == 0 else zeros
            win_ref[s, k * pitch + CONV_HALO:(k + 1) * pitch - CONV_HALO, :] = u_ref[k * seg:(k + 1) * seg, ls]
            win_ref[s, (k + 1) * pitch - CONV_HALO:(k + 1) * pitch, :] = tail[:, ls] if k == n_seg - 1 else zeros

    first = CONV_HALO - CONV_KSIZE // 2
    group = 16
    rows_per_group = group * SUBLANES
    for k in range(n_seg):
        for s, ls in enumerate(lane_slabs):
            for r0 in range(0, seg, rows_per_group):
                offs = [r + t for r in range(r0, r0 + rows_per_group, SUBLANES * CONV_ROW_STRIDE)
                        for t in range(CONV_ROW_STRIDE)]
                accs = [jnp.zeros((SUBLANES, LANES), F32)] * group
                for j in range(CONV_KSIZE):
                    w_tap = jnp.broadcast_to(w_ref[j:j + 1, ls], (SUBLANES, LANES))
                    for a, off in enumerate(offs):
                        start = k * pitch + off + first + j
                        accs[a] = accs[a] + win_ref[s, pl.ds(start, SUBLANES, stride=CONV_ROW_STRIDE), :] * w_tap
                for a, off in enumerate(offs):
                    y_ref[s, pl.ds(k * seg + off, SUBLANES, stride=CONV_ROW_STRIDE), :] = accs[a]
    acc = jnp.concatenate([y_ref[s] for s in range(len(lane_slabs))], axis=1) + bias
    mu = jnp.mean(acc, axis=-1, keepdims=True)
    xc = acc - mu
    y = xc * lax.rsqrt(jnp.mean(xc * xc, axis=-1, keepdims=True) + NORM_EPS) * gain + beta
    return y * jax.nn.sigmoid(y)


def _post_kernel(*refs, last, seg, tiles_per_seq):
    halo = tiles_per_seq > 1
    if halo:
        x_ref, da_ref, ga_ref, u_ref, uprev_ref, unext_ref = refs[:6]
        refs = refs[6:]
    else:
        x_ref, da_ref, ga_ref, u_ref = refs[:4]
        refs = refs[4:]
    (mod_ref, n1_ref, n2_ref, fn_ref, cw_ref, cb_ref, cg_ref, cbeta_ref,
     wg_ref, wda_ref, wga_ref, wco_ref, wo_ref, w1_ref, w2_ref, o_ref, win_ref, y_ref) = refs

    if halo:
        pos = pl.program_id(0) % tiles_per_seq
        head = jnp.where(pos > 0, uprev_ref[...], 0.0)
        tail = jnp.where(pos < tiles_per_seq - 1, unext_ref[...], 0.0)
    else:
        head = tail = jnp.zeros((CONV_HALO, CONV_WIDTH), F32)
    ca = _conv_branch(u_ref, head, tail, cw_ref, cb_ref[...], cg_ref[...], cbeta_ref[...], win_ref, y_ref, seg)

    x = x_ref[...]
    mod = mod_ref[0, 0]
    m = lambda k: mod[:, k * D_MODEL:(k + 1) * D_MODEL]
    h = _modulated_norm(x, n1_ref[...], m(0), m(1))

    branches = ((da_ref[...], wda_ref), (ga_ref[...], wga_ref), (ca.astype(BF16), wco_ref))
    merged = None
    for j, (act, w_ref) in enumerate(branches):
        gate = jax.nn.sigmoid(_dot(h, wg_ref[:, j * D_MODEL:(j + 1) * D_MODEL]))
        term = gate * _dot(act, w_ref[...])
        merged = term if merged is None else merged + term
    x = x + m(2) * _dot(merged.astype(BF16), wo_ref[...])

    h2 = _modulated_norm(x, n2_ref[...], m(3), m(4))
    f = None
    for c in range(0, MLP_HIDDEN, MLP_CHUNK):
        a = jnp.maximum(_dot(h2, w1_ref[:, c:c + MLP_CHUNK]), 0.0)
        term = _dot((a * a).astype(BF16), w2_ref[c:c + MLP_CHUNK, :])
        f = term if f is None else f + term
    x = x + m(5) * f
    o_ref[...] = _rms(x, fn_ref[...]) if last else x


def _post(x2d, da, ga, u, mod, layer, norm1, norm2, final_norm, conv_params, weights, *, latent, seq, tile, last):
    n_tok = x2d.shape[0]
    assert seq % tile == 0 or tile % seq == 0
    tiles_per_seq = max(seq // tile, 1)
    seg = min(seq, tile)
    row = (lambda i: 1 + i // tiles_per_seq) if latent else (lambda i: 0)
    tok = lambda w: pl.BlockSpec((tile, w), lambda i: (i, 0))
    in_specs = [tok(D_MODEL), tok(DIFF_WIDTH), tok(GQA_WIDTH), tok(CONV_WIDTH)]
    args = [x2d, da, ga, u]
    if tiles_per_seq > 1:
        per_tile, n_halo = tile // CONV_HALO, n_tok // CONV_HALO
        in_specs += [pl.BlockSpec((CONV_HALO, CONV_WIDTH), lambda i: (jnp.maximum(i * per_tile - 1, 0), 0)),
                     pl.BlockSpec((CONV_HALO, CONV_WIDTH),
                                  lambda i: (jnp.minimum((i + 1) * per_tile, n_halo - 1), 0))]
        args += [u, u]
    in_specs += [pl.BlockSpec((1, 1, 1, N_MOD * D_MODEL), lambda i: (layer, row(i), 0, 0)),
                 _resident((1, D_MODEL)), _resident((1, D_MODEL)), _resident((1, D_MODEL))]
    in_specs += [_resident(p.shape) for p in conv_params] + [_resident(w.shape) for w in weights]
    n_slab = CONV_WIDTH // LANES
    return pl.pallas_call(
        functools.partial(_post_kernel, last=last, seg=seg, tiles_per_seq=tiles_per_seq),
        grid=(n_tok // tile,),
        in_specs=in_specs,
        out_specs=tok(D_MODEL),
        out_shape=jax.ShapeDtypeStruct((n_tok, D_MODEL), F32),
        scratch_shapes=[pltpu.VMEM((n_slab, (tile // seg) * (seg + 2 * CONV_HALO), LANES), F32),
                        pltpu.VMEM((n_slab, tile, LANES), F32)],
        compiler_params=_params(1),
        name="post_latent" if latent else "post_ctx",
    )(*args, mod, norm1, norm2, final_norm, *conv_params, *weights)


def _rope_tables(n_tokens):
    n_rows = n_tokens // GRID_W
    row = jnp.repeat(jnp.arange(n_rows), GRID_W).astype(F32)
    col = jnp.tile(jnp.arange(GRID_W), n_rows).astype(F32)
    axis_dim = HEAD_DIM // 2
    freqs = ROPE_THETA ** (-jnp.arange(0, axis_dim, 2, dtype=F32) / axis_dim)
    ang_r = row[:, None] * freqs[None, :]
    ang_c = col[:, None] * freqs[None, :]
    ang = jnp.concatenate([ang_r, ang_r, ang_c, ang_c], axis=-1)
    sign = jnp.tile(jnp.repeat(jnp.array([-1.0, 1.0], F32), HEAD_DIM // 4), 2)
    return jnp.tile(jnp.cos(ang), (1, 2)), jnp.tile(jnp.sin(ang) * sign, (1, 2))


def kernel(x_prompt, x_sample, cache_diff_k, cache_diff_v, cache_gqa_k, cache_gqa_v, c, c_ctx, w_ada, b_ada, norm1, norm2, w_in, diff_lq1, diff_lk1, diff_lq2, diff_lk2, diff_subln, w_diff_o, gqa_q_norm, gqa_k_norm, w_gqa_o, conv_dw, conv_dw_b, conv_ln_g, conv_ln_b, w_conv_o, w_o, w_mlp1, w_mlp2, final_norm):
    n_ctx, s_ctx, _ = x_prompt.shape
    n_lat, s_lat, _ = x_sample.shape
    past = cache_diff_k.shape[2]
    assert n_lat + 1 <= MOD_ROWS

    cvecs = jnp.concatenate([c_ctx[None], c, jnp.zeros((MOD_ROWS - 1 - n_lat, D_MODEL), F32)], axis=0)
    mod = _modulation(cvecs, w_ada, b_ada).reshape(DEPTH, MOD_ROWS, 1, N_MOD * D_MODEL)

    caches = (cache_diff_k.reshape(n_lat, DEPTH, past, DIFF_WIDTH), cache_diff_v.reshape(n_lat, DEPTH, past, DIFF_WIDTH),
              cache_gqa_k.reshape(n_lat, DEPTH, past, GQA_KV_WIDTH),
              cache_gqa_v.reshape(n_lat, DEPTH, past, GQA_KV_WIDTH))
    rope_tabs = _rope_tables(s_lat)
    row_vec = lambda p: p.reshape(1, -1)
    fn = row_vec(final_norm)

    groups = (dict(latent=False, n_seq=n_ctx, seq=s_ctx, tile=TOKEN_TILE, q_tile=s_ctx, seqs_per_step=1),
              dict(latent=True, n_seq=n_lat, seq=s_lat, tile=TOKEN_TILE, q_tile=LATENT_Q_TILE, seqs_per_step=1))
    xs = [x_prompt.reshape(n_ctx * s_ctx, D_MODEL), x_sample.reshape(n_lat * s_lat, D_MODEL)]
    new_cache = None

    mix_cols, gate_cols = (0, MIX_WIDTH), (MIX_WIDTH, MIX_WIDTH + GATE_WIDTH)
    whole = lambda w: ((0, w.shape[2]),)
    w_mix, w_gate = w_in, None

    for l in range(DEPTH):
        qn = row_vec(jnp.tile(gqa_q_norm[l], 2))
        kn = row_vec(jnp.tile(gqa_k_norm[l], 2))
        lam_params = jnp.stack([diff_lq1[l], diff_lk1[l], diff_lq2[l], diff_lk2[l]])
        n1, n2 = row_vec(norm1[l]), row_vec(norm2[l])
        conv_params = [conv_dw[l], row_vec(conv_dw_b[l]), row_vec(conv_ln_g[l]), row_vec(conv_ln_b[l])]
        next_in = [(w_in, l + 1, (mix_cols, gate_cols))] if l + 1 < DEPTH else []
        this_gate = [(w_in, l, (gate_cols,))] if w_gate is None else []
        pre_casts = ([(w_mlp1, l, whole(w_mlp1))], [(w_mlp2, l, whole(w_mlp2))])
        attn_casts = ([(w, l, whole(w)) for w in (w_diff_o, w_gqa_o, w_conv_o, w_o)] + this_gate + next_in, [])

        pre_outs = []
        for gi, g in enumerate(groups):
            outs = _pre(xs[gi], mod, l, n1, w_mix, qn, kn, rope_tabs, new_cache, pre_casts[gi],
                        latent=g["latent"], seq=g["seq"], tile=g["tile"])
            if not g["latent"]:
                new_cache = outs[7:11]
            pre_outs.append(outs)
        attn_outs = [_attention(pre_outs[gi][:6], caches, l, lam_params, row_vec(diff_subln[l]), attn_casts[gi],
                                latent=g["latent"], n_seq=g["n_seq"], seq=g["seq"], q_tile=g["q_tile"],
                                seqs_per_step=g["seqs_per_step"]) for gi, g in enumerate(groups)]
        w_da_b, w_ga_b, w_co_b, w_o_b, *ctx_converted = attn_outs[0][2:]
        if this_gate:
            w_gate = ctx_converted.pop(0)
        post_w = [w_gate, w_da_b, w_ga_b, w_co_b, w_o_b, pre_outs[0][-1], pre_outs[1][-1]]
        next_in_b = ctx_converted
        for gi, g in enumerate(groups):
            da, ga = attn_outs[gi][:2]
            xs[gi] = _post(xs[gi], da, ga, pre_outs[gi][6], mod, l, n1, n2, fn, conv_params, post_w,
                           latent=g["latent"], seq=g["seq"], tile=g["tile"], last=(l == DEPTH - 1))
        if next_in_b:
            w_mix, w_gate = next_in_b

    ndk, ndv, ngk, ngv = new_cache
    lead = (n_ctx, DEPTH, s_ctx)
    return (xs[0].reshape(n_ctx, s_ctx, D_MODEL), xs[1].reshape(n_lat, s_lat, D_MODEL),
            ndk.reshape(lead + (DIFF_HEADS, 2, HEAD_DIM)), ndv.reshape(lead + (DIFF_HEADS, 2 * HEAD_DIM)),
            ngk.reshape(lead + (GQA_KV_HEADS, HEAD_DIM)), ngv.reshape(lead + (GQA_KV_HEADS, HEAD_DIM)))
```

```python
import functools
import math

import jax
import jax.numpy as jnp
from jax import lax
from jax.experimental import pallas as pl
from jax.experimental.pallas import tpu as pltpu

D_MODEL = 1024
DEPTH = 2
GRID_W = 64
ROPE_THETA = 10000.0
NORM_EPS = 1e-6

DIFF_HEADS = 4
HEAD_DIM = 64
DIFF_WIDTH = 512
GQA_KV_HEADS = 2
GQA_WIDTH = 512
GQA_KV_WIDTH = 128
CONV_WIDTH = 512
CONV_KSIZE = 31
CONV_HALO = 16
N_BRANCH = 3
MLP_HIDDEN = 4 * D_MODEL
N_MOD = 6

MIX_WIDTH = 3 * 512 + 512 + 2 * 128 + 2 * CONV_WIDTH
GATE_WIDTH = N_BRANCH * D_MODEL

LANES = 128
SUBLANES = 8
CONV_ROW_STRIDE = 4
MXU_WIDTH = 256
MOD_ROWS = 8
MOD_COL_TILE = 1536
TOKEN_TILE = 512
LATENT_Q_TILE = 512
MLP_CHUNK = 1024
VMEM_LIMIT = 61 * 1024 * 1024

F32 = jnp.float32
BF16 = jnp.bfloat16


def _dot(a, b):
    return jnp.dot(a, b, preferred_element_type=F32)


def _dot_nt(a, b):
    return lax.dot_general(a, b, (((1,), (1,)), ((), ())), preferred_element_type=F32)


def _rms(x, gain):
    return x * lax.rsqrt(jnp.mean(x * x, axis=-1, keepdims=True) + NORM_EPS) * gain


def _modulated_norm(x, gain, shift, scale):
    return (_rms(x, gain) * (1.0 + scale) + shift).astype(BF16)


def _lane_iota(shape):
    return lax.broadcasted_iota(jnp.int32, shape, len(shape) - 1)


def _resident(shape):
    nd = len(shape)
    return pl.BlockSpec(shape, lambda *_: (0,) * nd, pipeline_mode=pl.Buffered(1))


def _params(n_axes):
    return pltpu.CompilerParams(dimension_semantics=("arbitrary",) * n_axes,
                                vmem_limit_bytes=VMEM_LIMIT)


def _mod_kernel(c_ref, w_ref, b_ref, o_ref):
    c = c_ref[...]
    s = (c * jax.nn.sigmoid(c)).astype(BF16)
    o_ref[0] = _dot(s, w_ref[0].astype(BF16)) + b_ref[0]


def _modulation(cvecs, w_ada, b_ada):
    width = N_MOD * D_MODEL
    tn = MOD_COL_TILE
    assert width % tn == 0
    return pl.pallas_call(
        _mod_kernel,
        grid=(DEPTH, width // tn),
        in_specs=[pl.BlockSpec((MOD_ROWS, D_MODEL), lambda l, j: (0, 0)),
                  pl.BlockSpec((1, D_MODEL, tn), lambda l, j: (l, 0, j)),
                  pl.BlockSpec((1, 1, tn), lambda l, j: (l, 0, j))],
        out_specs=pl.BlockSpec((1, MOD_ROWS, tn), lambda l, j: (l, 0, j)),
        out_shape=jax.ShapeDtypeStruct((DEPTH, MOD_ROWS, width), F32),
        compiler_params=_params(2),
        name="adaln_mod",
    )(cvecs, w_ada, b_ada.reshape(DEPTH, 1, width))


def _rope(x, cos, sin_signed, first_half):
    rot = jnp.where(first_half, pltpu.roll(x, LANES - 16, 1), pltpu.roll(x, 16, 1))
    return x * cos + rot * sin_signed


def _head_mean_sq(x):
    width = min(x.shape[1], MXU_WIDTH)
    r = lax.broadcasted_iota(jnp.int32, (width, width), 0) // HEAD_DIM
    c = lax.broadcasted_iota(jnp.int32, (width, width), 1) // HEAD_DIM
    blockdiag = jnp.where(r == c, 1.0 / HEAD_DIM, 0.0).astype(BF16)
    parts = []
    for lo in range(0, x.shape[1], width):
        sq = x[:, lo:lo + width] * x[:, lo:lo + width]
        parts.append(_dot(sq.astype(BF16), blockdiag))
    return parts[0] if len(parts) == 1 else jnp.concatenate(parts, axis=1)


def _store_cache(ref, val):
    seq = ref.shape[2]
    for b in range(ref.shape[0]):
        ref[b, 0] = val[b * seq:(b + 1) * seq]


def _cast_specs(casts, n_steps, step_of):
    in_specs, args, out_specs, out_shape = [], [], [], []
    for w, w_layer, splits in casts:
        _, n_rows, n_cols = w.shape
        blk = n_rows // n_steps
        assert n_rows % n_steps == 0 and blk % 16 == 0
        in_specs.append(pl.BlockSpec((1, blk, n_cols), lambda *g, w_layer=w_layer: (w_layer, step_of(*g), 0)))
        args.append(w)
        for lo, hi in splits:
            out_specs.append(pl.BlockSpec((blk, hi - lo), lambda *g: (step_of(*g), 0)))
            out_shape.append(jax.ShapeDtypeStruct((n_rows, hi - lo), BF16))
    return in_specs, args, out_specs, out_shape, tuple(tuple(s) for _, _, s in casts)


def _run_casts(in_refs, out_refs, cast_splits):
    out_refs = iter(out_refs)
    for w_ref, splits in zip(in_refs, cast_splits):
        for lo, hi in splits:
            next(out_refs)[...] = w_ref[0, :, lo:hi].astype(BF16)


def _pre_kernel(*refs, latent, n_aliased, f32_weights, cast_splits):
    if f32_weights:
        *refs, w_bf16_ref = refs
    n_in = 8 if latent else 6
    n_out = 7 if latent else 11
    cast_in = refs[n_in + n_aliased:n_in + n_aliased + len(cast_splits)]
    outs = refs[n_in + n_aliased + len(cast_splits):]
    _run_casts(cast_in, outs[n_out:], cast_splits)
    if latent:
        x_ref, mod_ref, n1_ref, w_ref, qn_ref, kn_ref, cos_ref, sin_ref = refs[:n_in]
        qd_ref, kd_ref, vd_ref, qg_ref, kk_ref, vv_ref, u_ref = outs[:n_out]
    else:
        x_ref, mod_ref, n1_ref, w_ref, qn_ref, kn_ref = refs[:n_in]
        (qd_ref, kd_ref, vd_ref, qg_ref, kk_ref, vv_ref, u_ref,
         ndk_ref, ndv_ref, ngk_ref, ngv_ref) = outs[:n_out]

    if f32_weights:
        @pl.when(pl.program_id(0) == 0)
        def _():
            w_bf16_ref[...] = w_ref[0].astype(BF16)
        w_ref = w_bf16_ref

    mod = mod_ref[0, 0]
    h = _modulated_norm(x_ref[...], n1_ref[...], mod[:, 0:D_MODEL], mod[:, D_MODEL:2 * D_MODEL])
    rows = h.shape[0]

    lane = _lane_iota((rows, LANES))
    low_half = lane < HEAD_DIM
    if latent:
        cos, sin_signed = cos_ref[...], sin_ref[...]
        first_half = ((lane % HEAD_DIM) // 16) % 2 == 0
        rope = lambda t: _rope(t, cos, sin_signed, first_half)
    else:
        rope = lambda t: t

    qk_scale = HEAD_DIM ** -0.5 * math.log2(math.e)

    def proj(lo, width):
        return _dot(h, w_ref[:, lo:lo + width])

    def slabs(t):
        return [t[:, s:s + LANES] for s in range(0, t.shape[1], LANES)]

    def store_slabs(ref, parts):
        for s, p in enumerate(parts):
            ref[:, s * LANES:(s + 1) * LANES] = p.astype(ref.dtype)

    store_slabs(qd_ref, [rope(t) * qk_scale for t in slabs(proj(0, DIFF_WIDTH))])
    dk = proj(DIFF_WIDTH, DIFF_WIDTH)
    dv = proj(2 * DIFF_WIDTH, DIFF_WIDTH)
    if not latent:
        _store_cache(ndk_ref, dk)
        _store_cache(ndv_ref, dv)
    store_slabs(kd_ref, [rope(t) for t in slabs(dk)])
    vd_ref[...] = dv.astype(BF16)

    gq = proj(3 * DIFF_WIDTH, GQA_WIDTH)
    gq = gq * lax.rsqrt(_head_mean_sq(gq) + NORM_EPS)
    qn = qn_ref[...]
    store_slabs(qg_ref, [rope(t * qn) * qk_scale for t in slabs(gq)])

    gkv = proj(3 * DIFF_WIDTH + GQA_WIDTH, 2 * GQA_KV_WIDTH)
    k, v = gkv[:, :GQA_KV_WIDTH], gkv[:, GQA_KV_WIDTH:]
    k = k * lax.rsqrt(_head_mean_sq(k) + NORM_EPS) * kn_ref[...]
    if not latent:
        _store_cache(ngk_ref, k)
        _store_cache(ngv_ref, v)
    k = rope(k)
    k_sw, v_sw = pltpu.roll(k, HEAD_DIM, 1), pltpu.roll(v, HEAD_DIM, 1)
    store_slabs(kk_ref, [jnp.where(low_half, k, k_sw), jnp.where(low_half, k_sw, k)])
    store_slabs(vv_ref, [jnp.where(low_half, v, v_sw), jnp.where(low_half, v_sw, v)])

    cv = proj(3 * DIFF_WIDTH + GQA_WIDTH + 2 * GQA_KV_WIDTH, 2 * CONV_WIDTH)
    u_ref[...] = cv[:, :CONV_WIDTH] * jax.nn.sigmoid(cv[:, CONV_WIDTH:])


def _pre(x2d, mod, layer, norm1, w_mix, qn, kn, rope_tabs, new_cache, casts, *, latent, seq, tile):
    n_tok = x2d.shape[0]
    tiles_per_seq = seq // tile if latent else 1
    row = (lambda i: 1 + i // tiles_per_seq) if latent else (lambda i: 0)
    tok = lambda w: pl.BlockSpec((tile, w), lambda i: (i, 0))
    f32_weights = w_mix.dtype == F32
    w_spec = (pl.BlockSpec((1, D_MODEL, MIX_WIDTH), lambda i: (layer, 0, 0), pipeline_mode=pl.Buffered(1))
              if f32_weights else _resident((D_MODEL, MIX_WIDTH)))
    in_specs = [tok(D_MODEL),
                pl.BlockSpec((1, 1, 1, N_MOD * D_MODEL), lambda i: (layer, row(i), 0, 0)),
                _resident((1, D_MODEL)), w_spec,
                _resident((1, LANES)), _resident((1, LANES))]
    args = [x2d, mod, norm1, w_mix, qn, kn]
    widths = [(DIFF_WIDTH, BF16), (DIFF_WIDTH, BF16), (DIFF_WIDTH, BF16), (GQA_WIDTH, BF16),
              (GQA_KV_HEADS * LANES, BF16), (GQA_KV_HEADS * LANES, BF16), (CONV_WIDTH, F32)]
    out_specs = [tok(w) for w, _ in widths]
    out_shape = [jax.ShapeDtypeStruct((n_tok, w), dt) for w, dt in widths]
    aliases = {}
    if latent:
        in_specs += [pl.BlockSpec((tile, LANES), lambda i: (i % tiles_per_seq, 0))] * 2
        args += list(rope_tabs)
    else:
        per_tile = tile // seq
        for k, w in enumerate((DIFF_WIDTH, DIFF_WIDTH, GQA_KV_WIDTH, GQA_KV_WIDTH)):
            if new_cache is not None:
                aliases[len(args)] = len(out_shape)
                in_specs.append(pl.BlockSpec(memory_space=pl.ANY))
                args.append(new_cache[k])
            out_specs.append(pl.BlockSpec((per_tile, 1, seq, w), lambda i: (i, layer, 0, 0)))
            out_shape.append(jax.ShapeDtypeStruct((n_tok // seq, DEPTH, seq, w), F32))
    c_in, c_args, c_out, c_shape, cast_splits = _cast_specs(casts, n_tok // tile, lambda i: i)
    in_specs, args, out_specs, out_shape = in_specs + c_in, args + c_args, out_specs + c_out, out_shape + c_shape
    return pl.pallas_call(
        functools.partial(_pre_kernel, latent=latent, n_aliased=len(aliases), f32_weights=f32_weights,
                          cast_splits=cast_splits),
        grid=(n_tok // tile,),
        in_specs=in_specs,
        out_specs=out_specs,
        out_shape=out_shape,
        scratch_shapes=[pltpu.VMEM((D_MODEL, MIX_WIDTH), BF16)] if f32_weights else [],
        input_output_aliases=aliases,
        compiler_params=_params(1),
        name="pre_latent" if latent else "pre_ctx",
    )(*args)


def _attend(q, keys, values):
    scores = [_dot_nt(q, k) for k in keys]
    m = functools.reduce(jnp.maximum, [jnp.max(s, axis=-1, keepdims=True) for s in scores])
    return functools.reduce(jnp.add, [_dot(jnp.exp2(s - m).astype(BF16), v) for s, v in zip(scores, values)])


def _attend_rowsum(q, keys, values):
    scores = [_dot_nt(q, k) for k in keys]
    m = functools.reduce(jnp.maximum, [jnp.max(s, axis=-1, keepdims=True) for s in scores])
    exps = [jnp.exp2(s - m) for s in scores]
    denom = functools.reduce(jnp.add, [jnp.sum(e, axis=-1, keepdims=True) for e in exps])
    out = functools.reduce(jnp.add, [_dot(e.astype(BF16), v) for e, v in zip(exps, values)])
    return out / denom


def _attn_kernel(*refs, latent, lam_init, cast_splits, n_seqs):
    n_core = 12 if latent else 8
    if latent:
        (qd_ref, kd_ref, vd_ref, qg_ref, kk_ref, vv_ref, cdk_ref, cdv_ref, cgk_ref, cgv_ref,
         lp_ref, sub_ref) = refs[:n_core]
    else:
        qd_ref, kd_ref, vd_ref, qg_ref, kk_ref, vv_ref, lp_ref, sub_ref = refs[:n_core]
    n_in = n_core + len(cast_splits)
    da_ref, ga_ref = refs[n_in:n_in + 2]
    _run_casts(refs[n_core:n_in], refs[n_in + 2:], cast_splits)

    rows = qd_ref.shape[0] // n_seqs
    kv_rows = kd_ref.shape[0] // n_seqs
    low_half = _lane_iota((rows, LANES)) < HEAD_DIM
    zero = jnp.zeros((), BF16)

    lp = lp_ref[...]
    lam = (jnp.exp(jnp.sum(lp[0:1] * lp[1:2], axis=-1, keepdims=True))
           - jnp.exp(jnp.sum(lp[2:3] * lp[3:4], axis=-1, keepdims=True)) + lam_init)
    sub_gain = sub_ref[...] * (1.0 - lam_init)

    def with_ones(v):
        return jnp.concatenate([v, jnp.ones_like(v)], axis=1)

    def value_slabs(v_dup):
        low = _lane_iota(v_dup.shape) < HEAD_DIM
        one = jnp.ones((), v_dup.dtype)
        return jnp.where(low, v_dup, one), jnp.where(low, one, v_dup)

    if latent:
        ck, cv = cgk_ref[0, 0], cgv_ref[0, 0]
        ck_sw, cv_sw = pltpu.roll(ck, HEAD_DIM, 1), pltpu.roll(cv, HEAD_DIM, 1)
        low_c = _lane_iota(ck.shape) < HEAD_DIM
        cache_k = [jnp.where(low_c, ck, ck_sw).astype(BF16), jnp.where(low_c, ck_sw, ck).astype(BF16)]
        cache_v = [value_slabs(jnp.where(low_c, cv, cv_sw).astype(BF16)),
                   value_slabs(jnp.where(low_c, cv_sw, cv).astype(BF16))]
    for s in range(n_seqs):
        qs = slice(s * rows, (s + 1) * rows)
        ks = slice(s * kv_rows, (s + 1) * kv_rows)
        for h in range(DIFF_HEADS):
            sl = slice(h * LANES, (h + 1) * LANES)
            q = qd_ref[qs, sl]
            q1, q2 = jnp.where(low_half, q, zero), jnp.where(low_half, zero, q)
            if latent:
                keys = [kd_ref[ks, sl], cdk_ref[0, 0, :, sl].astype(BF16)]
                values = [with_ones(vd_ref[ks, sl]), with_ones(cdv_ref[0, 0, :, sl].astype(BF16))]
                r1 = _attend(q1, keys, values)
                r2 = _attend(q2, keys, values)
                o = r1[:, :LANES] / r1[:, LANES:] - lam * (r2[:, :LANES] / r2[:, LANES:])
            else:
                keys, values = [kd_ref[ks, sl]], [vd_ref[ks, sl]]
                o = _attend_rowsum(q1, keys, values) - lam * _attend_rowsum(q2, keys, values)
            da_ref[qs, sl] = _rms(o, sub_gain).astype(BF16)

        for n in range(GQA_KV_HEADS):
            kv_sl = slice(n * LANES, (n + 1) * LANES)
            keys = [kk_ref[ks, kv_sl]]
            if latent:
                keys.append(cache_k[n])
                values = [value_slabs(vv_ref[ks, kv_sl]), cache_v[n]]
            for j in range(2):
                sl = slice((2 * n + j) * LANES, (2 * n + j + 1) * LANES)
                q = qg_ref[qs, sl]
                q_even, q_odd = jnp.where(low_half, q, zero), jnp.where(low_half, zero, q)
                if latent:
                    r_even = _attend(q_even, keys, [v[0] for v in values])
                    r_odd = _attend(q_odd, keys, [v[1] for v in values])
                    r = jnp.where(low_half, r_even, r_odd)
                    denom = jnp.where(low_half, pltpu.roll(r_even, HEAD_DIM, 1), pltpu.roll(r_odd, HEAD_DIM, 1))
                    out = r / denom
                else:
                    v_dup = [vv_ref[ks, kv_sl]]
                    out = jnp.where(low_half, _attend_rowsum(q_even, keys, v_dup), _attend_rowsum(q_odd, keys, v_dup))
                ga_ref[qs, sl] = out.astype(BF16)


def _attention(pre_outs, caches, layer, lam_params, subln, casts, *, latent, n_seq, seq, q_tile, seqs_per_step):
    qd, kd, vd, qg, kk, vv = pre_outs
    tiles = seq // q_tile
    assert seqs_per_step == 1 or tiles == 1
    n_seq //= seqs_per_step
    n_steps = n_seq * tiles
    q_spec = lambda w: pl.BlockSpec((seqs_per_step * q_tile, w), lambda b, i: (b * tiles + i, 0))
    kv_spec = lambda w: pl.BlockSpec((seqs_per_step * seq, w), lambda b, i: (b, 0))
    kv_dup = GQA_KV_HEADS * LANES
    in_specs = [q_spec(DIFF_WIDTH), kv_spec(DIFF_WIDTH), kv_spec(DIFF_WIDTH), q_spec(GQA_WIDTH),
                kv_spec(kv_dup), kv_spec(kv_dup)]
    args = [qd, kd, vd, qg, kk, vv]
    if latent:
        for c in caches:
            in_specs.append(pl.BlockSpec((1, 1) + c.shape[2:], lambda b, i: (b, layer, 0, 0)))
            args.append(c)
    in_specs += [_resident((4, HEAD_DIM)), _resident((1, LANES))]
    args += [lam_params, subln]
    out_specs = [q_spec(DIFF_WIDTH), q_spec(GQA_WIDTH)]
    out_shape = [jax.ShapeDtypeStruct(qd.shape, BF16)] * 2
    c_in, c_args, c_out, c_shape, cast_splits = _cast_specs(casts, n_steps, lambda b, i: b * tiles + i)
    in_specs, args, out_specs, out_shape = in_specs + c_in, args + c_args, out_specs + c_out, out_shape + c_shape
    lam_init = 0.8 - 0.6 * math.exp(-0.3 * layer)
    return pl.pallas_call(
        functools.partial(_attn_kernel, latent=latent, lam_init=lam_init, cast_splits=cast_splits,
                          n_seqs=seqs_per_step),
        grid=(n_seq, tiles),
        in_specs=in_specs,
        out_specs=out_specs,
        out_shape=out_shape,
        compiler_params=_params(2),
        name="attn_latent" if latent else "attn_ctx",
    )(*args)


def _conv_branch(u, head, tail, joined, w_ref, bias, gain, beta, win_ref, y_ref, seg):
    n_seg = u.shape[0] // seg
    pitch = seg + 2 * CONV_HALO
    lane_slabs = [slice(s * LANES, (s + 1) * LANES) for s in range(CONV_WIDTH // LANES)]
    for k in range(n_seg):
        lo, hi = k * seg, (k + 1) * seg
        before = head if k == 0 else jnp.where(joined, u[lo - CONV_HALO:lo], 0.0)
        after = tail if k == n_seg - 1 else jnp.where(joined, u[hi:hi + CONV_HALO], 0.0)
        for s, ls in enumerate(lane_slabs):
            win_ref[s, k * pitch:k * pitch + CONV_HALO, :] = before[:, ls]
            win_ref[s, k * pitch + CONV_HALO:(k + 1) * pitch - CONV_HALO, :] = u[lo:hi, ls]
            win_ref[s, (k + 1) * pitch - CONV_HALO:(k + 1) * pitch, :] = after[:, ls]

    first = CONV_HALO - CONV_KSIZE // 2
    group = 16
    rows_per_group = group * SUBLANES
    for k in range(n_seg):
        for s, ls in enumerate(lane_slabs):
            for r0 in range(0, seg, rows_per_group):
                offs = [r + t for r in range(r0, r0 + rows_per_group, SUBLANES * CONV_ROW_STRIDE)
                        for t in range(CONV_ROW_STRIDE)]
                accs = [jnp.zeros((SUBLANES, LANES), F32)] * group
                for j in range(CONV_KSIZE):
                    w_tap = jnp.broadcast_to(w_ref[j:j + 1, ls], (SUBLANES, LANES))
                    for a, off in enumerate(offs):
                        start = k * pitch + off + first + j
                        accs[a] = accs[a] + win_ref[s, pl.ds(start, SUBLANES, stride=CONV_ROW_STRIDE), :] * w_tap
                for a, off in enumerate(offs):
                    y_ref[s, pl.ds(k * seg + off, SUBLANES, stride=CONV_ROW_STRIDE), :] = accs[a]
    acc = jnp.concatenate([y_ref[s] for s in range(len(lane_slabs))], axis=1) + bias
    mu = jnp.mean(acc, axis=-1, keepdims=True)
    xc = acc - mu
    y = xc * lax.rsqrt(jnp.mean(xc * xc, axis=-1, keepdims=True) + NORM_EPS) * gain + beta
    return y * jax.nn.sigmoid(y)


def _post_kernel(xc_ref, xl_ref, dac_ref, dal_ref, gac_ref, gal_ref, uc_ref, ul_ref, uprev_ref, unext_ref,
                 mod_ref, n1_ref, n2_ref, fn_ref, cw_ref, cb_ref, cg_ref, cbeta_ref,
                 wg_ref, wda_ref, wga_ref, wco_ref, wo_ref, w1_ref, w2_ref, oc_ref, ol_ref, win_ref, y_ref,
                 *, last, seg, n_ctx_tiles, tiles_per_seq):
    step = pl.program_id(0)
    is_lat = step >= n_ctx_tiles
    pick = lambda c_ref, l_ref: jnp.where(is_lat, l_ref[...], c_ref[...])

    pos = jnp.maximum(step - n_ctx_tiles, 0) % tiles_per_seq
    head = jnp.where(is_lat & (pos > 0), uprev_ref[...], 0.0)
    tail = jnp.where(is_lat & (pos < tiles_per_seq - 1), unext_ref[...], 0.0)
    ca = _conv_branch(pick(uc_ref, ul_ref), head, tail, is_lat, cw_ref, cb_ref[...], cg_ref[...], cbeta_ref[...],
                      win_ref, y_ref, seg)

    x = pick(xc_ref, xl_ref)
    mod = mod_ref[0, 0]
    m = lambda k: mod[:, k * D_MODEL:(k + 1) * D_MODEL]
    h = _modulated_norm(x, n1_ref[...], m(0), m(1))

    branches = ((pick(dac_ref, dal_ref), wda_ref), (pick(gac_ref, gal_ref), wga_ref), (ca.astype(BF16), wco_ref))
    merged = None
    for j, (act, w_ref) in enumerate(branches):
        gate = jax.nn.sigmoid(_dot(h, wg_ref[:, j * D_MODEL:(j + 1) * D_MODEL]))
        term = gate * _dot(act, w_ref[...])
        merged = term if merged is None else merged + term
    x = x + m(2) * _dot(merged.astype(BF16), wo_ref[...])

    h2 = _modulated_norm(x, n2_ref[...], m(3), m(4))
    f = None
    for c in range(0, MLP_HIDDEN, MLP_CHUNK):
        a = jnp.maximum(_dot(h2, w1_ref[:, c:c + MLP_CHUNK]), 0.0)
        term = _dot((a * a).astype(BF16), w2_ref[c:c + MLP_CHUNK, :])
        f = term if f is None else f + term
    x = x + m(5) * f
    out = _rms(x, fn_ref[...]) if last else x

    @pl.when(jnp.logical_not(is_lat))
    def _():
        oc_ref[...] = out

    @pl.when(is_lat)
    def _():
        ol_ref[...] = out


def _post(xs, das, gas, us, mod, layer, norm1, norm2, final_norm, conv_params, weights, *, seqs, tile, last):
    seq_c, seq_l = seqs
    assert tile % seq_c == 0 and seq_l % tile == 0
    n_c, n_l = xs[0].shape[0] // tile, xs[1].shape[0] // tile
    tiles_per_seq = seq_l // tile
    ctx_i = lambda i: jnp.minimum(i, n_c - 1)
    lat_i = lambda i: jnp.maximum(i - n_c, 0)
    pair = lambda w: [pl.BlockSpec((tile, w), lambda i: (ctx_i(i), 0)), pl.BlockSpec((tile, w), lambda i: (lat_i(i), 0))]
    per_tile, n_halo = tile // CONV_HALO, us[1].shape[0] // CONV_HALO
    row = lambda i: jnp.where(i < n_c, 0, 1 + lat_i(i) // tiles_per_seq)
    in_specs = pair(D_MODEL) + pair(DIFF_WIDTH) + pair(GQA_WIDTH) + pair(CONV_WIDTH)
    in_specs += [pl.BlockSpec((CONV_HALO, CONV_WIDTH), lambda i: (jnp.maximum(lat_i(i) * per_tile - 1, 0), 0)),
                 pl.BlockSpec((CONV_HALO, CONV_WIDTH),
                              lambda i: (jnp.minimum((lat_i(i) + 1) * per_tile, n_halo - 1), 0)),
                 pl.BlockSpec((1, 1, 1, N_MOD * D_MODEL), lambda i: (layer, row(i), 0, 0)),
                 _resident((1, D_MODEL)), _resident((1, D_MODEL)), _resident((1, D_MODEL))]
    in_specs += [_resident(p.shape) for p in conv_params] + [_resident(w.shape) for w in weights]
    n_slab = CONV_WIDTH // LANES
    return pl.pallas_call(
        functools.partial(_post_kernel, last=last, seg=seq_c, n_ctx_tiles=n_c, tiles_per_seq=tiles_per_seq),
        grid=(n_c + n_l,),
        in_specs=in_specs,
        out_specs=pair(D_MODEL),
        out_shape=[jax.ShapeDtypeStruct(x.shape, F32) for x in xs],
        scratch_shapes=[pltpu.VMEM((n_slab, (tile // seq_c) * (seq_c + 2 * CONV_HALO), LANES), F32),
                        pltpu.VMEM((n_slab, tile, LANES), F32)],
        compiler_params=_params(1),
        name="post",
    )(*xs, *das, *gas, *us, us[1], us[1], mod, norm1, norm2, final_norm, *conv_params, *weights)


def _rope_tables(n_tokens):
    n_rows = n_tokens // GRID_W
    row = jnp.repeat(jnp.arange(n_rows), GRID_W).astype(F32)
    col = jnp.tile(jnp.arange(GRID_W), n_rows).astype(F32)
    axis_dim = HEAD_DIM // 2
    freqs = ROPE_THETA ** (-jnp.arange(0, axis_dim, 2, dtype=F32) / axis_dim)
    ang_r = row[:, None] * freqs[None, :]
    ang_c = col[:, None] * freqs[None, :]
    ang = jnp.concatenate([ang_r, ang_r, ang_c, ang_c], axis=-1)
    sign = jnp.tile(jnp.repeat(jnp.array([-1.0, 1.0], F32), HEAD_DIM // 4), 2)
    return jnp.tile(jnp.cos(ang), (1, 2)), jnp.tile(jnp.sin(ang) * sign, (1, 2))


def kernel(x_prompt, x_sample, cache_diff_k, cache_diff_v, cache_gqa_k, cache_gqa_v, c, c_ctx, w_ada, b_ada, norm1, norm2, w_in, diff_lq1, diff_lk1, diff_lq2, diff_lk2, diff_subln, w_diff_o, gqa_q_norm, gqa_k_norm, w_gqa_o, conv_dw, conv_dw_b, conv_ln_g, conv_ln_b, w_conv_o, w_o, w_mlp1, w_mlp2, final_norm):
    n_ctx, s_ctx, _ = x_prompt.shape
    n_lat, s_lat, _ = x_sample.shape
    past = cache_diff_k.shape[2]
    assert n_lat + 1 <= MOD_ROWS

    cvecs = jnp.concatenate([c_ctx[None], c, jnp.zeros((MOD_ROWS - 1 - n_lat, D_MODEL), F32)], axis=0)
    mod = _modulation(cvecs, w_ada, b_ada).reshape(DEPTH, MOD_ROWS, 1, N_MOD * D_MODEL)

    caches = (cache_diff_k.reshape(n_lat, DEPTH, past, DIFF_WIDTH), cache_diff_v.reshape(n_lat, DEPTH, past, DIFF_WIDTH),
              cache_gqa_k.reshape(n_lat, DEPTH, past, GQA_KV_WIDTH),
              cache_gqa_v.reshape(n_lat, DEPTH, past, GQA_KV_WIDTH))
    rope_tabs = _rope_tables(s_lat)
    row_vec = lambda p: p.reshape(1, -1)
    fn = row_vec(final_norm)

    groups = (dict(latent=False, n_seq=n_ctx, seq=s_ctx, tile=TOKEN_TILE, q_tile=s_ctx, seqs_per_step=1),
              dict(latent=True, n_seq=n_lat, seq=s_lat, tile=TOKEN_TILE, q_tile=LATENT_Q_TILE, seqs_per_step=1))
    xs = [x_prompt.reshape(n_ctx * s_ctx, D_MODEL), x_sample.reshape(n_lat * s_lat, D_MODEL)]
    new_cache = None

    mix_cols, gate_cols = (0, MIX_WIDTH), (MIX_WIDTH, MIX_WIDTH + GATE_WIDTH)
    whole = lambda w: ((0, w.shape[2]),)
    w_mix, w_gate = w_in, None

    for l in range(DEPTH):
        qn = row_vec(jnp.tile(gqa_q_norm[l], 2))
        kn = row_vec(jnp.tile(gqa_k_norm[l], 2))
        lam_params = jnp.stack([diff_lq1[l], diff_lk1[l], diff_lq2[l], diff_lk2[l]])
        n1, n2 = row_vec(norm1[l]), row_vec(norm2[l])
        conv_params = [conv_dw[l], row_vec(conv_dw_b[l]), row_vec(conv_ln_g[l]), row_vec(conv_ln_b[l])]
        next_in = [(w_in, l + 1, (mix_cols, gate_cols))] if l + 1 < DEPTH else []
        this_gate = [(w_in, l, (gate_cols,))] if w_gate is None else []
        pre_casts = ([(w_mlp1, l, whole(w_mlp1))], [(w_mlp2, l, whole(w_mlp2))])
        attn_casts = ([(w, l, whole(w)) for w in (w_diff_o, w_gqa_o, w_conv_o, w_o)] + this_gate + next_in, [])

        pre_outs = []
        for gi, g in enumerate(groups):
            outs = _pre(xs[gi], mod, l, n1, w_mix, qn, kn, rope_tabs, new_cache, pre_casts[gi],
                        latent=g["latent"], seq=g["seq"], tile=g["tile"])
            if not g["latent"]:
                new_cache = outs[7:11]
            pre_outs.append(outs)
        attn_outs = [_attention(pre_outs[gi][:6], caches, l, lam_params, row_vec(diff_subln[l]), attn_casts[gi],
                                latent=g["latent"], n_seq=g["n_seq"], seq=g["seq"], q_tile=g["q_tile"],
                                seqs_per_step=g["seqs_per_step"]) for gi, g in enumerate(groups)]
        w_da_b, w_ga_b, w_co_b, w_o_b, *ctx_converted = attn_outs[0][2:]
        if this_gate:
            w_gate = ctx_converted.pop(0)
        post_w = [w_gate, w_da_b, w_ga_b, w_co_b, w_o_b, pre_outs[0][-1], pre_outs[1][-1]]
        next_in_b = ctx_converted
        xs = list(_post(xs, [a[0] for a in attn_outs], [a[1] for a in attn_outs], [p[6] for p in pre_outs], mod, l,
                        n1, n2, fn, conv_params, post_w, seqs=(s_ctx, s_lat), tile=TOKEN_TILE,
                        last=(l == DEPTH - 1)))
        if next_in_b:
            w_mix, w_gate = next_in_b

    ndk, ndv, ngk, ngv = new_cache
    lead = (n_ctx, DEPTH, s_ctx)
    return (xs[0].reshape(n_ctx, s_ctx, D_MODEL), xs[1].reshape(n_lat, s_lat, D_MODEL),
            ndk.reshape(lead + (DIFF_HEADS, 2, HEAD_DIM)), ndv.reshape(lead + (DIFF_HEADS, 2 * HEAD_DIM)),
            ngk.reshape(lead + (GQA_KV_HEADS, HEAD_DIM)), ngv.reshape(lead + (GQA_KV_HEADS, HEAD_DIM)))
```

```python
import functools
import math

import jax
import jax.numpy as jnp
from jax import lax
from jax.experimental import pallas as pl
from jax.experimental.pallas import tpu as pltpu

D_MODEL = 1024
DEPTH = 2
GRID_W = 64
ROPE_THETA = 10000.0
NORM_EPS = 1e-6

DIFF_HEADS = 4
HEAD_DIM = 64
DIFF_WIDTH = 512
GQA_KV_HEADS = 2
GQA_WIDTH = 512
GQA_KV_WIDTH = 128
CONV_WIDTH = 512
CONV_KSIZE = 31
CONV_HALO = 16
N_BRANCH = 3
MLP_HIDDEN = 4 * D_MODEL
N_MOD = 6

MIX_WIDTH = 3 * 512 + 512 + 2 * 128 + 2 * CONV_WIDTH
GATE_WIDTH = N_BRANCH * D_MODEL

LANES = 128
SUBLANES = 8
CONV_ROW_STRIDE = 4
MXU_WIDTH = 256
MOD_ROWS = 8
MOD_COL_TILE = 1536
TOKEN_TILE = 512
LATENT_Q_TILE = 512
MLP_CHUNK = 1024
VMEM_LIMIT = 61 * 1024 * 1024

F32 = jnp.float32
BF16 = jnp.bfloat16


def _dot(a, b):
    return jnp.dot(a, b, preferred_element_type=F32)


def _dot_nt(a, b):
    return lax.dot_general(a, b, (((1,), (1,)), ((), ())), preferred_element_type=F32)


def _rms(x, gain):
    return x * lax.rsqrt(jnp.mean(x * x, axis=-1, keepdims=True) + NORM_EPS) * gain


def _modulated_norm(x, gain, shift, scale):
    return (_rms(x, gain) * (1.0 + scale) + shift).astype(BF16)


def _lane_iota(shape):
    return lax.broadcasted_iota(jnp.int32, shape, len(shape) - 1)


def _resident(shape):
    nd = len(shape)
    return pl.BlockSpec(shape, lambda *_: (0,) * nd, pipeline_mode=pl.Buffered(1))


def _params(n_axes):
    return pltpu.CompilerParams(dimension_semantics=("arbitrary",) * n_axes,
                                vmem_limit_bytes=VMEM_LIMIT)


def _mod_kernel(c_ref, w_ref, b_ref, o_ref):
    c = c_ref[...]
    s = (c * jax.nn.sigmoid(c)).astype(BF16)
    o_ref[0] = _dot(s, w_ref[0].astype(BF16)) + b_ref[0]


def _modulation(cvecs, w_ada, b_ada):
    width = N_MOD * D_MODEL
    tn = MOD_COL_TILE
    assert width % tn == 0
    return pl.pallas_call(
        _mod_kernel,
        grid=(DEPTH, width // tn),
        in_specs=[pl.BlockSpec((MOD_ROWS, D_MODEL), lambda l, j: (0, 0)),
                  pl.BlockSpec((1, D_MODEL, tn), lambda l, j: (l, 0, j)),
                  pl.BlockSpec((1, 1, tn), lambda l, j: (l, 0, j))],
        out_specs=pl.BlockSpec((1, MOD_ROWS, tn), lambda l, j: (l, 0, j)),
        out_shape=jax.ShapeDtypeStruct((DEPTH, MOD_ROWS, width), F32),
        compiler_params=_params(2),
        name="adaln_mod",
    )(cvecs, w_ada, b_ada.reshape(DEPTH, 1, width))


def _rope(x, cos, sin_signed, first_half):
    rot = jnp.where(first_half, pltpu.roll(x, LANES - 16, 1), pltpu.roll(x, 16, 1))
    return x * cos + rot * sin_signed


def _head_mean_sq(x):
    width = min(x.shape[1], MXU_WIDTH)
    r = lax.broadcasted_iota(jnp.int32, (width, width), 0) // HEAD_DIM
    c = lax.broadcasted_iota(jnp.int32, (width, width), 1) // HEAD_DIM
    blockdiag = jnp.where(r == c, 1.0 / HEAD_DIM, 0.0).astype(BF16)
    parts = []
    for lo in range(0, x.shape[1], width):
        sq = x[:, lo:lo + width] * x[:, lo:lo + width]
        parts.append(_dot(sq.astype(BF16), blockdiag))
    return parts[0] if len(parts) == 1 else jnp.concatenate(parts, axis=1)


def _store_cache(ref, val, layer):
    seq = ref.shape[2]
    own = layer if ref.shape[1] > 1 else 0
    for b in range(ref.shape[0]):
        for slot in range(ref.shape[1]):
            ref[b, slot] = val[b * seq:(b + 1) * seq] if slot == own else jnp.zeros((seq, ref.shape[3]), ref.dtype)


def _cast_specs(casts, n_steps, step_of):
    in_specs, args, out_specs, out_shape = [], [], [], []
    for w, w_layer, splits in casts:
        _, n_rows, n_cols = w.shape
        blk = n_rows // n_steps
        assert n_rows % n_steps == 0 and blk % 16 == 0
        in_specs.append(pl.BlockSpec((1, blk, n_cols), lambda *g, w_layer=w_layer: (w_layer, step_of(*g), 0)))
        args.append(w)
        for lo, hi in splits:
            out_specs.append(pl.BlockSpec((blk, hi - lo), lambda *g: (step_of(*g), 0)))
            out_shape.append(jax.ShapeDtypeStruct((n_rows, hi - lo), BF16))
    return in_specs, args, out_specs, out_shape, tuple(tuple(s) for _, _, s in casts)


def _run_casts(in_refs, out_refs, cast_splits):
    out_refs = iter(out_refs)
    for w_ref, splits in zip(in_refs, cast_splits):
        for lo, hi in splits:
            next(out_refs)[...] = w_ref[0, :, lo:hi].astype(BF16)


def _pre_kernel(*refs, latent, layer, n_aliased, f32_weights, cast_splits):
    if f32_weights:
        *refs, w_bf16_ref = refs
    n_in = 8 if latent else 6
    n_out = 7 if latent else 11
    cast_in = refs[n_in + n_aliased:n_in + n_aliased + len(cast_splits)]
    outs = refs[n_in + n_aliased + len(cast_splits):]
    _run_casts(cast_in, outs[n_out:], cast_splits)
    if latent:
        x_ref, mod_ref, n1_ref, w_ref, qn_ref, kn_ref, cos_ref, sin_ref = refs[:n_in]
        qd_ref, kd_ref, vd_ref, qg_ref, kk_ref, vv_ref, u_ref = outs[:n_out]
    else:
        x_ref, mod_ref, n1_ref, w_ref, qn_ref, kn_ref = refs[:n_in]
        (qd_ref, kd_ref, vd_ref, qg_ref, kk_ref, vv_ref, u_ref,
         ndk_ref, ndv_ref, ngk_ref, ngv_ref) = outs[:n_out]

    if f32_weights:
        @pl.when(pl.program_id(0) == 0)
        def _():
            w_bf16_ref[...] = w_ref[0].astype(BF16)
        w_ref = w_bf16_ref

    mod = mod_ref[0, 0]
    h = _modulated_norm(x_ref[...], n1_ref[...], mod[:, 0:D_MODEL], mod[:, D_MODEL:2 * D_MODEL])
    rows = h.shape[0]

    lane = _lane_iota((rows, LANES))
    low_half = lane < HEAD_DIM
    if latent:
        cos, sin_signed = cos_ref[...], sin_ref[...]
        first_half = ((lane % HEAD_DIM) // 16) % 2 == 0
        rope = lambda t: _rope(t, cos, sin_signed, first_half)
    else:
        rope = lambda t: t

    qk_scale = HEAD_DIM ** -0.5 * math.log2(math.e)

    def proj(lo, width):
        return _dot(h, w_ref[:, lo:lo + width])

    def slabs(t):
        return [t[:, s:s + LANES] for s in range(0, t.shape[1], LANES)]

    def store_slabs(ref, parts):
        for s, p in enumerate(parts):
            ref[:, s * LANES:(s + 1) * LANES] = p.astype(ref.dtype)

    store_slabs(qd_ref, [rope(t) * qk_scale for t in slabs(proj(0, DIFF_WIDTH))])
    dk = proj(DIFF_WIDTH, DIFF_WIDTH)
    dv = proj(2 * DIFF_WIDTH, DIFF_WIDTH)
    if not latent:
        _store_cache(ndk_ref, dk, layer)
        _store_cache(ndv_ref, dv, layer)
    store_slabs(kd_ref, [rope(t) for t in slabs(dk)])
    vd_ref[...] = dv.astype(BF16)

    gq = proj(3 * DIFF_WIDTH, GQA_WIDTH)
    gq = gq * lax.rsqrt(_head_mean_sq(gq) + NORM_EPS)
    qn = qn_ref[...]
    store_slabs(qg_ref, [rope(t * qn) * qk_scale for t in slabs(gq)])

    gkv = proj(3 * DIFF_WIDTH + GQA_WIDTH, 2 * GQA_KV_WIDTH)
    k, v = gkv[:, :GQA_KV_WIDTH], gkv[:, GQA_KV_WIDTH:]
    k = k * lax.rsqrt(_head_mean_sq(k) + NORM_EPS) * kn_ref[...]
    if not latent:
        _store_cache(ngk_ref, k, layer)
        _store_cache(ngv_ref, v, layer)
    k = rope(k)
    k_sw, v_sw = pltpu.roll(k, HEAD_DIM, 1), pltpu.roll(v, HEAD_DIM, 1)
    store_slabs(kk_ref, [jnp.where(low_half, k, k_sw), jnp.where(low_half, k_sw, k)])
    store_slabs(vv_ref, [jnp.where(low_half, v, v_sw), jnp.where(low_half, v_sw, v)])

    cv = proj(3 * DIFF_WIDTH + GQA_WIDTH + 2 * GQA_KV_WIDTH, 2 * CONV_WIDTH)
    u_ref[...] = cv[:, :CONV_WIDTH] * jax.nn.sigmoid(cv[:, CONV_WIDTH:])


def _pre(x2d, mod, layer, norm1, w_mix, qn, kn, rope_tabs, new_cache, casts, *, latent, seq, tile):
    n_tok = x2d.shape[0]
    tiles_per_seq = seq // tile if latent else 1
    row = (lambda i: 1 + i // tiles_per_seq) if latent else (lambda i: 0)
    tok = lambda w: pl.BlockSpec((tile, w), lambda i: (i, 0))
    f32_weights = w_mix.dtype == F32
    w_spec = (pl.BlockSpec((1, D_MODEL, MIX_WIDTH), lambda i: (layer, 0, 0), pipeline_mode=pl.Buffered(1))
              if f32_weights else _resident((D_MODEL, MIX_WIDTH)))
    in_specs = [tok(D_MODEL),
                pl.BlockSpec((1, 1, 1, N_MOD * D_MODEL), lambda i: (layer, row(i), 0, 0)),
                _resident((1, D_MODEL)), w_spec,
                _resident((1, LANES)), _resident((1, LANES))]
    args = [x2d, mod, norm1, w_mix, qn, kn]
    widths = [(DIFF_WIDTH, BF16), (DIFF_WIDTH, BF16), (DIFF_WIDTH, BF16), (GQA_WIDTH, BF16),
              (GQA_KV_HEADS * LANES, BF16), (GQA_KV_HEADS * LANES, BF16), (CONV_WIDTH, F32)]
    out_specs = [tok(w) for w, _ in widths]
    out_shape = [jax.ShapeDtypeStruct((n_tok, w), dt) for w, dt in widths]
    aliases = {}
    if latent:
        in_specs += [pl.BlockSpec((tile, LANES), lambda i: (i % tiles_per_seq, 0))] * 2
        args += list(rope_tabs)
    else:
        per_tile = tile // seq
        for k, w in enumerate((DIFF_WIDTH, DIFF_WIDTH, GQA_KV_WIDTH, GQA_KV_WIDTH)):
            if new_cache is None:
                out_specs.append(pl.BlockSpec((per_tile, DEPTH, seq, w), lambda i: (i, 0, 0, 0)))
            else:
                aliases[len(args)] = len(out_shape)
                in_specs.append(pl.BlockSpec(memory_space=pl.ANY))
                args.append(new_cache[k])
                out_specs.append(pl.BlockSpec((per_tile, 1, seq, w), lambda i: (i, layer, 0, 0)))
            out_shape.append(jax.ShapeDtypeStruct((n_tok // seq, DEPTH, seq, w), F32))
    c_in, c_args, c_out, c_shape, cast_splits = _cast_specs(casts, n_tok // tile, lambda i: i)
    in_specs, args, out_specs, out_shape = in_specs + c_in, args + c_args, out_specs + c_out, out_shape + c_shape
    return pl.pallas_call(
        functools.partial(_pre_kernel, latent=latent, layer=layer, n_aliased=len(aliases),
                          f32_weights=f32_weights, cast_splits=cast_splits),
        grid=(n_tok // tile,),
        in_specs=in_specs,
        out_specs=out_specs,
        out_shape=out_shape,
        scratch_shapes=[pltpu.VMEM((D_MODEL, MIX_WIDTH), BF16)] if f32_weights else [],
        input_output_aliases=aliases,
        compiler_params=_params(1),
        name="pre_latent" if latent else "pre_ctx",
    )(*args)


def _attend(q, keys, values):
    scores = [_dot_nt(q, k) for k in keys]
    m = functools.reduce(jnp.maximum, [jnp.max(s, axis=-1, keepdims=True) for s in scores])
    return functools.reduce(jnp.add, [_dot(jnp.exp2(s - m).astype(BF16), v) for s, v in zip(scores, values)])


def _attend_rowsum(q, keys, values):
    scores = [_dot_nt(q, k) for k in keys]
    m = functools.reduce(jnp.maximum, [jnp.max(s, axis=-1, keepdims=True) for s in scores])
    exps = [jnp.exp2(s - m) for s in scores]
    denom = functools.reduce(jnp.add, [jnp.sum(e, axis=-1, keepdims=True) for e in exps])
    out = functools.reduce(jnp.add, [_dot(e.astype(BF16), v) for e, v in zip(exps, values)])
    return out / denom


def _attn_kernel(*refs, latent, lam_init, cast_splits, n_seqs):
    n_core = 12 if latent else 8
    if latent:
        (qd_ref, kd_ref, vd_ref, qg_ref, kk_ref, vv_ref, cdk_ref, cdv_ref, cgk_ref, cgv_ref,
         lp_ref, sub_ref) = refs[:n_core]
    else:
        qd_ref, kd_ref, vd_ref, qg_ref, kk_ref, vv_ref, lp_ref, sub_ref = refs[:n_core]
    n_in = n_core + len(cast_splits)
    da_ref, ga_ref = refs[n_in:n_in + 2]
    _run_casts(refs[n_core:n_in], refs[n_in + 2:], cast_splits)

    rows = qd_ref.shape[0] // n_seqs
    kv_rows = kd_ref.shape[0] // n_seqs
    low_half = _lane_iota((rows, LANES)) < HEAD_DIM
    zero = jnp.zeros((), BF16)

    lp = lp_ref[...]
    lam = (jnp.exp(jnp.sum(lp[0:1] * lp[1:2], axis=-1, keepdims=True))
           - jnp.exp(jnp.sum(lp[2:3] * lp[3:4], axis=-1, keepdims=True)) + lam_init)
    sub_gain = sub_ref[...] * (1.0 - lam_init)

    def with_ones(v):
        return jnp.concatenate([v, jnp.ones_like(v)], axis=1)

    def value_slabs(v_dup):
        low = _lane_iota(v_dup.shape) < HEAD_DIM
        one = jnp.ones((), v_dup.dtype)
        return jnp.where(low, v_dup, one), jnp.where(low, one, v_dup)

    if latent:
        ck, cv = cgk_ref[0, 0], cgv_ref[0, 0]
        ck_sw, cv_sw = pltpu.roll(ck, HEAD_DIM, 1), pltpu.roll(cv, HEAD_DIM, 1)
        low_c = _lane_iota(ck.shape) < HEAD_DIM
        cache_k = [jnp.where(low_c, ck, ck_sw).astype(BF16), jnp.where(low_c, ck_sw, ck).astype(BF16)]
        cache_v = [value_slabs(jnp.where(low_c, cv, cv_sw).astype(BF16)),
                   value_slabs(jnp.where(low_c, cv_sw, cv).astype(BF16))]
    for s in range(n_seqs):
        qs = slice(s * rows, (s + 1) * rows)
        ks = slice(s * kv_rows, (s + 1) * kv_rows)
        for h in range(DIFF_HEADS):
            sl = slice(h * LANES, (h + 1) * LANES)
            q = qd_ref[qs, sl]
            q1, q2 = jnp.where(low_half, q, zero), jnp.where(low_half, zero, q)
            if latent:
                keys = [kd_ref[ks, sl], cdk_ref[0, 0, :, sl].astype(BF16)]
                values = [with_ones(vd_ref[ks, sl]), with_ones(cdv_ref[0, 0, :, sl].astype(BF16))]
                r1 = _attend(q1, keys, values)
                r2 = _attend(q2, keys, values)
                o = r1[:, :LANES] / r1[:, LANES:] - lam * (r2[:, :LANES] / r2[:, LANES:])
            else:
                keys, values = [kd_ref[ks, sl]], [vd_ref[ks, sl]]
                o = _attend_rowsum(q1, keys, values) - lam * _attend_rowsum(q2, keys, values)
            da_ref[qs, sl] = _rms(o, sub_gain).astype(BF16)

        for n in range(GQA_KV_HEADS):
            kv_sl = slice(n * LANES, (n + 1) * LANES)
            keys = [kk_ref[ks, kv_sl]]
            if latent:
                keys.append(cache_k[n])
                values = [value_slabs(vv_ref[ks, kv_sl]), cache_v[n]]
            for j in range(2):
                sl = slice((2 * n + j) * LANES, (2 * n + j + 1) * LANES)
                q = qg_ref[qs, sl]
                q_even, q_odd = jnp.where(low_half, q, zero), jnp.where(low_half, zero, q)
                if latent:
                    r_even = _attend(q_even, keys, [v[0] for v in values])
                    r_odd = _attend(q_odd, keys, [v[1] for v in values])
                    r = jnp.where(low_half, r_even, r_odd)
                    denom = jnp.where(low_half, pltpu.roll(r_even, HEAD_DIM, 1), pltpu.roll(r_odd, HEAD_DIM, 1))
                    out = r / denom
                else:
                    v_dup = [vv_ref[ks, kv_sl]]
                    out = jnp.where(low_half, _attend_rowsum(q_even, keys, v_dup), _attend_rowsum(q_odd, keys, v_dup))
                ga_ref[qs, sl] = out.astype(BF16)


def _attention(pre_outs, caches, layer, lam_params, subln, casts, *, latent, n_seq, seq, q_tile, seqs_per_step):
    qd, kd, vd, qg, kk, vv = pre_outs
    tiles = seq // q_tile
    assert seqs_per_step == 1 or tiles == 1
    n_seq //= seqs_per_step
    n_steps = n_seq * tiles
    q_spec = lambda w: pl.BlockSpec((seqs_per_step * q_tile, w), lambda b, i: (b * tiles + i, 0))
    kv_spec = lambda w: pl.BlockSpec((seqs_per_step * seq, w), lambda b, i: (b, 0))
    kv_dup = GQA_KV_HEADS * LANES
    in_specs = [q_spec(DIFF_WIDTH), kv_spec(DIFF_WIDTH), kv_spec(DIFF_WIDTH), q_spec(GQA_WIDTH),
                kv_spec(kv_dup), kv_spec(kv_dup)]
    args = [qd, kd, vd, qg, kk, vv]
    if latent:
        for c in caches:
            in_specs.append(pl.BlockSpec((1, 1) + c.shape[2:], lambda b, i: (b, layer, 0, 0)))
            args.append(c)
    in_specs += [_resident((4, HEAD_DIM)), _resident((1, LANES))]
    args += [lam_params, subln]
    out_specs = [q_spec(DIFF_WIDTH), q_spec(GQA_WIDTH)]
    out_shape = [jax.ShapeDtypeStruct(qd.shape, BF16)] * 2
    c_in, c_args, c_out, c_shape, cast_splits = _cast_specs(casts, n_steps, lambda b, i: b * tiles + i)
    in_specs, args, out_specs, out_shape = in_specs + c_in, args + c_args, out_specs + c_out, out_shape + c_shape
    lam_init = 0.8 - 0.6 * math.exp(-0.3 * layer)
    return pl.pallas_call(
        functools.partial(_attn_kernel, latent=latent, lam_init=lam_init, cast_splits=cast_splits,
                          n_seqs=seqs_per_step),
        grid=(n_seq, tiles),
        in_specs=in_specs,
        out_specs=out_specs,
        out_shape=out_shape,
        compiler_params=_params(2),
        name="attn_latent" if latent else "attn_ctx",
    )(*args)


def _conv_branch(u, head, tail, joined, w_ref, bias, gain, beta, win_ref, y_ref, seg):
    n_seg = u.shape[0] // seg
    pitch = seg + 2 * CONV_HALO
    lane_slabs = [slice(s * LANES, (s + 1) * LANES) for s in range(CONV_WIDTH // LANES)]
    for k in range(n_seg):
        lo, hi = k * seg, (k + 1) * seg
        before = head if k == 0 else jnp.where(joined, u[lo - CONV_HALO:lo], 0.0)
        after = tail if k == n_seg - 1 else jnp.where(joined, u[hi:hi + CONV_HALO], 0.0)
        for s, ls in enumerate(lane_slabs):
            win_ref[s, k * pitch:k * pitch + CONV_HALO, :] = before[:, ls]
            win_ref[s, k * pitch + CONV_HALO:(k + 1) * pitch - CONV_HALO, :] = u[lo:hi, ls]
            win_ref[s, (k + 1) * pitch - CONV_HALO:(k + 1) * pitch, :] = after[:, ls]

    first = CONV_HALO - CONV_KSIZE // 2
    group = 16
    rows_per_group = group * SUBLANES
    for k in range(n_seg):
        for s, ls in enumerate(lane_slabs):
            for r0 in range(0, seg, rows_per_group):
                offs = [r + t for r in range(r0, r0 + rows_per_group, SUBLANES * CONV_ROW_STRIDE)
                        for t in range(CONV_ROW_STRIDE)]
                accs = [jnp.zeros((SUBLANES, LANES), F32)] * group
                for j in range(CONV_KSIZE):
                    w_tap = jnp.broadcast_to(w_ref[j:j + 1, ls], (SUBLANES, LANES))
                    for a, off in enumerate(offs):
                        start = k * pitch + off + first + j
                        accs[a] = accs[a] + win_ref[s, pl.ds(start, SUBLANES, stride=CONV_ROW_STRIDE), :] * w_tap
                for a, off in enumerate(offs):
                    y_ref[s, pl.ds(k * seg + off, SUBLANES, stride=CONV_ROW_STRIDE), :] = accs[a]
    acc = jnp.concatenate([y_ref[s] for s in range(len(lane_slabs))], axis=1) + bias
    mu = jnp.mean(acc, axis=-1, keepdims=True)
    xc = acc - mu
    y = xc * lax.rsqrt(jnp.mean(xc * xc, axis=-1, keepdims=True) + NORM_EPS) * gain + beta
    return y * jax.nn.sigmoid(y)


def _post_kernel(xc_ref, xl_ref, dac_ref, dal_ref, gac_ref, gal_ref, uc_ref, ul_ref, uprev_ref, unext_ref,
                 mod_ref, n1_ref, n2_ref, fn_ref, cw_ref, cb_ref, cg_ref, cbeta_ref,
                 wg_ref, wda_ref, wga_ref, wco_ref, wo_ref, w1_ref, w2_ref, oc_ref, ol_ref, win_ref, y_ref,
                 *, last, seg, n_ctx_tiles, tiles_per_seq):
    step = pl.program_id(0)
    is_lat = step >= n_ctx_tiles
    pick = lambda c_ref, l_ref: jnp.where(is_lat, l_ref[...], c_ref[...])

    pos = jnp.maximum(step - n_ctx_tiles, 0) % tiles_per_seq
    head = jnp.where(is_lat & (pos > 0), uprev_ref[...], 0.0)
    tail = jnp.where(is_lat & (pos < tiles_per_seq - 1), unext_ref[...], 0.0)
    ca = _conv_branch(pick(uc_ref, ul_ref), head, tail, is_lat, cw_ref, cb_ref[...], cg_ref[...], cbeta_ref[...],
                      win_ref, y_ref, seg)

    x = pick(xc_ref, xl_ref)
    mod = mod_ref[0, 0]
    m = lambda k: mod[:, k * D_MODEL:(k + 1) * D_MODEL]
    h = _modulated_norm(x, n1_ref[...], m(0), m(1))

    branches = ((pick(dac_ref, dal_ref), wda_ref), (pick(gac_ref, gal_ref), wga_ref), (ca.astype(BF16), wco_ref))
    merged = None
    for j, (act, w_ref) in enumerate(branches):
        gate = jax.nn.sigmoid(_dot(h, wg_ref[:, j * D_MODEL:(j + 1) * D_MODEL]))
        term = gate * _dot(act, w_ref[...])
        merged = term if merged is None else merged + term
    x = x + m(2) * _dot(merged.astype(BF16), wo_ref[...])

    h2 = _modulated_norm(x, n2_ref[...], m(3), m(4))
    f = None
    for c in range(0, MLP_HIDDEN, MLP_CHUNK):
        a = jnp.maximum(_dot(h2, w1_ref[:, c:c + MLP_CHUNK]), 0.0)
        term = _dot((a * a).astype(BF16), w2_ref[c:c + MLP_CHUNK, :])
        f = term if f is None else f + term
    x = x + m(5) * f
    out = _rms(x, fn_ref[...]) if last else x

    @pl.when(jnp.logical_not(is_lat))
    def _():
        oc_ref[...] = out

    @pl.when(is_lat)
    def _():
        ol_ref[...] = out


def _post(xs, das, gas, us, mod, layer, norm1, norm2, final_norm, conv_params, weights, *, seqs, tile, last):
    seq_c, seq_l = seqs
    assert tile % seq_c == 0 and seq_l % tile == 0
    n_c, n_l = xs[0].shape[0] // tile, xs[1].shape[0] // tile
    tiles_per_seq = seq_l // tile
    ctx_i = lambda i: jnp.minimum(i, n_c - 1)
    lat_i = lambda i: jnp.maximum(i - n_c, 0)
    pair = lambda w: [pl.BlockSpec((tile, w), lambda i: (ctx_i(i), 0)), pl.BlockSpec((tile, w), lambda i: (lat_i(i), 0))]
    per_tile, n_halo = tile // CONV_HALO, us[1].shape[0] // CONV_HALO
    row = lambda i: jnp.where(i < n_c, 0, 1 + lat_i(i) // tiles_per_seq)
    in_specs = pair(D_MODEL) + pair(DIFF_WIDTH) + pair(GQA_WIDTH) + pair(CONV_WIDTH)
    in_specs += [pl.BlockSpec((CONV_HALO, CONV_WIDTH), lambda i: (jnp.maximum(lat_i(i) * per_tile - 1, 0), 0)),
                 pl.BlockSpec((CONV_HALO, CONV_WIDTH),
                              lambda i: (jnp.minimum((lat_i(i) + 1) * per_tile, n_halo - 1), 0)),
                 pl.BlockSpec((1, 1, 1, N_MOD * D_MODEL), lambda i: (layer, row(i), 0, 0)),
                 _resident((1, D_MODEL)), _resident((1, D_MODEL)), _resident((1, D_MODEL))]
    in_specs += [_resident(p.shape) for p in conv_params] + [_resident(w.shape) for w in weights]
    n_slab = CONV_WIDTH // LANES
    return pl.pallas_call(
        functools.partial(_post_kernel, last=last, seg=seq_c, n_ctx_tiles=n_c, tiles_per_seq=tiles_per_seq),
        grid=(n_c + n_l,),
        in_specs=in_specs,
        out_specs=pair(D_MODEL),
        out_shape=[jax.ShapeDtypeStruct(x.shape, F32) for x in xs],
        scratch_shapes=[pltpu.VMEM((n_slab, (tile // seq_c) * (seq_c + 2 * CONV_HALO), LANES), F32),
                        pltpu.VMEM((n_slab, tile, LANES), F32)],
        compiler_params=_params(1),
        name="post",
    )(*xs, *das, *gas, *us, us[1], us[1], mod, norm1, norm2, final_norm, *conv_params, *weights)


def _rope_tables(n_tokens):
    n_rows = n_tokens // GRID_W
    row = jnp.repeat(jnp.arange(n_rows), GRID_W).astype(F32)
    col = jnp.tile(jnp.arange(GRID_W), n_rows).astype(F32)
    axis_dim = HEAD_DIM // 2
    freqs = ROPE_THETA ** (-jnp.arange(0, axis_dim, 2, dtype=F32) / axis_dim)
    ang_r = row[:, None] * freqs[None, :]
    ang_c = col[:, None] * freqs[None, :]
    ang = jnp.concatenate([ang_r, ang_r, ang_c, ang_c], axis=-1)
    sign = jnp.tile(jnp.repeat(jnp.array([-1.0, 1.0], F32), HEAD_DIM // 4), 2)
    return jnp.tile(jnp.cos(ang), (1, 2)), jnp.tile(jnp.sin(ang) * sign, (1, 2))


def kernel(x_prompt, x_sample, cache_diff_k, cache_diff_v, cache_gqa_k, cache_gqa_v, c, c_ctx, w_ada, b_ada, norm1, norm2, w_in, diff_lq1, diff_lk1, diff_lq2, diff_lk2, diff_subln, w_diff_o, gqa_q_norm, gqa_k_norm, w_gqa_o, conv_dw, conv_dw_b, conv_ln_g, conv_ln_b, w_conv_o, w_o, w_mlp1, w_mlp2, final_norm):
    n_ctx, s_ctx, _ = x_prompt.shape
    n_lat, s_lat, _ = x_sample.shape
    past = cache_diff_k.shape[2]
    assert n_lat + 1 <= MOD_ROWS

    cvecs = jnp.concatenate([c_ctx[None], c, jnp.zeros((MOD_ROWS - 1 - n_lat, D_MODEL), F32)], axis=0)
    mod = _modulation(cvecs, w_ada, b_ada).reshape(DEPTH, MOD_ROWS, 1, N_MOD * D_MODEL)

    caches = (cache_diff_k.reshape(n_lat, DEPTH, past, DIFF_WIDTH), cache_diff_v.reshape(n_lat, DEPTH, past, DIFF_WIDTH),
              cache_gqa_k.reshape(n_lat, DEPTH, past, GQA_KV_WIDTH),
              cache_gqa_v.reshape(n_lat, DEPTH, past, GQA_KV_WIDTH))
    rope_tabs = _rope_tables(s_lat)
    row_vec = lambda p: p.reshape(1, -1)
    fn = row_vec(final_norm)

    groups = (dict(latent=False, n_seq=n_ctx, seq=s_ctx, tile=TOKEN_TILE, q_tile=s_ctx, seqs_per_step=1),
              dict(latent=True, n_seq=n_lat, seq=s_lat, tile=TOKEN_TILE, q_tile=LATENT_Q_TILE, seqs_per_step=1))
    xs = [x_prompt.reshape(n_ctx * s_ctx, D_MODEL), x_sample.reshape(n_lat * s_lat, D_MODEL)]
    new_cache = None

    mix_cols, gate_cols = (0, MIX_WIDTH), (MIX_WIDTH, MIX_WIDTH + GATE_WIDTH)
    whole = lambda w: ((0, w.shape[2]),)
    w_mix, w_gate = w_in, None

    for l in range(DEPTH):
        qn = row_vec(jnp.tile(gqa_q_norm[l], 2))
        kn = row_vec(jnp.tile(gqa_k_norm[l], 2))
        lam_params = jnp.stack([diff_lq1[l], diff_lk1[l], diff_lq2[l], diff_lk2[l]])
        n1, n2 = row_vec(norm1[l]), row_vec(norm2[l])
        conv_params = [conv_dw[l], row_vec(conv_dw_b[l]), row_vec(conv_ln_g[l]), row_vec(conv_ln_b[l])]
        next_in = [(w_in, l + 1, (mix_cols, gate_cols))] if l + 1 < DEPTH else []
        this_gate = [(w_in, l, (gate_cols,))] if w_gate is None else []
        pre_casts = ([(w_mlp1, l, whole(w_mlp1))], [(w_mlp2, l, whole(w_mlp2))])
        attn_casts = ([(w, l, whole(w)) for w in (w_diff_o, w_gqa_o, w_conv_o, w_o)] + this_gate + next_in, [])

        pre_outs = []
        for gi, g in enumerate(groups):
            outs = _pre(xs[gi], mod, l, n1, w_mix, qn, kn, rope_tabs, new_cache, pre_casts[gi],
                        latent=g["latent"], seq=g["seq"], tile=g["tile"])
            if not g["latent"]:
                new_cache = outs[7:11]
            pre_outs.append(outs)
        attn_outs = [_attention(pre_outs[gi][:6], caches, l, lam_params, row_vec(diff_subln[l]), attn_casts[gi],
                                latent=g["latent"], n_seq=g["n_seq"], seq=g["seq"], q_tile=g["q_tile"],
                                seqs_per_step=g["seqs_per_step"]) for gi, g in enumerate(groups)]
        w_da_b, w_ga_b, w_co_b, w_o_b, *ctx_converted = attn_outs[0][2:]
        if this_gate:
            w_gate = ctx_converted.pop(0)
        post_w = [w_gate, w_da_b, w_ga_b, w_co_b, w_o_b, pre_outs[0][-1], pre_outs[1][-1]]
        next_in_b = ctx_converted
        xs = list(_post(xs, [a[0] for a in attn_outs], [a[1] for a in attn_outs], [p[6] for p in pre_outs], mod, l,
                        n1, n2, fn, conv_params, post_w, seqs=(s_ctx, s_lat), tile=TOKEN_TILE,
                        last=(l == DEPTH - 1)))
        if next_in_b:
            w_mix, w_gate = next_in_b

    ndk, ndv, ngk, ngv = new_cache
    lead = (n_ctx, DEPTH, s_ctx)
    return (xs[0].reshape(n_ctx, s_ctx, D_MODEL), xs[1].reshape(n_lat, s_lat, D_MODEL),
            ndk.reshape(lead + (DIFF_HEADS, 2, HEAD_DIM)), ndv.reshape(lead + (DIFF_HEADS, 2 * HEAD_DIM)),
            ngk.reshape(lead + (GQA_KV_HEADS, HEAD_DIM)), ngv.reshape(lead + (GQA_KV_HEADS, HEAD_DIM)))
```

```python
import functools
import math

import jax
import jax.numpy as jnp
from jax import lax
from jax.experimental import pallas as pl
from jax.experimental.pallas import tpu as pltpu

D_MODEL = 1024
DEPTH = 2
GRID_W = 64
ROPE_THETA = 10000.0
NORM_EPS = 1e-6

DIFF_HEADS = 4
HEAD_DIM = 64
DIFF_WIDTH = 512
GQA_KV_HEADS = 2
GQA_WIDTH = 512
GQA_KV_WIDTH = 128
CONV_WIDTH = 512
CONV_KSIZE = 31
CONV_HALO = 16
N_BRANCH = 3
MLP_HIDDEN = 4 * D_MODEL
N_MOD = 6

MIX_WIDTH = 3 * 512 + 512 + 2 * 128 + 2 * CONV_WIDTH
GATE_WIDTH = N_BRANCH * D_MODEL

LANES = 128
SUBLANES = 8
CONV_ROW_STRIDE = 4
MXU_WIDTH = 256
MOD_ROWS = 8
MOD_COL_TILE = 3072
TOKEN_TILE = 512
LATENT_Q_TILE = 512
MLP_CHUNK = 2048
VMEM_LIMIT = 61 * 1024 * 1024

F32 = jnp.float32
BF16 = jnp.bfloat16


def _dot(a, b):
    return jnp.dot(a, b, preferred_element_type=F32)


def _dot_nt(a, b):
    return lax.dot_general(a, b, (((1,), (1,)), ((), ())), preferred_element_type=F32)


def _rms(x, gain):
    return x * lax.rsqrt(jnp.mean(x * x, axis=-1, keepdims=True) + NORM_EPS) * gain


def _modulated_norm(x, gain, shift, scale):
    return (_rms(x, gain) * (1.0 + scale) + shift).astype(BF16)


def _lane_iota(shape):
    return lax.broadcasted_iota(jnp.int32, shape, len(shape) - 1)


def _resident(shape):
    nd = len(shape)
    return pl.BlockSpec(shape, lambda *_: (0,) * nd, pipeline_mode=pl.Buffered(1))


def _params(n_axes):
    return pltpu.CompilerParams(dimension_semantics=("arbitrary",) * n_axes,
                                vmem_limit_bytes=VMEM_LIMIT)


def _mod_kernel(c_ref, w_ref, b_ref, o_ref):
    c = c_ref[...]
    s = (c * jax.nn.sigmoid(c)).astype(BF16)
    o_ref[0] = _dot(s, w_ref[0].astype(BF16)) + b_ref[0]


def _modulation(cvecs, w_ada, b_ada):
    width = N_MOD * D_MODEL
    tn = MOD_COL_TILE
    assert width % tn == 0
    return pl.pallas_call(
        _mod_kernel,
        grid=(DEPTH, width // tn),
        in_specs=[pl.BlockSpec((MOD_ROWS, D_MODEL), lambda l, j: (0, 0)),
                  pl.BlockSpec((1, D_MODEL, tn), lambda l, j: (l, 0, j)),
                  pl.BlockSpec((1, 1, tn), lambda l, j: (l, 0, j))],
        out_specs=pl.BlockSpec((1, MOD_ROWS, tn), lambda l, j: (l, 0, j)),
        out_shape=jax.ShapeDtypeStruct((DEPTH, MOD_ROWS, width), F32),
        compiler_params=_params(2),
        name="adaln_mod",
    )(cvecs, w_ada, b_ada.reshape(DEPTH, 1, width))


def _rope(x, cos, sin_signed, first_half):
    rot = jnp.where(first_half, pltpu.roll(x, LANES - 16, 1), pltpu.roll(x, 16, 1))
    return x * cos + rot * sin_signed


def _head_mean_sq(x):
    width = min(x.shape[1], MXU_WIDTH)
    r = lax.broadcasted_iota(jnp.int32, (width, width), 0) // HEAD_DIM
    c = lax.broadcasted_iota(jnp.int32, (width, width), 1) // HEAD_DIM
    blockdiag = jnp.where(r == c, 1.0 / HEAD_DIM, 0.0).astype(BF16)
    parts = []
    for lo in range(0, x.shape[1], width):
        sq = x[:, lo:lo + width] * x[:, lo:lo + width]
        parts.append(_dot(sq.astype(BF16), blockdiag))
    return parts[0] if len(parts) == 1 else jnp.concatenate(parts, axis=1)


def _store_cache(ref, val, layer):
    seq = ref.shape[2]
    own = layer if ref.shape[1] > 1 else 0
    for b in range(ref.shape[0]):
        for slot in range(ref.shape[1]):
            ref[b, slot] = val[b * seq:(b + 1) * seq] if slot == own else jnp.zeros((seq, ref.shape[3]), ref.dtype)


def _cast_specs(casts, n_steps, step_of):
    in_specs, args, out_specs, out_shape = [], [], [], []
    for w, w_layer, splits in casts:
        _, n_rows, n_cols = w.shape
        blk = n_rows // n_steps
        assert n_rows % n_steps == 0 and blk % 16 == 0
        in_specs.append(pl.BlockSpec((1, blk, n_cols), lambda *g, w_layer=w_layer: (w_layer, step_of(*g), 0)))
        args.append(w)
        for lo, hi in splits:
            out_specs.append(pl.BlockSpec((blk, hi - lo), lambda *g: (step_of(*g), 0)))
            out_shape.append(jax.ShapeDtypeStruct((n_rows, hi - lo), BF16))
    return in_specs, args, out_specs, out_shape, tuple(tuple(s) for _, _, s in casts)


def _run_casts(in_refs, out_refs, cast_splits):
    out_refs = iter(out_refs)
    for w_ref, splits in zip(in_refs, cast_splits):
        for lo, hi in splits:
            next(out_refs)[...] = w_ref[0, :, lo:hi].astype(BF16)


def _pre_kernel(*refs, latent, layer, n_aliased, f32_weights, cast_splits):
    if f32_weights:
        *refs, w_bf16_ref = refs
    n_in = 8 if latent else 6
    n_out = 7 if latent else 11
    cast_in = refs[n_in + n_aliased:n_in + n_aliased + len(cast_splits)]
    outs = refs[n_in + n_aliased + len(cast_splits):]
    _run_casts(cast_in, outs[n_out:], cast_splits)
    if latent:
        x_ref, mod_ref, n1_ref, w_ref, qn_ref, kn_ref, cos_ref, sin_ref = refs[:n_in]
        qd_ref, kd_ref, vd_ref, qg_ref, kk_ref, vv_ref, u_ref = outs[:n_out]
    else:
        x_ref, mod_ref, n1_ref, w_ref, qn_ref, kn_ref = refs[:n_in]
        (qd_ref, kd_ref, vd_ref, qg_ref, kk_ref, vv_ref, u_ref,
         ndk_ref, ndv_ref, ngk_ref, ngv_ref) = outs[:n_out]

    if f32_weights:
        @pl.when(pl.program_id(0) == 0)
        def _():
            w_bf16_ref[...] = w_ref[0].astype(BF16)
        w_ref = w_bf16_ref

    mod = mod_ref[0, 0]
    h = _modulated_norm(x_ref[...], n1_ref[...], mod[:, 0:D_MODEL], mod[:, D_MODEL:2 * D_MODEL])
    rows = h.shape[0]

    lane = _lane_iota((rows, LANES))
    low_half = lane < HEAD_DIM
    if latent:
        cos, sin_signed = cos_ref[...], sin_ref[...]
        first_half = ((lane % HEAD_DIM) // 16) % 2 == 0
        rope = lambda t: _rope(t, cos, sin_signed, first_half)
    else:
        rope = lambda t: t

    qk_scale = HEAD_DIM ** -0.5 * math.log2(math.e)

    def proj(lo, width):
        return _dot(h, w_ref[:, lo:lo + width])

    def slabs(t):
        return [t[:, s:s + LANES] for s in range(0, t.shape[1], LANES)]

    def store_slabs(ref, parts):
        for s, p in enumerate(parts):
            ref[:, s * LANES:(s + 1) * LANES] = p.astype(ref.dtype)

    store_slabs(qd_ref, [rope(t) * qk_scale for t in slabs(proj(0, DIFF_WIDTH))])
    dk = proj(DIFF_WIDTH, DIFF_WIDTH)
    dv = proj(2 * DIFF_WIDTH, DIFF_WIDTH)
    if not latent:
        _store_cache(ndk_ref, dk, layer)
        _store_cache(ndv_ref, dv, layer)
    store_slabs(kd_ref, [rope(t) for t in slabs(dk)])
    vd_ref[...] = dv.astype(BF16)

    gq = proj(3 * DIFF_WIDTH, GQA_WIDTH)
    gq = gq * lax.rsqrt(_head_mean_sq(gq) + NORM_EPS)
    qn = qn_ref[...]
    store_slabs(qg_ref, [rope(t * qn) * qk_scale for t in slabs(gq)])

    gkv = proj(3 * DIFF_WIDTH + GQA_WIDTH, 2 * GQA_KV_WIDTH)
    k, v = gkv[:, :GQA_KV_WIDTH], gkv[:, GQA_KV_WIDTH:]
    k = k * lax.rsqrt(_head_mean_sq(k) + NORM_EPS) * kn_ref[...]
    if not latent:
        _store_cache(ngk_ref, k, layer)
        _store_cache(ngv_ref, v, layer)
    k = rope(k)
    k_sw, v_sw = pltpu.roll(k, HEAD_DIM, 1), pltpu.roll(v, HEAD_DIM, 1)
    store_slabs(kk_ref, [jnp.where(low_half, k, k_sw), jnp.where(low_half, k_sw, k)])
    store_slabs(vv_ref, [jnp.where(low_half, v, v_sw), jnp.where(low_half, v_sw, v)])

    cv = proj(3 * DIFF_WIDTH + GQA_WIDTH + 2 * GQA_KV_WIDTH, 2 * CONV_WIDTH)
    u_ref[...] = cv[:, :CONV_WIDTH] * jax.nn.sigmoid(cv[:, CONV_WIDTH:])


def _pre(x2d, mod, layer, norm1, w_mix, qn, kn, rope_tabs, new_cache, casts, *, latent, seq, tile):
    n_tok = x2d.shape[0]
    tiles_per_seq = seq // tile if latent else 1
    row = (lambda i: 1 + i // tiles_per_seq) if latent else (lambda i: 0)
    tok = lambda w: pl.BlockSpec((tile, w), lambda i: (i, 0))
    f32_weights = w_mix.dtype == F32
    w_spec = (pl.BlockSpec((1, D_MODEL, MIX_WIDTH), lambda i: (layer, 0, 0), pipeline_mode=pl.Buffered(1))
              if f32_weights else _resident((D_MODEL, MIX_WIDTH)))
    in_specs = [tok(D_MODEL),
                pl.BlockSpec((1, 1, 1, N_MOD * D_MODEL), lambda i: (layer, row(i), 0, 0)),
                _resident((1, D_MODEL)), w_spec,
                _resident((1, LANES)), _resident((1, LANES))]
    args = [x2d, mod, norm1, w_mix, qn, kn]
    widths = [(DIFF_WIDTH, BF16), (DIFF_WIDTH, BF16), (DIFF_WIDTH, BF16), (GQA_WIDTH, BF16),
              (GQA_KV_HEADS * LANES, BF16), (GQA_KV_HEADS * LANES, BF16), (CONV_WIDTH, F32)]
    out_specs = [tok(w) for w, _ in widths]
    out_shape = [jax.ShapeDtypeStruct((n_tok, w), dt) for w, dt in widths]
    aliases = {}
    if latent:
        in_specs += [pl.BlockSpec((tile, LANES), lambda i: (i % tiles_per_seq, 0))] * 2
        args += list(rope_tabs)
    else:
        per_tile = tile // seq
        for k, w in enumerate((DIFF_WIDTH, DIFF_WIDTH, GQA_KV_WIDTH, GQA_KV_WIDTH)):
            if new_cache is None:
                out_specs.append(pl.BlockSpec((per_tile, DEPTH, seq, w), lambda i: (i, 0, 0, 0)))
            else:
                aliases[len(args)] = len(out_shape)
                in_specs.append(pl.BlockSpec(memory_space=pl.ANY))
                args.append(new_cache[k])
                out_specs.append(pl.BlockSpec((per_tile, 1, seq, w), lambda i: (i, layer, 0, 0)))
            out_shape.append(jax.ShapeDtypeStruct((n_tok // seq, DEPTH, seq, w), F32))
    c_in, c_args, c_out, c_shape, cast_splits = _cast_specs(casts, n_tok // tile, lambda i: i)
    in_specs, args, out_specs, out_shape = in_specs + c_in, args + c_args, out_specs + c_out, out_shape + c_shape
    return pl.pallas_call(
        functools.partial(_pre_kernel, latent=latent, layer=layer, n_aliased=len(aliases),
                          f32_weights=f32_weights, cast_splits=cast_splits),
        grid=(n_tok // tile,),
        in_specs=in_specs,
        out_specs=out_specs,
        out_shape=out_shape,
        scratch_shapes=[pltpu.VMEM((D_MODEL, MIX_WIDTH), BF16)] if f32_weights else [],
        input_output_aliases=aliases,
        compiler_params=_params(1),
        name="pre_latent" if latent else "pre_ctx",
    )(*args)


def _attend(q, keys, values):
    scores = [_dot_nt(q, k) for k in keys]
    m = functools.reduce(jnp.maximum, [jnp.max(s, axis=-1, keepdims=True) for s in scores])
    return functools.reduce(jnp.add, [_dot(jnp.exp2(s - m).astype(BF16), v) for s, v in zip(scores, values)])


def _attend_rowsum(q, keys, values):
    scores = [_dot_nt(q, k) for k in keys]
    m = functools.reduce(jnp.maximum, [jnp.max(s, axis=-1, keepdims=True) for s in scores])
    exps = [jnp.exp2(s - m) for s in scores]
    denom = functools.reduce(jnp.add, [jnp.sum(e, axis=-1, keepdims=True) for e in exps])
    out = functools.reduce(jnp.add, [_dot(e.astype(BF16), v) for e, v in zip(exps, values)])
    return out / denom


def _attn_kernel(*refs, latent, lam_init, cast_splits, n_seqs):
    n_core = 12 if latent else 8
    if latent:
        (qd_ref, kd_ref, vd_ref, qg_ref, kk_ref, vv_ref, cdk_ref, cdv_ref, cgk_ref, cgv_ref,
         lp_ref, sub_ref) = refs[:n_core]
    else:
        qd_ref, kd_ref, vd_ref, qg_ref, kk_ref, vv_ref, lp_ref, sub_ref = refs[:n_core]
    n_in = n_core + len(cast_splits)
    da_ref, ga_ref = refs[n_in:n_in + 2]
    _run_casts(refs[n_core:n_in], refs[n_in + 2:], cast_splits)

    rows = qd_ref.shape[0] // n_seqs
    kv_rows = kd_ref.shape[0] // n_seqs
    low_half = _lane_iota((rows, LANES)) < HEAD_DIM
    zero = jnp.zeros((), BF16)

    lp = lp_ref[...]
    lam = (jnp.exp(jnp.sum(lp[0:1] * lp[1:2], axis=-1, keepdims=True))
           - jnp.exp(jnp.sum(lp[2:3] * lp[3:4], axis=-1, keepdims=True)) + lam_init)
    sub_gain = sub_ref[...] * (1.0 - lam_init)

    def with_ones(v):
        return jnp.concatenate([v, jnp.ones_like(v)], axis=1)

    def value_slabs(v_dup):
        low = _lane_iota(v_dup.shape) < HEAD_DIM
        one = jnp.ones((), v_dup.dtype)
        return jnp.where(low, v_dup, one), jnp.where(low, one, v_dup)

    if latent:
        ck, cv = cgk_ref[0, 0], cgv_ref[0, 0]
        ck_sw, cv_sw = pltpu.roll(ck, HEAD_DIM, 1), pltpu.roll(cv, HEAD_DIM, 1)
        low_c = _lane_iota(ck.shape) < HEAD_DIM
        cache_k = [jnp.where(low_c, ck, ck_sw).astype(BF16), jnp.where(low_c, ck_sw, ck).astype(BF16)]
        cache_v = [value_slabs(jnp.where(low_c, cv, cv_sw).astype(BF16)),
                   value_slabs(jnp.where(low_c, cv_sw, cv).astype(BF16))]
    for s in range(n_seqs):
        qs = slice(s * rows, (s + 1) * rows)
        ks = slice(s * kv_rows, (s + 1) * kv_rows)
        for h in range(DIFF_HEADS):
            sl = slice(h * LANES, (h + 1) * LANES)
            q = qd_ref[qs, sl]
            q1, q2 = jnp.where(low_half, q, zero), jnp.where(low_half, zero, q)
            if latent:
                keys = [kd_ref[ks, sl], cdk_ref[0, 0, :, sl].astype(BF16)]
                values = [with_ones(vd_ref[ks, sl]), with_ones(cdv_ref[0, 0, :, sl].astype(BF16))]
                r1 = _attend(q1, keys, values)
                r2 = _attend(q2, keys, values)
                o = r1[:, :LANES] / r1[:, LANES:] - lam * (r2[:, :LANES] / r2[:, LANES:])
            else:
                keys, values = [kd_ref[ks, sl]], [vd_ref[ks, sl]]
                o = _attend_rowsum(q1, keys, values) - lam * _attend_rowsum(q2, keys, values)
            da_ref[qs, sl] = _rms(o, sub_gain).astype(BF16)

        for n in range(GQA_KV_HEADS):
            kv_sl = slice(n * LANES, (n + 1) * LANES)
            keys = [kk_ref[ks, kv_sl]]
            if latent:
                keys.append(cache_k[n])
                values = [value_slabs(vv_ref[ks, kv_sl]), cache_v[n]]
            for j in range(2):
                sl = slice((2 * n + j) * LANES, (2 * n + j + 1) * LANES)
                q = qg_ref[qs, sl]
                q_even, q_odd = jnp.where(low_half, q, zero), jnp.where(low_half, zero, q)
                if latent:
                    r_even = _attend(q_even, keys, [v[0] for v in values])
                    r_odd = _attend(q_odd, keys, [v[1] for v in values])
                    r = jnp.where(low_half, r_even, r_odd)
                    denom = jnp.where(low_half, pltpu.roll(r_even, HEAD_DIM, 1), pltpu.roll(r_odd, HEAD_DIM, 1))
                    out = r / denom
                else:
                    v_dup = [vv_ref[ks, kv_sl]]
                    out = jnp.where(low_half, _attend_rowsum(q_even, keys, v_dup), _attend_rowsum(q_odd, keys, v_dup))
                ga_ref[qs, sl] = out.astype(BF16)


def _attention(pre_outs, caches, layer, lam_params, subln, casts, *, latent, n_seq, seq, q_tile, seqs_per_step):
    qd, kd, vd, qg, kk, vv = pre_outs
    tiles = seq // q_tile
    assert seqs_per_step == 1 or tiles == 1
    n_seq //= seqs_per_step
    n_steps = n_seq * tiles
    q_spec = lambda w: pl.BlockSpec((seqs_per_step * q_tile, w), lambda b, i: (b * tiles + i, 0))
    kv_spec = lambda w: pl.BlockSpec((seqs_per_step * seq, w), lambda b, i: (b, 0))
    kv_dup = GQA_KV_HEADS * LANES
    in_specs = [q_spec(DIFF_WIDTH), kv_spec(DIFF_WIDTH), kv_spec(DIFF_WIDTH), q_spec(GQA_WIDTH),
                kv_spec(kv_dup), kv_spec(kv_dup)]
    args = [qd, kd, vd, qg, kk, vv]
    if latent:
        for c in caches:
            in_specs.append(pl.BlockSpec((1, 1) + c.shape[2:], lambda b, i: (b, layer, 0, 0)))
            args.append(c)
    in_specs += [_resident((4, HEAD_DIM)), _resident((1, LANES))]
    args += [lam_params, subln]
    out_specs = [q_spec(DIFF_WIDTH), q_spec(GQA_WIDTH)]
    out_shape = [jax.ShapeDtypeStruct(qd.shape, BF16)] * 2
    c_in, c_args, c_out, c_shape, cast_splits = _cast_specs(casts, n_steps, lambda b, i: b * tiles + i)
    in_specs, args, out_specs, out_shape = in_specs + c_in, args + c_args, out_specs + c_out, out_shape + c_shape
    lam_init = 0.8 - 0.6 * math.exp(-0.3 * layer)
    return pl.pallas_call(
        functools.partial(_attn_kernel, latent=latent, lam_init=lam_init, cast_splits=cast_splits,
                          n_seqs=seqs_per_step),
        grid=(n_seq, tiles),
        in_specs=in_specs,
        out_specs=out_specs,
        out_shape=out_shape,
        compiler_params=_params(2),
        name="attn_latent" if latent else "attn_ctx",
    )(*args)


def _conv_branch(u, head, tail, joined, w_ref, bias, gain, beta, win_ref, y_ref, seg):
    n_seg = u.shape[0] // seg
    pitch = seg + 2 * CONV_HALO
    lane_slabs = [slice(s * LANES, (s + 1) * LANES) for s in range(CONV_WIDTH // LANES)]
    for k in range(n_seg):
        lo, hi = k * seg, (k + 1) * seg
        before = head if k == 0 else jnp.where(joined, u[lo - CONV_HALO:lo], 0.0)
        after = tail if k == n_seg - 1 else jnp.where(joined, u[hi:hi + CONV_HALO], 0.0)
        for s, ls in enumerate(lane_slabs):
            win_ref[s, k * pitch:k * pitch + CONV_HALO, :] = before[:, ls]
            win_ref[s, k * pitch + CONV_HALO:(k + 1) * pitch - CONV_HALO, :] = u[lo:hi, ls]
            win_ref[s, (k + 1) * pitch - CONV_HALO:(k + 1) * pitch, :] = after[:, ls]

    first = CONV_HALO - CONV_KSIZE // 2
    group = 16
    rows_per_group = group * SUBLANES
    for k in range(n_seg):
        for s, ls in enumerate(lane_slabs):
            for r0 in range(0, seg, rows_per_group):
                offs = [r + t for r in range(r0, r0 + rows_per_group, SUBLANES * CONV_ROW_STRIDE)
                        for t in range(CONV_ROW_STRIDE)]
                accs = [jnp.zeros((SUBLANES, LANES), F32)] * group
                for j in range(CONV_KSIZE):
                    w_tap = jnp.broadcast_to(w_ref[j:j + 1, ls], (SUBLANES, LANES))
                    for a, off in enumerate(offs):
                        start = k * pitch + off + first + j
                        accs[a] = accs[a] + win_ref[s, pl.ds(start, SUBLANES, stride=CONV_ROW_STRIDE), :] * w_tap
                for a, off in enumerate(offs):
                    y_ref[s, pl.ds(k * seg + off, SUBLANES, stride=CONV_ROW_STRIDE), :] = accs[a]
    acc = jnp.concatenate([y_ref[s] for s in range(len(lane_slabs))], axis=1) + bias
    mu = jnp.mean(acc, axis=-1, keepdims=True)
    xc = acc - mu
    y = xc * lax.rsqrt(jnp.mean(xc * xc, axis=-1, keepdims=True) + NORM_EPS) * gain + beta
    return y * jax.nn.sigmoid(y)


def _post_kernel(xc_ref, xl_ref, dac_ref, dal_ref, gac_ref, gal_ref, uc_ref, ul_ref, uprev_ref, unext_ref,
                 mod_ref, n1_ref, n2_ref, fn_ref, cw_ref, cb_ref, cg_ref, cbeta_ref,
                 wg_ref, wda_ref, wga_ref, wco_ref, wo_ref, w1_ref, w2_ref, oc_ref, ol_ref, win_ref, y_ref,
                 *, last, seg, n_ctx_tiles, tiles_per_seq):
    step = pl.program_id(0)
    is_lat = step >= n_ctx_tiles
    pick = lambda c_ref, l_ref: jnp.where(is_lat, l_ref[...], c_ref[...])

    pos = jnp.maximum(step - n_ctx_tiles, 0) % tiles_per_seq
    head = jnp.where(is_lat & (pos > 0), uprev_ref[...], 0.0)
    tail = jnp.where(is_lat & (pos < tiles_per_seq - 1), unext_ref[...], 0.0)
    ca = _conv_branch(pick(uc_ref, ul_ref), head, tail, is_lat, cw_ref, cb_ref[...], cg_ref[...], cbeta_ref[...],
                      win_ref, y_ref, seg)

    x = pick(xc_ref, xl_ref)
    mod = mod_ref[0, 0]
    m = lambda k: mod[:, k * D_MODEL:(k + 1) * D_MODEL]
    h = _modulated_norm(x, n1_ref[...], m(0), m(1))

    branches = ((pick(dac_ref, dal_ref), wda_ref), (pick(gac_ref, gal_ref), wga_ref), (ca.astype(BF16), wco_ref))
    merged = None
    for j, (act, w_ref) in enumerate(branches):
        gate = jax.nn.sigmoid(_dot(h, wg_ref[:, j * D_MODEL:(j + 1) * D_MODEL]))
        term = gate * _dot(act, w_ref[...])
        merged = term if merged is None else merged + term
    x = x + m(2) * _dot(merged.astype(BF16), wo_ref[...])

    h2 = _modulated_norm(x, n2_ref[...], m(3), m(4))
    f = None
    for c in range(0, MLP_HIDDEN, MLP_CHUNK):
        a = jnp.maximum(_dot(h2, w1_ref[:, c:c + MLP_CHUNK]), 0.0)
        term = _dot((a * a).astype(BF16), w2_ref[c:c + MLP_CHUNK, :])
        f = term if f is None else f + term
    x = x + m(5) * f
    out = _rms(x, fn_ref[...]) if last else x

    @pl.when(jnp.logical_not(is_lat))
    def _():
        oc_ref[...] = out

    @pl.when(is_lat)
    def _():
        ol_ref[...] = out


def _post(xs, das, gas, us, mod, layer, norm1, norm2, final_norm, conv_params, weights, *, seqs, tile, last):
    seq_c, seq_l = seqs
    assert tile % seq_c == 0 and seq_l % tile == 0
    n_c, n_l = xs[0].shape[0] // tile, xs[1].shape[0] // tile
    tiles_per_seq = seq_l // tile
    ctx_i = lambda i: jnp.minimum(i, n_c - 1)
    lat_i = lambda i: jnp.maximum(i - n_c, 0)
    pair = lambda w: [pl.BlockSpec((tile, w), lambda i: (ctx_i(i), 0)), pl.BlockSpec((tile, w), lambda i: (lat_i(i), 0))]
    per_tile, n_halo = tile // CONV_HALO, us[1].shape[0] // CONV_HALO
    row = lambda i: jnp.where(i < n_c, 0, 1 + lat_i(i) // tiles_per_seq)
    in_specs = pair(D_MODEL) + pair(DIFF_WIDTH) + pair(GQA_WIDTH) + pair(CONV_WIDTH)
    in_specs += [pl.BlockSpec((CONV_HALO, CONV_WIDTH), lambda i: (jnp.maximum(lat_i(i) * per_tile - 1, 0), 0)),
                 pl.BlockSpec((CONV_HALO, CONV_WIDTH),
                              lambda i: (jnp.minimum((lat_i(i) + 1) * per_tile, n_halo - 1), 0)),
                 pl.BlockSpec((1, 1, 1, N_MOD * D_MODEL), lambda i: (layer, row(i), 0, 0)),
                 _resident((1, D_MODEL)), _resident((1, D_MODEL)), _resident((1, D_MODEL))]
    in_specs += [_resident(p.shape) for p in conv_params] + [_resident(w.shape) for w in weights]
    n_slab = CONV_WIDTH // LANES
    return pl.pallas_call(
        functools.partial(_post_kernel, last=last, seg=seq_c, n_ctx_tiles=n_c, tiles_per_seq=tiles_per_seq),
        grid=(n_c + n_l,),
        in_specs=in_specs,
        out_specs=pair(D_MODEL),
        out_shape=[jax.ShapeDtypeStruct(x.shape, F32) for x in xs],
        scratch_shapes=[pltpu.VMEM((n_slab, (tile // seq_c) * (seq_c + 2 * CONV_HALO), LANES), F32),
                        pltpu.VMEM((n_slab, tile, LANES), F32)],
        compiler_params=_params(1),
        name="post",
    )(*xs, *das, *gas, *us, us[1], us[1], mod, norm1, norm2, final_norm, *conv_params, *weights)


def _rope_tables(n_tokens):
    n_rows = n_tokens // GRID_W
    row = jnp.repeat(jnp.arange(n_rows), GRID_W).astype(F32)
    col = jnp.tile(jnp.arange(GRID_W), n_rows).astype(F32)
    axis_dim = HEAD_DIM // 2
    freqs = ROPE_THETA ** (-jnp.arange(0, axis_dim, 2, dtype=F32) / axis_dim)
    ang_r = row[:, None] * freqs[None, :]
    ang_c = col[:, None] * freqs[None, :]
    ang = jnp.concatenate([ang_r, ang_r, ang_c, ang_c], axis=-1)
    sign = jnp.tile(jnp.repeat(jnp.array([-1.0, 1.0], F32), HEAD_DIM // 4), 2)
    return jnp.tile(jnp.cos(ang), (1, 2)), jnp.tile(jnp.sin(ang) * sign, (1, 2))


def kernel(x_prompt, x_sample, cache_diff_k, cache_diff_v, cache_gqa_k, cache_gqa_v, c, c_ctx, w_ada, b_ada, norm1, norm2, w_in, diff_lq1, diff_lk1, diff_lq2, diff_lk2, diff_subln, w_diff_o, gqa_q_norm, gqa_k_norm, w_gqa_o, conv_dw, conv_dw_b, conv_ln_g, conv_ln_b, w_conv_o, w_o, w_mlp1, w_mlp2, final_norm):
    n_ctx, s_ctx, _ = x_prompt.shape
    n_lat, s_lat, _ = x_sample.shape
    past = cache_diff_k.shape[2]
    assert n_lat + 1 <= MOD_ROWS

    cvecs = jnp.concatenate([c_ctx[None], c, jnp.zeros((MOD_ROWS - 1 - n_lat, D_MODEL), F32)], axis=0)
    mod = _modulation(cvecs, w_ada, b_ada).reshape(DEPTH, MOD_ROWS, 1, N_MOD * D_MODEL)

    caches = (cache_diff_k.reshape(n_lat, DEPTH, past, DIFF_WIDTH), cache_diff_v.reshape(n_lat, DEPTH, past, DIFF_WIDTH),
              cache_gqa_k.reshape(n_lat, DEPTH, past, GQA_KV_WIDTH),
              cache_gqa_v.reshape(n_lat, DEPTH, past, GQA_KV_WIDTH))
    rope_tabs = _rope_tables(s_lat)
    row_vec = lambda p: p.reshape(1, -1)
    fn = row_vec(final_norm)

    groups = (dict(latent=False, n_seq=n_ctx, seq=s_ctx, tile=TOKEN_TILE, q_tile=s_ctx, seqs_per_step=1),
              dict(latent=True, n_seq=n_lat, seq=s_lat, tile=TOKEN_TILE, q_tile=LATENT_Q_TILE, seqs_per_step=1))
    xs = [x_prompt.reshape(n_ctx * s_ctx, D_MODEL), x_sample.reshape(n_lat * s_lat, D_MODEL)]
    new_cache = None

    mix_cols, gate_cols = (0, MIX_WIDTH), (MIX_WIDTH, MIX_WIDTH + GATE_WIDTH)
    whole = lambda w: ((0, w.shape[2]),)
    w_mix, w_gate = w_in, None

    for l in range(DEPTH):
        qn = row_vec(jnp.tile(gqa_q_norm[l], 2))
        kn = row_vec(jnp.tile(gqa_k_norm[l], 2))
        lam_params = jnp.stack([diff_lq1[l], diff_lk1[l], diff_lq2[l], diff_lk2[l]])
        n1, n2 = row_vec(norm1[l]), row_vec(norm2[l])
        conv_params = [conv_dw[l], row_vec(conv_dw_b[l]), row_vec(conv_ln_g[l]), row_vec(conv_ln_b[l])]
        next_in = [(w_in, l + 1, (mix_cols, gate_cols))] if l + 1 < DEPTH else []
        this_gate = [(w_in, l, (gate_cols,))] if w_gate is None else []
        pre_casts = ([(w_mlp1, l, whole(w_mlp1))], [(w_mlp2, l, whole(w_mlp2))])
        attn_casts = ([(w, l, whole(w)) for w in (w_diff_o, w_gqa_o, w_conv_o, w_o)] + this_gate + next_in, [])

        pre_outs = []
        for gi, g in enumerate(groups):
            outs = _pre(xs[gi], mod, l, n1, w_mix, qn, kn, rope_tabs, new_cache, pre_casts[gi],
                        latent=g["latent"], seq=g["seq"], tile=g["tile"])
            if not g["latent"]:
                new_cache = outs[7:11]
            pre_outs.append(outs)
        attn_outs = [_attention(pre_outs[gi][:6], caches, l, lam_params, row_vec(diff_subln[l]), attn_casts[gi],
                                latent=g["latent"], n_seq=g["n_seq"], seq=g["seq"], q_tile=g["q_tile"],
                                seqs_per_step=g["seqs_per_step"]) for gi, g in enumerate(groups)]
        w_da_b, w_ga_b, w_co_b, w_o_b, *ctx_converted = attn_outs[0][2:]
        if this_gate:
            w_gate = ctx_converted.pop(0)
        post_w = [w_gate, w_da_b, w_ga_b, w_co_b, w_o_b, pre_outs[0][-1], pre_outs[1][-1]]
        next_in_b = ctx_converted
        xs = list(_post(xs, [a[0] for a in attn_outs], [a[1] for a in attn_outs], [p[6] for p in pre_outs], mod, l,
                        n1, n2, fn, conv_params, post_w, seqs=(s_ctx, s_lat), tile=TOKEN_TILE,
                        last=(l == DEPTH - 1)))
        if next_in_b:
            w_mix, w_gate = next_in_b

    ndk, ndv, ngk, ngv = new_cache
    lead = (n_ctx, DEPTH, s_ctx)
    return (xs[0].reshape(n_ctx, s_ctx, D_MODEL), xs[1].reshape(n_lat, s_lat, D_MODEL),
            ndk.reshape(lead + (DIFF_HEADS, 2, HEAD_DIM)), ndv.reshape(lead + (DIFF_HEADS, 2 * HEAD_DIM)),
            ngk.reshape(lead + (GQA_KV_HEADS, HEAD_DIM)), ngv.reshape(lead + (GQA_KV_HEADS, HEAD_DIM)))
```
